```python
import jax, jax.numpy as jnp
from jax import lax
import numpy as np

D_MODEL = 2048
BATCH = 1
SEQ = 8192
DEPTH = 4

GRID_W = 64
CTX_LEN = 256
N_FG = 4
FG_W = 256
F_WIDTH = N_FG * FG_W
N_HEADS = 16
N_KV_HEADS = 2
HEAD_DIM = 64
Q_GROUP = N_HEADS // N_KV_HEADS
ATT_WIDTH = N_HEADS * HEAD_DIM
KV_WIDTH = N_KV_HEADS * HEAD_DIM
WINDOW = 128
BLOCK = 128
ROPE_BASE = 10000.0
FA_IN = F_WIDTH + ATT_WIDTH + 2 * KV_WIDTH
FA_OUT = F_WIDTH + ATT_WIDTH
D_RNN = D_MODEL
N_RNN_BLOCKS = 8
RNN_BLOCK = D_RNN // N_RNN_BLOCKS
CONV_W = 4
CONV_LEFT = 2
LRU_C = 8.0
D_FF = 5632
FFN_CONV_W = 3
FFN_CONV_LEFT = 1
N_MOD = 6
EPS = 1e-6
NEG_INF = -1e30
N_FA = (DEPTH + 1) // 2
N_RG = DEPTH // 2

kernel_name = 'hybrid_fourier_swa_rglru_convffn_dit'


def rmsnorm(x, g):
    xf = x.astype(jnp.float32)
    r = lax.rsqrt(jnp.mean(xf * xf, axis=-1, keepdims=True) + EPS)
    return (xf * r).astype(x.dtype) * g


def modulate(h, shift, scale):
    return h * (1 + scale) + shift


def dwconv(x, w, b, left):
    k_w = w.shape[0]
    n = x.shape[1]
    xp = jnp.pad(x, ((0, 0), (left, k_w - 1 - left), (0, 0)))
    y = xp[:, 0:n] * w[0] + b
    for k in range(1, k_w):
        y = y + xp[:, k:k + n] * w[k]
    return y


def _rope_axis(t, pos):
    f = t.shape[-1] // 2
    inv = ROPE_BASE ** (-jnp.arange(f, dtype=jnp.float32) / f)
    ang = pos.astype(jnp.float32)[:, None] * inv[None, :]
    cos = jnp.cos(ang)[:, None, :].astype(t.dtype)
    sin = jnp.sin(ang)[:, None, :].astype(t.dtype)
    t1, t2 = t[..., :f], t[..., f:]
    return jnp.concatenate([t1 * cos - t2 * sin, t1 * sin + t2 * cos], axis=-1)


def rope_2d(t, row_ids, col_ids):
    half = t.shape[-1] // 2
    return jnp.concatenate([_rope_axis(t[..., :half], row_ids), _rope_axis(t[..., half:], col_ids)], axis=-1)


def fourier_mix(u):
    b, n, _ = u.shape
    ug = u.reshape(b, n, N_FG, FG_W).astype(jnp.float32)
    y = jnp.fft.fft2(ug, axes=(1, 3), norm='ortho').real
    return y.reshape(b, n, F_WIDTH).astype(u.dtype)


def _split_fa(u):
    b, n = u.shape[:2]
    f = u[..., :F_WIDTH]
    q = u[..., F_WIDTH:F_WIDTH + ATT_WIDTH].reshape(b, n, N_HEADS, HEAD_DIM)
    k = u[..., F_WIDTH + ATT_WIDTH:F_WIDTH + ATT_WIDTH + KV_WIDTH].reshape(b, n, N_KV_HEADS, HEAD_DIM)
    v = u[..., F_WIDTH + ATT_WIDTH + KV_WIDTH:].reshape(b, n, N_KV_HEADS, HEAD_DIM)
    return f, q, k, v


def _window_mask(nb):
    q_pos = jnp.arange(nb)[:, None, None] * BLOCK + jnp.arange(BLOCK)[None, :, None]
    k_pos = (jnp.arange(nb)[:, None, None] - 1) * BLOCK + jnp.arange(3 * BLOCK)[None, None, :]
    return (jnp.abs(k_pos - q_pos) <= WINDOW) & (k_pos >= 0) & (k_pos < nb * BLOCK)


def _latent_attention(q, k, v, kc, vc, sink):
    b, s = q.shape[:2]
    nb = s // BLOCK
    n_ctx = kc.shape[1]
    qb = (q * HEAD_DIM ** -0.5).reshape(b, nb, BLOCK, N_KV_HEADS, Q_GROUP, HEAD_DIM)

    def bands(t):
        tp = jnp.pad(t, ((0, 0), (BLOCK, BLOCK), (0, 0), (0, 0))).reshape(b, nb + 2, BLOCK, N_KV_HEADS, HEAD_DIM)
        return jnp.concatenate([tp[:, :-2], tp[:, 1:-1], tp[:, 2:]], axis=2)

    kw, vw = bands(k), bands(v)
    s_win = jnp.einsum('bnqkgd,bnskd->bnkgqs', qb, kw).astype(jnp.float32)
    s_win = jnp.where(_window_mask(nb)[None, :, None, None], s_win, NEG_INF)
    s_ctx = jnp.einsum('bnqkgd,bckd->bnkgqc', qb, kc).astype(jnp.float32)
    sink_col = jnp.broadcast_to(sink.astype(jnp.float32).reshape(1, 1, N_KV_HEADS, Q_GROUP, 1, 1), s_win.shape[:-1] + (1,))
    p = jax.nn.softmax(jnp.concatenate([s_win, s_ctx, sink_col], axis=-1), axis=-1)
    p_win = p[..., :3 * BLOCK].astype(v.dtype)
    p_ctx = p[..., 3 * BLOCK:3 * BLOCK + n_ctx].astype(v.dtype)
    o = jnp.einsum('bnkgqs,bnskd->bnqkgd', p_win, vw) + jnp.einsum('bnkgqc,bckd->bnqkgd', p_ctx, vc)
    return o.reshape(b, s, ATT_WIDTH)


def _context_attention(qc, kc, vc, sink):
    b, n_ctx = qc.shape[:2]
    qs = (qc * HEAD_DIM ** -0.5).reshape(b, n_ctx, N_KV_HEADS, Q_GROUP, HEAD_DIM)
    s = jnp.einsum('bqkgd,bckd->bkgqc', qs, kc).astype(jnp.float32)
    sink_col = jnp.broadcast_to(sink.astype(jnp.float32).reshape(1, N_KV_HEADS, Q_GROUP, 1, 1), s.shape[:-1] + (1,))
    p = jax.nn.softmax(jnp.concatenate([s, sink_col], axis=-1), axis=-1)[..., :n_ctx].astype(vc.dtype)
    o = jnp.einsum('bkgqc,bckd->bqkgd', p, vc)
    return o.reshape(b, n_ctx, ATT_WIDTH)


def fourier_attn_mixer(h_lat, h_ctx, w_in, w_out, sink, row_ids, col_ids, ctx_out):
    f, q, k, v = _split_fa(h_lat @ w_in)
    fc, qc, kc, vc = _split_fa(h_ctx @ w_in)
    q = rope_2d(q, row_ids, col_ids)
    k = rope_2d(k, row_ids, col_ids)
    y_lat = jnp.concatenate([fourier_mix(f), _latent_attention(q, k, v, kc, vc, sink)], axis=-1) @ w_out
    y_ctx = None
    if ctx_out:
        y_ctx = jnp.concatenate([fourier_mix(fc), _context_attention(qc, kc, vc, sink)], axis=-1) @ w_out
    return y_lat, y_ctx


def _rglru_gates(xs, w_a, b_a, w_i, b_i, lam):
    b, n = xs.shape[:2]
    xb = xs.reshape(b, n, N_RNN_BLOCKS, RNN_BLOCK)
    r = jax.nn.sigmoid(jnp.einsum('blhi,hij->blhj', xb, w_a.astype(jnp.float32)).reshape(b, n, D_RNN) + b_a.astype(jnp.float32))
    i = jax.nn.sigmoid(jnp.einsum('blhi,hij->blhj', xb, w_i.astype(jnp.float32)).reshape(b, n, D_RNN) + b_i.astype(jnp.float32))
    log_a = -LRU_C * r * jax.nn.softplus(-lam.astype(jnp.float32))
    a = jnp.exp(log_a)
    gx = jnp.sqrt(-jnp.expm1(2.0 * log_a)) * (i * xs)
    return a, gx


def _linear_scan(a, gx, h0, reverse):
    def combine(e1, e2):
        a1, b1 = e1
        a2, b2 = e2
        return a1 * a2, a2 * b1 + b2
    a_cum, b_cum = lax.associative_scan(combine, (a, gx), reverse=reverse, axis=1)
    return b_cum + a_cum * h0[:, None, :]


def rglru_mixer(h_lat, h_ctx, w_in, conv_w, conv_b, w_a, b_a, w_i, b_i, lam, w_out, ctx_out):
    gate, xs = jnp.split(h_lat @ w_in, 2, axis=-1)
    xs_c = h_ctx @ w_in[:, D_RNN:]
    xs = dwconv(xs, conv_w, conv_b, CONV_LEFT).astype(jnp.float32)
    xs_c = dwconv(xs_c, conv_w, conv_b, CONV_LEFT).astype(jnp.float32)
    b = xs_c.shape[0]
    h_dirs, hc_dirs = [], []
    for d, reverse in enumerate((False, True)):
        a, gx = _rglru_gates(xs, w_a[d], b_a[d], w_i[d], b_i[d], lam[d])
        ac, gxc = _rglru_gates(xs_c, w_a[d], b_a[d], w_i[d], b_i[d], lam[d])
        hc = _linear_scan(ac, gxc, jnp.zeros((b, D_RNN), jnp.float32), reverse)
        h0 = hc[:, 0] if reverse else hc[:, -1]
        h_dirs.append(_linear_scan(a, gx, h0, reverse))
        hc_dirs.append(hc)
    y = (h_dirs[0] + h_dirs[1]).astype(h_lat.dtype) * jax.nn.gelu(gate)
    y_lat = y @ w_out
    y_ctx = None
    if ctx_out:
        gate_c = h_ctx @ w_in[:, :D_RNN]
        y_ctx = ((hc_dirs[0] + hc_dirs[1]).astype(h_ctx.dtype) * jax.nn.gelu(gate_c)) @ w_out
    return y_lat, y_ctx


def conv_ffn(h, w_up, conv_w, conv_b, w_down):
    u = dwconv(h @ w_up, conv_w, conv_b, FFN_CONV_LEFT)
    g, v = jnp.split(u, 2, axis=-1)
    return (jax.nn.silu(g) * v) @ w_down


def setup_inputs(seed: int = 0) -> dict:
    key = jax.random.key(seed)
    ks = jax.random.split(key, 32)
    f32 = jnp.float32

    def nrm(k, shape, scale):
        return jax.random.normal(k, shape, f32) * scale

    D = D_MODEL
    u = jax.random.uniform(ks[18], (N_RG, 2, D_RNN), f32, 0.9, 0.999)
    s = u ** (1.0 / LRU_C)
    return {
        'x': nrm(ks[0], (BATCH, SEQ, D), 1.0),
        'c': nrm(ks[1], (BATCH, D), 1.0),
        'ctx': nrm(ks[2], (BATCH, CTX_LEN, D), 1.0),
        'c_ctx': nrm(ks[3], (D,), 1.0),
        'w_mod': nrm(ks[4], (DEPTH, D, N_MOD * D), 0.5 * D ** -0.5),
        'b_mod': nrm(ks[5], (DEPTH, N_MOD * D), 0.02),
        'g_mix': 1.0 + nrm(ks[6], (DEPTH, D), 0.02),
        'g_ffn': 1.0 + nrm(ks[7], (DEPTH, D), 0.02),
        'fa_w_in': nrm(ks[8], (N_FA, D, FA_IN), D ** -0.5),
        'fa_w_out': nrm(ks[9], (N_FA, FA_OUT, D), FA_OUT ** -0.5),
        'attn_sink': nrm(ks[10], (N_FA, N_HEADS), 0.5),
        'rg_w_in': nrm(ks[11], (N_RG, D, 2 * D_RNN), D ** -0.5),
        'rg_conv_w': nrm(ks[12], (N_RG, CONV_W, D_RNN), CONV_W ** -0.5),
        'rg_conv_b': nrm(ks[13], (N_RG, D_RNN), 0.02),
        'rg_w_a': nrm(ks[14], (N_RG, 2, N_RNN_BLOCKS, RNN_BLOCK, RNN_BLOCK), RNN_BLOCK ** -0.5),
        'rg_b_a': nrm(ks[15], (N_RG, 2, D_RNN), 0.02),
        'rg_w_i': nrm(ks[16], (N_RG, 2, N_RNN_BLOCKS, RNN_BLOCK, RNN_BLOCK), RNN_BLOCK ** -0.5),
        'rg_b_i': nrm(ks[17], (N_RG, 2, D_RNN), 0.02),
        'rg_lambda': jnp.log(s) - jnp.log1p(-s),
        'rg_w_out': nrm(ks[19], (N_RG, D_RNN, D), D_RNN ** -0.5),
        'ffn_w_up': nrm(ks[20], (DEPTH, D, 2 * D_FF), D ** -0.5),
        'ffn_conv_w': nrm(ks[21], (DEPTH, FFN_CONV_W, 2 * D_FF), FFN_CONV_W ** -0.5),
        'ffn_conv_b': nrm(ks[22], (DEPTH, 2 * D_FF), 0.02),
        'ffn_w_down': nrm(ks[23], (DEPTH, D_FF, D), D_FF ** -0.5),
        'g_final': 1.0 + nrm(ks[24], (D,), 0.02),
    }


def reference(x, c, ctx, c_ctx, w_mod, b_mod, g_mix, g_ffn, fa_w_in, fa_w_out, attn_sink,
              rg_w_in, rg_conv_w, rg_conv_b, rg_w_a, rg_b_a, rg_w_i, rg_b_i, rg_lambda, rg_w_out,
              ffn_w_up, ffn_conv_w, ffn_conv_b, ffn_w_down, g_final):
    n = x.shape[1]
    rows = n // GRID_W
    row_ids = jnp.repeat(jnp.arange(rows, dtype=jnp.int32), GRID_W)
    col_ids = jnp.tile(jnp.arange(GRID_W, dtype=jnp.int32), rows)
    x_lat, x_ctx = x, ctx
    s_lat = jax.nn.silu(c)
    s_ctx = jax.nn.silu(c_ctx)[None]
    for layer in range(DEPTH):
        ctx_out = layer < DEPTH - 1
        mod_lat = (s_lat @ w_mod[layer] + b_mod[layer])[:, None, :]
        mod_ctx = (s_ctx @ w_mod[layer] + b_mod[layer])[:, None, :]
        sh_m, sc_m, gt_m, sh_f, sc_f, gt_f = jnp.split(mod_lat, N_MOD, axis=-1)
        csh_m, csc_m, cgt_m, csh_f, csc_f, cgt_f = jnp.split(mod_ctx, N_MOD, axis=-1)
        h_lat = modulate(rmsnorm(x_lat, g_mix[layer]), sh_m, sc_m)
        h_ctx = modulate(rmsnorm(x_ctx, g_mix[layer]), csh_m, csc_m)
        i = layer // 2
        if layer % 2 == 0:
            y_lat, y_ctx = fourier_attn_mixer(h_lat, h_ctx, fa_w_in[i], fa_w_out[i], attn_sink[i],
                                              row_ids, col_ids, ctx_out)
        else:
            y_lat, y_ctx = rglru_mixer(h_lat, h_ctx, rg_w_in[i], rg_conv_w[i], rg_conv_b[i], rg_w_a[i], rg_b_a[i],
                                       rg_w_i[i], rg_b_i[i], rg_lambda[i], rg_w_out[i], ctx_out)
        x_lat = x_lat + gt_m * y_lat
        h_lat = modulate(rmsnorm(x_lat, g_ffn[layer]), sh_f, sc_f)
        x_lat = x_lat + gt_f * conv_ffn(h_lat, ffn_w_up[layer], ffn_conv_w[layer], ffn_conv_b[layer], ffn_w_down[layer])
        if ctx_out:
            x_ctx = x_ctx + cgt_m * y_ctx
            h_ctx = modulate(rmsnorm(x_ctx, g_ffn[layer]), csh_f, csc_f)
            x_ctx = x_ctx + cgt_f * conv_ffn(h_ctx, ffn_w_up[layer], ffn_conv_w[layer], ffn_conv_b[layer], ffn_w_down[layer])
    return rmsnorm(x_lat, g_final)
```

```python
import functools
import math

import numpy as np
import jax
import jax.numpy as jnp
from jax import lax
from jax.experimental import pallas as pl
from jax.experimental.pallas import tpu as pltpu

D_MODEL = 2048
DEPTH = 4
GRID_W = 64
N_FG = 4
FG_W = 256
F_WIDTH = N_FG * FG_W
N_HEADS = 16
N_KV_HEADS = 2
HEAD_DIM = 64
ATT_WIDTH = N_HEADS * HEAD_DIM
KV_WIDTH = N_KV_HEADS * HEAD_DIM
WINDOW = 128
BLOCK = 128
ROPE_BASE = 10000.0
D_RNN = D_MODEL
N_RNN_BLOCKS = 8
RNN_BLOCK = D_RNN // N_RNN_BLOCKS
CONV_W = 4
CONV_LEFT = 2
LRU_C = 8.0
D_FF = 5632
FFN_CONV_W = 3
FFN_CONV_LEFT = 1
N_MOD = 6
EPS = 1e-6
NEG_INF = -1e30

LANES = 128
SUBLANES = 8
HALO = 16
VMEM_LIMIT = 56 * 1024 * 1024

BF16 = jnp.bfloat16
F32 = jnp.float32


def _params(*sem):
    return pltpu.CompilerParams(dimension_semantics=sem, vmem_limit_bytes=VMEM_LIMIT)


def _dot(a, b):
    return jnp.dot(a, b, preferred_element_type=F32)


def _dot_nt(a, b):
    return lax.dot_general(a, b, (((1,), (1,)), ((), ())), preferred_element_type=F32)


def _gelu_tanh(x):
    return 0.5 * x * (1.0 + jnp.tanh(math.sqrt(2.0 / math.pi) * (x + 0.044715 * (x * x * x))))


def _sigmoid(x):
    return 0.5 * (1.0 + jnp.tanh(0.5 * x))


def _mod_kernel(cl_ref, cc_ref, w_ref, b_ref, o_ref, sl_ref, sc_ref):
    @pl.when((pl.program_id(0) == 0) & (pl.program_id(1) == 0))
    def _():
        cl = cl_ref[...]
        cc = cc_ref[...]
        sl_ref[...] = cl * _sigmoid(cl)
        sc_ref[...] = cc * _sigmoid(cc)

    tn = w_ref.shape[1]
    reps = tn // LANES

    def body(kg, carry):
        al, ac = carry
        r0 = pl.multiple_of(kg * SUBLANES, SUBLANES)
        w8 = w_ref[pl.ds(r0, SUBLANES), :]
        s8l = pltpu.repeat(sl_ref[pl.ds(r0, SUBLANES), :], reps, axis=1)
        s8c = pltpu.repeat(sc_ref[pl.ds(r0, SUBLANES), :], reps, axis=1)
        return al + w8 * s8l, ac + w8 * s8c

    zero = jnp.zeros((SUBLANES, tn), F32)
    al, ac = lax.fori_loop(0, w_ref.shape[0] // SUBLANES, body, (zero, zero), unroll=4)
    b = b_ref[...]
    o_ref[0] = jnp.sum(al, axis=0, keepdims=True) + b
    o_ref[1] = jnp.sum(ac, axis=0, keepdims=True) + b


def _modulation(c, c_ctx, w_mod, b_mod):
    d = D_MODEL
    n = N_MOD * d
    tn = 1024
    cl = jnp.broadcast_to(c.reshape(d, 1), (d, LANES))
    cc = jnp.broadcast_to(c_ctx.reshape(d, 1), (d, LANES))
    return pl.pallas_call(
        _mod_kernel,
        grid=(DEPTH, n // tn),
        in_specs=[
            pl.BlockSpec((d, LANES), lambda l, j: (0, 0)),
            pl.BlockSpec((d, LANES), lambda l, j: (0, 0)),
            pl.BlockSpec((None, d, tn), lambda l, j: (l, 0, j)),
            pl.BlockSpec((None, 1, tn), lambda l, j: (l, 0, j)),
        ],
        out_specs=pl.BlockSpec((None, 2, 1, tn), lambda l, j: (l, 0, 0, j)),
        out_shape=jax.ShapeDtypeStruct((DEPTH, 2, 1, n), F32),
        scratch_shapes=[pltpu.VMEM((d, LANES), F32), pltpu.VMEM((d, LANES), F32)],
        compiler_params=_params("arbitrary", "arbitrary"),
        name="modulation",
    )(cl, cc, w_mod, b_mod.reshape(DEPTH, 1, n))


def _mod_spec(layer, row, k, tn, col_of):
    per = D_MODEL // tn
    return pl.BlockSpec((None, None, 1, tn), lambda *g: (layer, row, 0, k * per + col_of(*g)))


def _norm_mod_kernel(x_ref, g_ref, sh_ref, sc_ref, o_ref):
    x = x_ref[...]
    r = lax.rsqrt(jnp.mean(x * x, axis=-1, keepdims=True) + EPS)
    h = (x * r) * g_ref[...]
    o_ref[...] = (h * (1.0 + sc_ref[...]) + sh_ref[...]).astype(o_ref.dtype)


def _norm_mod(x, g, layer, mods, row, k_shift, k_scale):
    m, d = x.shape
    tm = min(m, 512)
    zero = lambda i: 0
    return pl.pallas_call(
        _norm_mod_kernel,
        grid=(m // tm,),
        in_specs=[
            pl.BlockSpec((tm, d), lambda i: (i, 0)),
            pl.BlockSpec((None, 1, d), lambda i: (layer, 0, 0)),
            _mod_spec(layer, row, k_shift, d, zero),
            _mod_spec(layer, row, k_scale, d, zero),
        ],
        out_specs=pl.BlockSpec((tm, d), lambda i: (i, 0)),
        out_shape=jax.ShapeDtypeStruct((m, d), BF16),
        compiler_params=_params("arbitrary"),
        name="norm_mod",
    )(x, g, mods, mods)


def _cast_weights(w_refs, wb_ref):
    @pl.when(pl.program_id(1) == 0)
    def _():
        c0 = 0
        for w_ref in w_refs:
            wn = w_ref.shape[1]
            wb_ref[:, c0:c0 + wn] = w_ref[...].astype(BF16)
            c0 += wn


def _proj_f32_kernel(n_w, h_ref, *refs):
    o_ref, wb_ref = refs[n_w:]
    _cast_weights(refs[:n_w], wb_ref)
    o_ref[...] = _dot(h_ref[...], wb_ref[...])


def _proj_gelu_kernel(n_w, h_ref, *refs):
    o_ref, wb_ref = refs[n_w:]
    _cast_weights(refs[:n_w], wb_ref)
    o_ref[...] = _gelu_tanh(_dot(h_ref[...], wb_ref[...])).astype(o_ref.dtype)


def _swap16(x, even):
    return jnp.where(even, pltpu.roll(x, LANES - 16, axis=1), pltpu.roll(x, 16, axis=1))


def _proj_qkv_kernel(rope, n_w, h_ref, *refs):
    if rope:
        cos_ref, sin_ref, o_ref, wb_ref = refs[n_w:]
    else:
        o_ref, wb_ref = refs[n_w:]
    _cast_weights(refs[:n_w], wb_ref)
    acc = _dot(h_ref[...], wb_ref[...])
    q_scale = HEAD_DIM ** -0.5
    n_rot = (ATT_WIDTH + KV_WIDTH) // LANES
    if rope:
        cos = cos_ref[...]
        sin = sin_ref[...]
        even = (lax.broadcasted_iota(jnp.int32, cos.shape, 1) & 16) == 0
    for cidx in range(acc.shape[1] // LANES):
        t = acc[:, cidx * LANES:(cidx + 1) * LANES]
        if cidx < ATT_WIDTH // LANES:
            t = t * q_scale
        if rope and cidx < n_rot:
            t = t * cos + _swap16(t, even) * sin
        o_ref[:, cidx * LANES:(cidx + 1) * LANES] = t.astype(o_ref.dtype)


def _proj(kind, h, w, w_index, col0, n, tn, wtn, out_dtype, tm=None, extra=()):
    m, k = h.shape
    tm = tm or min(m, 1024)
    assert col0 % wtn == 0 and tn % wtn == 0 and n % tn == 0 and m % tm == 0
    n_w = tn // wtn
    kernels = {
        "f32": _proj_f32_kernel,
        "gelu": _proj_gelu_kernel,
        "qkv": functools.partial(_proj_qkv_kernel, False),
        "qkv_rope": functools.partial(_proj_qkv_kernel, True),
    }
    in_specs = [pl.BlockSpec((tm, k), lambda j, i: (i, 0))]
    for p in range(n_w):
        in_specs.append(pl.BlockSpec((None, k, wtn),
                                     lambda j, i, p=p: (w_index, 0, col0 // wtn + j * n_w + p)))
    in_specs += [pl.BlockSpec((tm, LANES), lambda j, i: (i, 0)) for _ in extra]
    return pl.pallas_call(
        functools.partial(kernels[kind], n_w),
        grid=(n // tn, m // tm),
        in_specs=in_specs,
        out_specs=pl.BlockSpec((tm, tn), lambda j, i: (i, j)),
        out_shape=jax.ShapeDtypeStruct((m, n), out_dtype),
        scratch_shapes=[pltpu.VMEM((k, tn), BF16)],
        compiler_params=_params("arbitrary", "arbitrary"),
        name="proj_" + kind,
    )(h, *([w] * n_w), *extra)


def _proj_res_kernel(n_parts, *refs):
    a_refs = refs[:n_parts]
    w_ref, x_ref, gt_ref, o_ref, wb_ref = refs[n_parts:]
    _cast_weights([w_ref], wb_ref)
    acc = None
    k0 = 0
    for a_ref in a_refs:
        kp = a_ref.shape[1]
        part = _dot(a_ref[...], wb_ref[k0:k0 + kp, :])
        acc = part if acc is None else acc + part
        k0 += kp
    o_ref[...] = x_ref[...] + gt_ref[...] * acc


def _proj_residual(parts, w, w_index, x, layer, mods, row, k_gate, tm, tn):
    m, n = x.shape
    k = sum(p.shape[1] for p in parts)
    tm = min(m, tm)
    in_specs = [pl.BlockSpec((tm, p.shape[1]), lambda j, i: (i, 0)) for p in parts]
    in_specs += [
        pl.BlockSpec((None, k, tn), lambda j, i: (w_index, 0, j)),
        pl.BlockSpec((tm, tn), lambda j, i: (i, j)),
        _mod_spec(layer, row, k_gate, tn, lambda j, i: j),
    ]
    return pl.pallas_call(
        functools.partial(_proj_res_kernel, len(parts)),
        grid=(n // tn, m // tm),
        in_specs=in_specs,
        out_specs=pl.BlockSpec((tm, tn), lambda j, i: (i, j)),
        out_shape=jax.ShapeDtypeStruct((m, n), F32),
        scratch_shapes=[pltpu.VMEM((k, tn), BF16)],
        compiler_params=_params("arbitrary", "arbitrary"),
        name="proj_residual",
    )(*parts, w, x, mods)


def _conv_proj_kernel(n_w, kw, left, gated, h_ref, hp_ref, hn_ref, *refs):
    w_refs = refs[:n_w]
    cw_refs = refs[n_w:2 * n_w]
    cb_refs = refs[2 * n_w:3 * n_w]
    o_ref, wb_ref, ext_ref = refs[3 * n_w:]
    i = pl.program_id(1)
    tm = h_ref.shape[0]
    tn = w_refs[0].shape[1]

    @pl.when(i == 0)
    def _():
        for p, w_ref in enumerate(w_refs):
            wb_ref[:, p * tn:(p + 1) * tn] = w_ref[...].astype(BF16)

    ext_ref[0:HALO, :] = jnp.where(i > 0, hp_ref[...], jnp.zeros_like(hp_ref))
    ext_ref[HALO:HALO + tm, :] = h_ref[...]
    ext_ref[HALO + tm:, :] = jnp.where(i < pl.num_programs(1) - 1, hn_ref[...], jnp.zeros_like(hn_ref))
    z = _dot(ext_ref[...], wb_ref[...])
    outs = []
    for p in range(n_w):
        zp = z[:, p * tn:(p + 1) * tn]
        cw = cw_refs[p][...]
        u = cb_refs[p][...]
        for tap in range(kw):
            r0 = HALO - left + tap
            u = u + zp[r0:r0 + tm, :] * cw[tap:tap + 1, :]
        outs.append(u)
    if gated:
        g, v = outs
        o_ref[...] = (g * _sigmoid(g) * v).astype(o_ref.dtype)
    else:
        o_ref[...] = outs[0].astype(o_ref.dtype)


def _conv_proj(h, w, w_index, col_blocks, conv_blocks, n_out, cw, cb, kw, left, gated, tm, tn, out_dtype):
    m, k = h.shape
    tm = min(m, tm)
    n_w = len(col_blocks)
    hb = m // HALO
    tb = tm // HALO
    in_specs = [
        pl.BlockSpec((tm, k), lambda j, i: (i, 0)),
        pl.BlockSpec((HALO, k), lambda j, i: (jnp.maximum(i * tb - 1, 0), 0)),
        pl.BlockSpec((HALO, k), lambda j, i: (jnp.minimum((i + 1) * tb, hb - 1), 0)),
    ]
    for c0 in col_blocks:
        in_specs.append(pl.BlockSpec((None, k, tn), lambda j, i, c0=c0: (w_index, 0, c0 + j)))
    for c0 in conv_blocks:
        in_specs.append(pl.BlockSpec((None, kw, tn), lambda j, i, c0=c0: (w_index, 0, c0 + j)))
    for c0 in conv_blocks:
        in_specs.append(pl.BlockSpec((None, 1, tn), lambda j, i, c0=c0: (w_index, 0, c0 + j)))
    args = [h, h, h] + [w] * n_w + [cw] * n_w + [cb] * n_w
    return pl.pallas_call(
        functools.partial(_conv_proj_kernel, n_w, kw, left, gated),
        grid=(n_out // tn, m // tm),
        in_specs=in_specs,
        out_specs=pl.BlockSpec((tm, tn), lambda j, i: (i, j)),
        out_shape=jax.ShapeDtypeStruct((m, n_out), out_dtype),
        scratch_shapes=[pltpu.VMEM((k, n_w * tn), BF16), pltpu.VMEM((tm + 2 * HALO, k), BF16)],
        compiler_params=_params("arbitrary", "arbitrary"),
        name="conv_proj",
    )(*args)


def _pair_operand(band, kv_head):
    b = band.astype(F32)
    rolled = pltpu.roll(b, HEAD_DIM, axis=1)
    low = lax.broadcasted_iota(jnp.int32, b.shape, 1) < HEAD_DIM
    zero = jnp.zeros_like(b)
    if kv_head == 0:
        top = jnp.where(low, b, zero)
        bot = jnp.where(low, zero, rolled)
    else:
        top = jnp.where(low, rolled, zero)
        bot = jnp.where(low, zero, b)
    return jnp.concatenate([top, bot], axis=0).astype(BF16)


def _attention_core(q_ref, k_band, v_band, bias, sink_ref, fa_index, o_ref):
    tq = q_ref.shape[0]
    nk = k_band.shape[0]
    low = lax.broadcasted_iota(jnp.int32, (tq, LANES), 1) < HEAD_DIM
    heads_per_kv = N_HEADS // N_KV_HEADS
    for kv_head in range(N_KV_HEADS):
        k2 = _pair_operand(k_band, kv_head)
        v2 = _pair_operand(v_band, kv_head)
        for pair in range(heads_per_kv // 2):
            c0 = (kv_head * (heads_per_kv // 2) + pair) * LANES
            head = c0 // HEAD_DIM
            s = _dot_nt(q_ref[:, c0:c0 + LANES], k2)
            ps, inv = [], []
            for half in range(2):
                sh = s[:, half * nk:(half + 1) * nk]
                if bias is not None:
                    sh = sh + bias
                sink = sink_ref[fa_index, head + half]
                mx = jnp.maximum(jnp.max(sh, axis=1, keepdims=True), sink)
                p = jnp.exp(sh - mx)
                den = jnp.sum(p, axis=1, keepdims=True) + jnp.exp(sink - mx)
                ps.append(p.astype(BF16))
                inv.append(1.0 / den)
            o = _dot(jnp.concatenate(ps, axis=1), v2)
            o = o * jnp.where(low, inv[0], inv[1])
            o_ref[:, c0:c0 + LANES] = o.astype(o_ref.dtype)


def _attn_lat_kernel(fa_index, sink_ref, q_ref, kp_ref, kc_ref, kn_ref, vp_ref, vc_ref, vn_ref,
                     kx_ref, vx_ref, o_ref):
    n = pl.program_id(0)
    nb = pl.num_programs(0)
    n_ctx = kx_ref.shape[0]
    k_band = jnp.concatenate([kp_ref[...], kc_ref[...], kn_ref[...], kx_ref[...]], axis=0)
    v_band = jnp.concatenate([vp_ref[...], vc_ref[...], vn_ref[...], vx_ref[...]], axis=0)
    nk = 3 * BLOCK + n_ctx
    qi = lax.broadcasted_iota(jnp.int32, (BLOCK, nk), 0)
    kj = lax.broadcasted_iota(jnp.int32, (BLOCK, nk), 1)
    rel = kj - BLOCK - qi
    k_pos = (n - 1) * BLOCK + kj
    ok_win = (jnp.abs(rel) <= WINDOW) & (k_pos >= 0) & (k_pos < nb * BLOCK)
    ok = (kj >= 3 * BLOCK) | ok_win
    bias = jnp.where(ok, 0.0, NEG_INF).astype(F32)
    _attention_core(q_ref, k_band, v_band, bias, sink_ref, fa_index, o_ref)


def _attn_ctx_kernel(fa_index, sink_ref, q_ref, kx_ref, vx_ref, o_ref):
    _attention_core(q_ref, kx_ref[...], vx_ref[...], None, sink_ref, fa_index, o_ref)


def _attention_lat(qkv, qkv_ctx, sink, fa_index):
    s = qkv.shape[0]
    n_ctx = qkv_ctx.shape[0]
    nb = s // BLOCK
    kcol = ATT_WIDTH // LANES
    vcol = kcol + 1
    prev = lambda n: jnp.maximum(n - 1, 0)
    nxt = lambda n: jnp.minimum(n + 1, nb - 1)
    in_specs = [
        pl.BlockSpec(memory_space=pltpu.SMEM),
        pl.BlockSpec((BLOCK, ATT_WIDTH), lambda n: (n, 0)),
        pl.BlockSpec((BLOCK, LANES), lambda n: (prev(n), kcol)),
        pl.BlockSpec((BLOCK, LANES), lambda n: (n, kcol)),
        pl.BlockSpec((BLOCK, LANES), lambda n: (nxt(n), kcol)),
        pl.BlockSpec((BLOCK, LANES), lambda n: (prev(n), vcol)),
        pl.BlockSpec((BLOCK, LANES), lambda n: (n, vcol)),
        pl.BlockSpec((BLOCK, LANES), lambda n: (nxt(n), vcol)),
        pl.BlockSpec((n_ctx, LANES), lambda n: (0, kcol)),
        pl.BlockSpec((n_ctx, LANES), lambda n: (0, vcol)),
    ]
    return pl.pallas_call(
        functools.partial(_attn_lat_kernel, fa_index),
        grid=(nb,),
        in_specs=in_specs,
        out_specs=pl.BlockSpec((BLOCK, ATT_WIDTH), lambda n: (n, 0)),
        out_shape=jax.ShapeDtypeStruct((s, ATT_WIDTH), BF16),
        compiler_params=_params("arbitrary"),
        name="attention_latent",
    )(sink, qkv, qkv, qkv, qkv, qkv, qkv, qkv, qkv_ctx, qkv_ctx)


def _attention_ctx(qkv_ctx, sink, fa_index):
    n_ctx = qkv_ctx.shape[0]
    kcol = ATT_WIDTH // LANES
    return pl.pallas_call(
        functools.partial(_attn_ctx_kernel, fa_index),
        grid=(1,),
        in_specs=[
            pl.BlockSpec(memory_space=pltpu.SMEM),
            pl.BlockSpec((n_ctx, ATT_WIDTH), lambda n: (0, 0)),
            pl.BlockSpec((n_ctx, LANES), lambda n: (0, kcol)),
            pl.BlockSpec((n_ctx, LANES), lambda n: (0, kcol + 1)),
        ],
        out_specs=pl.BlockSpec((n_ctx, ATT_WIDTH), lambda n: (0, 0)),
        out_shape=jax.ShapeDtypeStruct((n_ctx, ATT_WIDTH), BF16),
        compiler_params=_params("arbitrary"),
        name="attention_context",
    )(sink, qkv_ctx, qkv_ctx, qkv_ctx)


FFT_N1 = 64
FFT_N2 = 128


def _dft_cos_sin(n):
    idx = np.arange(n)
    ang = 2.0 * np.pi * ((idx[:, None] * idx[None, :]) % n) / n
    return np.cos(ang), np.sin(ang)


def _fourier_constants():
    c1, s1 = _dft_cos_sin(FFT_N1)
    stage1 = np.concatenate([c1, -s1], axis=0)
    c2, s2 = _dft_cos_sin(FFT_N2)
    stage2 = np.block([[c2, s2], [-s2, c2]])
    cc, sc = _dft_cos_sin(FG_W)
    chan = np.concatenate([cc, sc], axis=0)
    return (jnp.asarray(stage1, F32), jnp.asarray(stage2, F32), jnp.asarray(chan, F32))


def _twiddle_tables():
    k1 = jnp.arange(FFT_N1, dtype=jnp.int32)[:, None]
    n2 = jnp.arange(FFT_N2, dtype=jnp.int32)[None, :]
    ang = ((k1 * n2) % (FFT_N1 * FFT_N2)).astype(F32) * (2.0 * math.pi / (FFT_N1 * FFT_N2))
    wr = jnp.repeat(jnp.cos(ang), LANES, axis=1)
    wi = jnp.repeat(-jnp.sin(ang), LANES, axis=1)
    return wr, wi


def _fourier_stage1_kernel(x_ref, m_ref, wr_ref, wi_ref, tr_ref, ti_ref):
    y = _dot(m_ref[...].astype(BF16), x_ref[...].astype(BF16))
    reps = F_WIDTH // LANES
    for b in range(x_ref.shape[1] // F_WIDTH):
        cols = slice(b * F_WIDTH, (b + 1) * F_WIDTH)
        yr = y[:FFT_N1, cols]
        yi = y[FFT_N1:, cols]
        wr = pltpu.repeat(wr_ref[:, b * LANES:(b + 1) * LANES], reps, axis=1)
        wi = pltpu.repeat(wi_ref[:, b * LANES:(b + 1) * LANES], reps, axis=1)
        tr_ref[:, cols] = (yr * wr - yi * wi).astype(tr_ref.dtype)
        ti_ref[:, cols] = (yr * wi + yi * wr).astype(ti_ref.dtype)


def _channel_stage(p, chan_ref, scale, o_ref):
    r = p.shape[0] // 2
    pr = p[:r].astype(BF16)
    pi = p[r:].astype(BF16)
    chan_c = chan_ref[:FG_W, :].astype(BF16)
    chan_s = chan_ref[FG_W:, :].astype(BF16)
    for g in range(N_FG):
        cols = slice(g * FG_W, (g + 1) * FG_W)
        y = _dot(pr[:, cols], chan_c) + _dot(pi[:, cols], chan_s)
        o_ref[:, cols] = (y * scale).astype(o_ref.dtype)


def _fourier_stage2_kernel(scale, tr_ref, ti_ref, m_ref, chan_ref, o_ref):
    t = jnp.concatenate([tr_ref[...], ti_ref[...]], axis=0)
    _channel_stage(_dot(m_ref[...].astype(BF16), t), chan_ref, scale, o_ref)


def _fourier_ctx_kernel(scale, x_ref, m_ref, chan_ref, o_ref):
    _channel_stage(_dot(m_ref[...].astype(BF16), x_ref[...].astype(BF16)), chan_ref, scale, o_ref)


def _fourier_lat(f, consts, twiddles):
    n = f.shape[0]
    assert n == FFT_N1 * FFT_N2
    stage1, stage2, chan = consts
    wr, wi = twiddles
    n2_blk = 8
    cols = n2_blk * F_WIDTH
    full = lambda a: pl.BlockSpec(a.shape, lambda j: (0,) * a.ndim)
    tr, ti = pl.pallas_call(
        _fourier_stage1_kernel,
        grid=(FFT_N2 // n2_blk,),
        in_specs=[
            pl.BlockSpec((FFT_N1, cols), lambda j: (0, j)),
            full(stage1),
            pl.BlockSpec((FFT_N1, n2_blk * LANES), lambda j: (0, j)),
            pl.BlockSpec((FFT_N1, n2_blk * LANES), lambda j: (0, j)),
        ],
        out_specs=[pl.BlockSpec((FFT_N1, cols), lambda j: (0, j))] * 2,
        out_shape=[jax.ShapeDtypeStruct((FFT_N1, FFT_N2 * F_WIDTH), BF16)] * 2,
        compiler_params=_params("arbitrary"),
        name="fourier_stage1",
    )(f.reshape(FFT_N1, FFT_N2 * F_WIDTH), stage1, wr, wi)
    tr = tr.reshape(FFT_N1, FFT_N2, F_WIDTH)
    ti = ti.reshape(FFT_N1, FFT_N2, F_WIDTH)
    scale = 1.0 / math.sqrt(n * FG_W)
    out = pl.pallas_call(
        functools.partial(_fourier_stage2_kernel, scale),
        grid=(FFT_N1,),
        in_specs=[
            pl.BlockSpec((None, FFT_N2, F_WIDTH), lambda k1: (k1, 0, 0)),
            pl.BlockSpec((None, FFT_N2, F_WIDTH), lambda k1: (k1, 0, 0)),
            full(stage2),
            full(chan),
        ],
        out_specs=pl.BlockSpec((FFT_N2, F_WIDTH), lambda k1: (0, k1)),
        out_shape=jax.ShapeDtypeStruct((FFT_N2, FFT_N1 * F_WIDTH), BF16),
        compiler_params=_params("arbitrary"),
        name="fourier_stage2",
    )(tr, ti, stage2, chan)
    return out.reshape(n, F_WIDTH)


def _fourier_ctx(f, chan):
    n = f.shape[0]
    c, s = _dft_cos_sin(n)
    m = jnp.asarray(np.concatenate([c, -s], axis=0), F32)
    full = lambda a: pl.BlockSpec(a.shape, lambda j: (0,) * a.ndim)
    return pl.pallas_call(
        functools.partial(_fourier_ctx_kernel, 1.0 / math.sqrt(n * FG_W)),
        grid=(1,),
        in_specs=[full(f), full(m), full(chan)],
        out_specs=pl.BlockSpec((n, F_WIDTH), lambda j: (0, 0)),
        out_shape=jax.ShapeDtypeStruct((n, F_WIDTH), BF16),
        compiler_params=_params("arbitrary"),
        name="fourier_context",
    )(f, m, chan)


def _scan_rows(a_ref, b_ref, h_ref, carry0, reverse):
    tt, c = a_ref.shape
    groups = tt // SUBLANES
    row = lax.broadcasted_iota(jnp.int32, (SUBLANES, c), 0)

    def body(g, carry):
        gi = (groups - 1 - g) if reverse else g
        r0 = pl.multiple_of(gi * SUBLANES, SUBLANES)
        a = a_ref[pl.ds(r0, SUBLANES), :]
        b = b_ref[pl.ds(r0, SUBLANES), :]
        for d in (1, 2, 4):
            shift = (SUBLANES - d) if reverse else d
            keep = (row < SUBLANES - d) if reverse else (row >= d)
            b = jnp.where(keep, b + a * pltpu.roll(b, shift, axis=0), b)
            a = jnp.where(keep, a * pltpu.roll(a, shift, axis=0), a)
        h = b + a * carry
        h_ref[pl.ds(r0, SUBLANES), :] = h
        edge = h[0:1, :] if reverse else h[SUBLANES - 1:SUBLANES, :]
        return jnp.broadcast_to(edge, (SUBLANES, c))

    return lax.fori_loop(0, groups, body, carry0, unroll=2)


def _rglru_kernel(reverse, combine, xs_ref, wa_ref, wi_ref, ba_ref, bi_ref, lam_ref, h0_ref, *refs):
    if combine:
        hb_ref, gg_ref, o_ref, last_ref, a_scr, b_scr, h_scr, carry_scr = refs
    else:
        o_ref, last_ref, a_scr, b_scr, h_scr, carry_scr = refs
    t = pl.program_id(1)

    @pl.when(t == 0)
    def _():
        carry_scr[...] = jnp.broadcast_to(h0_ref[...], carry_scr.shape)

    xs = xs_ref[...]
    xb = xs.astype(BF16)
    r = _sigmoid(_dot(xb, wa_ref[...].astype(BF16)) + ba_ref[...])
    gi = _sigmoid(_dot(xb, wi_ref[...].astype(BF16)) + bi_ref[...])
    neg_lam = -lam_ref[...]
    softplus = jnp.maximum(neg_lam, 0.0) + jnp.log1p(jnp.exp(-jnp.abs(neg_lam)))
    log_a = (-LRU_C * r) * softplus
    a = jnp.exp(log_a)
    a_scr[...] = a
    b_scr[...] = jnp.sqrt(1.0 - a * a) * (gi * xs)
    carry = _scan_rows(a_scr, b_scr, h_scr, carry_scr[...], reverse)
    carry_scr[...] = carry
    last_ref[...] = carry[0:1, :]
    if combine:
        o_ref[...] = ((h_scr[...] + hb_ref[...]) * gg_ref[...].astype(F32)).astype(o_ref.dtype)
    else:
        o_ref[...] = h_scr[...]


def _rglru_scan(xs, w_a, w_i, b_a, b_i, lam, rg_index, direction, h0, h_other=None, gelu_gate=None):
    m = xs.shape[0]
    tt = min(m, 1024)
    nt = m // tt
    c = RNN_BLOCK
    reverse = direction == 1
    combine = h_other is not None
    tix = (lambda t: nt - 1 - t) if reverse else (lambda t: t)
    wspec = pl.BlockSpec((None, None, None, c, c), lambda cb, t: (rg_index, direction, cb, 0, 0))
    vspec = pl.BlockSpec((None, None, 1, c), lambda cb, t: (rg_index, direction, 0, cb))
    tile = pl.BlockSpec((tt, c), lambda cb, t: (tix(t), cb))
    in_specs = [tile, wspec, wspec, vspec, vspec, vspec, pl.BlockSpec((1, c), lambda cb, t: (0, cb))]
    args = [xs, w_a, w_i, b_a, b_i, lam, h0]
    if combine:
        in_specs += [tile, tile]
        args += [h_other, gelu_gate]
    return pl.pallas_call(
        functools.partial(_rglru_kernel, reverse, combine),
        grid=(D_RNN // c, nt),
        in_specs=in_specs,
        out_specs=[tile, pl.BlockSpec((1, c), lambda cb, t: (0, cb))],
        out_shape=[jax.ShapeDtypeStruct((m, D_RNN), BF16 if combine else F32),
                   jax.ShapeDtypeStruct((1, D_RNN), F32)],
        scratch_shapes=[pltpu.VMEM((tt, c), F32)] * 3 + [pltpu.VMEM((SUBLANES, c), F32)],
        compiler_params=_params("arbitrary", "arbitrary"),
        name="rglru_scan",
    )(*args)


def _final_norm_kernel(x_ref, g_ref, o_ref):
    x = x_ref[...]
    r = lax.rsqrt(jnp.mean(x * x, axis=-1, keepdims=True) + EPS)
    o_ref[...] = (x * r) * g_ref[...]


def _final_norm(x, g):
    m, d = x.shape
    tm = 512
    return pl.pallas_call(
        _final_norm_kernel,
        grid=(m // tm,),
        in_specs=[pl.BlockSpec((tm, d), lambda i: (i, 0)), pl.BlockSpec((1, d), lambda i: (0, 0))],
        out_specs=pl.BlockSpec((tm, d), lambda i: (i, 0)),
        out_shape=jax.ShapeDtypeStruct((m, d), F32),
        compiler_params=_params("arbitrary"),
        name="final_norm",
    )(x, g.reshape(1, d))


def _rope_tables(n):
    f = HEAD_DIM // 4
    inv = ROPE_BASE ** (-jnp.arange(f, dtype=F32) / f)
    pos = jnp.arange(n, dtype=jnp.int32)
    ang_r = (pos // GRID_W).astype(F32)[:, None] * inv[None, :]
    ang_c = (pos % GRID_W).astype(F32)[:, None] * inv[None, :]
    cr, sr, cc, sc = jnp.cos(ang_r), jnp.sin(ang_r), jnp.cos(ang_c), jnp.sin(ang_c)
    cos = jnp.concatenate([cr, cr, cc, cc], axis=1)
    sin = jnp.concatenate([-sr, sr, -sc, sc], axis=1)
    reps = LANES // HEAD_DIM
    return jnp.tile(cos, (1, reps)), jnp.tile(sin, (1, reps))


def _fourier_attn_layer(layer, i, x_lat, x_ctx, h_lat, h_ctx, mods, fa_w_in, fa_w_out, attn_sink,
                        tables, ctx_out):
    rope, consts, twiddles = tables
    qkv_w = ATT_WIDTH + 2 * KV_WIDTH
    f_lat = _proj("f32", h_lat, fa_w_in, i, 0, F_WIDTH, F_WIDTH, F_WIDTH, F32)
    qkv_lat = _proj("qkv_rope", h_lat, fa_w_in, i, F_WIDTH, qkv_w, qkv_w, 256, BF16, tm=512, extra=rope)
    qkv_ctx = _proj("qkv", h_ctx, fa_w_in, i, F_WIDTH, qkv_w, qkv_w, 256, BF16)
    fo_lat = _fourier_lat(f_lat, consts, twiddles)
    ao_lat = _attention_lat(qkv_lat, qkv_ctx, attn_sink, i)
    x_lat = _proj_residual([fo_lat, ao_lat], fa_w_out, i, x_lat, layer, mods, 0, 2, 1024, 1024)
    if ctx_out:
        f_ctx = _proj("f32", h_ctx, fa_w_in, i, 0, F_WIDTH, F_WIDTH, F_WIDTH, F32)
        fo_ctx = _fourier_ctx(f_ctx, consts[2])
        ao_ctx = _attention_ctx(qkv_ctx, attn_sink, i)
        x_ctx = _proj_residual([fo_ctx, ao_ctx], fa_w_out, i, x_ctx, layer, mods, 1, 2, 1024, 1024)
    return x_lat, x_ctx


def _rglru_layer(layer, i, x_lat, x_ctx, h_lat, h_ctx, mods, rg_w_in, rg_conv_w, rg_conv_b, w_a, b_a,
                 w_i, b_i, lam, rg_w_out, ctx_out):
    tn = 1024
    xcol = D_RNN // tn
    zero_state = jnp.zeros((1, D_RNN), F32)
    gate_lat = _proj("gelu", h_lat, rg_w_in, i, 0, D_RNN, tn, tn, BF16)
    xs_lat = _conv_proj(h_lat, rg_w_in, i, [xcol], [0], D_RNN, rg_conv_w, rg_conv_b, CONV_W, CONV_LEFT,
                        False, 1024, tn, F32)
    xs_ctx = _conv_proj(h_ctx, rg_w_in, i, [xcol], [0], D_RNN, rg_conv_w, rg_conv_b, CONV_W, CONV_LEFT,
                        False, 1024, tn, F32)
    scan = functools.partial(_rglru_scan, w_a=w_a, w_i=w_i, b_a=b_a, b_i=b_i, lam=lam, rg_index=i)
    hb_ctx, s_bwd = scan(xs_ctx, direction=1, h0=zero_state)
    if ctx_out:
        gate_ctx = _proj("gelu", h_ctx, rg_w_in, i, 0, D_RNN, tn, tn, BF16)
        y_ctx, s_fwd = scan(xs_ctx, direction=0, h0=zero_state, h_other=hb_ctx, gelu_gate=gate_ctx)
    else:
        _, s_fwd = scan(xs_ctx, direction=0, h0=zero_state)
    hb_lat, _ = scan(xs_lat, direction=1, h0=s_bwd)
    y_lat, _ = scan(xs_lat, direction=0, h0=s_fwd, h_other=hb_lat, gelu_gate=gate_lat)
    x_lat = _proj_residual([y_lat], rg_w_out, i, x_lat, layer, mods, 0, 2, 1024, 1024)
    if ctx_out:
        x_ctx = _proj_residual([y_ctx], rg_w_out, i, x_ctx, layer, mods, 1, 2, 1024, 1024)
    return x_lat, x_ctx


def _conv_ffn(layer, x, g_ffn, mods, row, w_up, conv_w, conv_b, w_down):
    h = _norm_mod(x, g_ffn, layer, mods, row, 3, 4)
    tf = 512
    blocks = [0, D_FF // tf]
    act = _conv_proj(h, w_up, layer, blocks, blocks, D_FF, conv_w, conv_b, FFN_CONV_W, FFN_CONV_LEFT, True,
                     1024, tf, BF16)
    return _proj_residual([act], w_down, layer, x, layer, mods, row, 5, 512, 512)


def kernel(x, c, ctx, c_ctx, w_mod, b_mod, g_mix, g_ffn, fa_w_in, fa_w_out, attn_sink, rg_w_in, rg_conv_w,
           rg_conv_b, rg_w_a, rg_b_a, rg_w_i, rg_b_i, rg_lambda, rg_w_out, ffn_w_up, ffn_conv_w, ffn_conv_b,
           ffn_w_down, g_final):
    assert x.shape[0] == 1 and ctx.shape[0] == 1
    n = x.shape[1]
    x_lat = x[0]
    x_ctx = ctx[0]
    mods = _modulation(c, c_ctx, w_mod, b_mod)
    tables = (_rope_tables(n), _fourier_constants(), _twiddle_tables())
    g_mix3 = g_mix.reshape(DEPTH, 1, D_MODEL)
    g_ffn3 = g_ffn.reshape(DEPTH, 1, D_MODEL)
    n_rg = rg_conv_b.shape[0]
    rg_conv_b3 = rg_conv_b.reshape(n_rg, 1, D_RNN)
    rg_b_a4 = rg_b_a.reshape(n_rg, 2, 1, D_RNN)
    rg_b_i4 = rg_b_i.reshape(n_rg, 2, 1, D_RNN)
    rg_lam4 = rg_lambda.reshape(n_rg, 2, 1, D_RNN)
    ffn_conv_b3 = ffn_conv_b.reshape(DEPTH, 1, 2 * D_FF)
    for layer in range(DEPTH):
        ctx_out = layer < DEPTH - 1
        i = layer // 2
        h_lat = _norm_mod(x_lat, g_mix3, layer, mods, 0, 0, 1)
        h_ctx = _norm_mod(x_ctx, g_mix3, layer, mods, 1, 0, 1)
        if layer % 2 == 0:
            x_lat, x_ctx = _fourier_attn_layer(layer, i, x_lat, x_ctx, h_lat, h_ctx, mods, fa_w_in, fa_w_out,
                                               attn_sink, tables, ctx_out)
        else:
            x_lat, x_ctx = _rglru_layer(layer, i, x_lat, x_ctx, h_lat, h_ctx, mods, rg_w_in, rg_conv_w,
                                        rg_conv_b3, rg_w_a, rg_b_a4, rg_w_i, rg_b_i4, rg_lam4, rg_w_out,
                                        ctx_out)
        x_lat = _conv_ffn(layer, x_lat, g_ffn3, mods, 0, ffn_w_up, ffn_conv_w, ffn_conv_b3, ffn_w_down)
        if ctx_out:
            x_ctx = _conv_ffn(layer, x_ctx, g_ffn3, mods, 1, ffn_w_up, ffn_conv_w, ffn_conv_b3, ffn_w_down)
    return _final_norm(x_lat, g_final)[None]
```

```python
import functools
import math

import numpy as np
import jax
import jax.numpy as jnp
from jax import lax
from jax.experimental import pallas as pl
from jax.experimental.pallas import tpu as pltpu

D_MODEL = 2048
DEPTH = 4
GRID_W = 64
N_FG = 4
FG_W = 256
F_WIDTH = N_FG * FG_W
N_HEADS = 16
N_KV_HEADS = 2
HEAD_DIM = 64
ATT_WIDTH = N_HEADS * HEAD_DIM
KV_WIDTH = N_KV_HEADS * HEAD_DIM
WINDOW = 128
BLOCK = 128
ROPE_BASE = 10000.0
D_RNN = D_MODEL
N_RNN_BLOCKS = 8
RNN_BLOCK = D_RNN // N_RNN_BLOCKS
CONV_W = 4
CONV_LEFT = 2
LRU_C = 8.0
D_FF = 5632
FFN_CONV_W = 3
FFN_CONV_LEFT = 1
N_MOD = 6
EPS = 1e-6
NEG_INF = -1e30

LANES = 128
SUBLANES = 8
HALO = 16
VMEM_LIMIT = 56 * 1024 * 1024
ROW_CHUNK = 64
PROJ_BLOCK = 256
CONV_SUB_BLOCKS = 6

BF16 = jnp.bfloat16
F32 = jnp.float32


def _params(*sem):
    return pltpu.CompilerParams(dimension_semantics=sem, vmem_limit_bytes=VMEM_LIMIT)


def _dot(a, b):
    return jnp.dot(a, b, preferred_element_type=F32)


def _dot_nt(a, b):
    return lax.dot_general(a, b, (((1,), (1,)), ((), ())), preferred_element_type=F32)


def _gelu_tanh(x):
    return 0.5 * x * (1.0 + jnp.tanh(math.sqrt(2.0 / math.pi) * (x + 0.044715 * (x * x * x))))


def _sigmoid(x):
    return 0.5 * (1.0 + jnp.tanh(0.5 * x))


def _mod_kernel(cl_ref, cc_ref, w_ref, b_ref, o_ref, sl_ref, sc_ref):
    @pl.when((pl.program_id(0) == 0) & (pl.program_id(1) == 0))
    def _():
        cl = cl_ref[...]
        cc = cc_ref[...]
        sl_ref[...] = cl * _sigmoid(cl)
        sc_ref[...] = cc * _sigmoid(cc)

    tn = w_ref.shape[1]
    reps = tn // LANES

    def body(kg, carry):
        al, ac = carry
        r0 = pl.multiple_of(kg * SUBLANES, SUBLANES)
        w8 = w_ref[pl.ds(r0, SUBLANES), :]
        s8l = jnp.tile(sl_ref[pl.ds(r0, SUBLANES), :], (1, reps))
        s8c = jnp.tile(sc_ref[pl.ds(r0, SUBLANES), :], (1, reps))
        return al + w8 * s8l, ac + w8 * s8c

    zero = jnp.zeros((SUBLANES, tn), F32)
    al, ac = lax.fori_loop(0, w_ref.shape[0] // SUBLANES, body, (zero, zero), unroll=4)
    b = b_ref[...]
    o_ref[0] = jnp.sum(al, axis=0, keepdims=True) + b
    o_ref[1] = jnp.sum(ac, axis=0, keepdims=True) + b


def _modulation(c, c_ctx, w_mod, b_mod):
    d = D_MODEL
    n = N_MOD * d
    tn = 1024
    cl = jnp.broadcast_to(c.reshape(d, 1), (d, LANES))
    cc = jnp.broadcast_to(c_ctx.reshape(d, 1), (d, LANES))
    return pl.pallas_call(
        _mod_kernel,
        grid=(DEPTH, n // tn),
        in_specs=[
            pl.BlockSpec((d, LANES), lambda l, j: (0, 0)),
            pl.BlockSpec((d, LANES), lambda l, j: (0, 0)),
            pl.BlockSpec((None, d, tn), lambda l, j: (l, 0, j)),
            pl.BlockSpec((None, 1, tn), lambda l, j: (l, 0, j)),
        ],
        out_specs=pl.BlockSpec((None, 2, 1, tn), lambda l, j: (l, 0, 0, j)),
        out_shape=jax.ShapeDtypeStruct((DEPTH, 2, 1, n), F32),
        scratch_shapes=[pltpu.VMEM((d, LANES), F32), pltpu.VMEM((d, LANES), F32)],
        compiler_params=_params("arbitrary", "arbitrary"),
        name="modulation",
    )(cl, cc, w_mod, b_mod.reshape(DEPTH, 1, n))


def _mod_spec(layer, row, k, tn, col_of):
    per = D_MODEL // tn
    return pl.BlockSpec((None, None, 1, tn), lambda *g: (layer, row, 0, k * per + col_of(*g)))


NORM_ROWS = 16


def _norm_mod_kernel(x_ref, g_ref, sh_ref, sc_ref, o_ref):
    g = g_ref[...]
    gain = g + g * sc_ref[...]
    shift = sh_ref[...]
    for r0 in range(0, x_ref.shape[0], NORM_ROWS):
        x = x_ref[r0:r0 + NORM_ROWS, :]
        r = lax.rsqrt(jnp.mean(x * x, axis=-1, keepdims=True) + EPS)
        o_ref[r0:r0 + NORM_ROWS, :] = ((x * r) * gain + shift).astype(o_ref.dtype)


def _norm_mod(x, g, layer, mods, row, k_shift, k_scale):
    m, d = x.shape
    tm = min(m, 512)
    zero = lambda i: 0
    return pl.pallas_call(
        _norm_mod_kernel,
        grid=(m // tm,),
        in_specs=[
            pl.BlockSpec((tm, d), lambda i: (i, 0)),
            pl.BlockSpec((None, 1, d), lambda i: (layer, 0, 0)),
            _mod_spec(layer, row, k_shift, d, zero),
            _mod_spec(layer, row, k_scale, d, zero),
        ],
        out_specs=pl.BlockSpec((tm, d), lambda i: (i, 0)),
        out_shape=jax.ShapeDtypeStruct((m, d), BF16),
        compiler_params=_params("arbitrary"),
        name="norm_mod",
    )(x, g, mods, mods)


def _cast_weights(w_refs, wb_ref):
    @pl.when(pl.program_id(1) == 0)
    def _():
        c0 = 0
        for w_ref in w_refs:
            wn = w_ref.shape[1]
            wb_ref[:, c0:c0 + wn] = w_ref[...].astype(BF16)
            c0 += wn


def _proj_plain_kernel(n_w, h_ref, *refs):
    o_ref, wb_ref = refs[n_w:]
    _cast_weights(refs[:n_w], wb_ref)
    o_ref[...] = _dot(h_ref[...], wb_ref[...]).astype(o_ref.dtype)


def _blocked_dot(lhs_block, rows, epilogue):
    blk = min(rows, PROJ_BLOCK)
    for r0 in range(0, rows, blk):
        acc = lhs_block(r0, blk)
        for r in range(0, blk, ROW_CHUNK):
            epilogue(acc[r:r + ROW_CHUNK, :], r0 + r)


def _proj_gelu_kernel(n_w, h_ref, *refs):
    o_ref, wb_ref = refs[n_w:]
    _cast_weights(refs[:n_w], wb_ref)

    def epilogue(acc, r):
        o_ref[r:r + ROW_CHUNK, :] = _gelu_tanh(acc).astype(o_ref.dtype)

    _blocked_dot(lambda r0, n: _dot(h_ref[r0:r0 + n, :], wb_ref[...]), h_ref.shape[0], epilogue)


def _swap16(x, even):
    return jnp.where(even, pltpu.roll(x, LANES - 16, axis=1), pltpu.roll(x, 16, axis=1))


def _proj_qkv_kernel(rope, n_w, h_ref, *refs):
    if rope:
        cos_ref, sin_ref, o_ref, wb_ref = refs[n_w:]
    else:
        o_ref, wb_ref = refs[n_w:]
    _cast_weights(refs[:n_w], wb_ref)
    q_scale = HEAD_DIM ** -0.5
    n_rot = (ATT_WIDTH + KV_WIDTH) // LANES
    even = (lax.broadcasted_iota(jnp.int32, (ROW_CHUNK, LANES), 1) & 16) == 0

    def epilogue(acc, r):
        if rope:
            cos = cos_ref[r:r + ROW_CHUNK, :]
            sin = sin_ref[r:r + ROW_CHUNK, :]
        for cidx in range(acc.shape[1] // LANES):
            t = acc[:, cidx * LANES:(cidx + 1) * LANES]
            if cidx < ATT_WIDTH // LANES:
                t = t * q_scale
            if rope and cidx < n_rot:
                t = t * cos + _swap16(t, even) * sin
            o_ref[r:r + ROW_CHUNK, cidx * LANES:(cidx + 1) * LANES] = t.astype(o_ref.dtype)

    _blocked_dot(lambda r0, n: _dot(h_ref[r0:r0 + n, :], wb_ref[...]), h_ref.shape[0], epilogue)


def _proj(kind, h, w, w_index, col0, n, tn, wtn, out_dtype, tm=None, extra=()):
    m, k = h.shape
    tm = tm or min(m, 1024)
    assert col0 % wtn == 0 and tn % wtn == 0 and n % tn == 0 and m % tm == 0
    n_w = tn // wtn
    kernels = {
        "plain": _proj_plain_kernel,
        "gelu": _proj_gelu_kernel,
        "qkv": functools.partial(_proj_qkv_kernel, False),
        "qkv_rope": functools.partial(_proj_qkv_kernel, True),
    }
    in_specs = [pl.BlockSpec((tm, k), lambda j, i: (i, 0))]
    for p in range(n_w):
        in_specs.append(pl.BlockSpec((None, k, wtn),
                                     lambda j, i, p=p: (w_index, 0, col0 // wtn + j * n_w + p)))
    in_specs += [pl.BlockSpec((tm, LANES), lambda j, i: (i, 0)) for _ in extra]
    return pl.pallas_call(
        functools.partial(kernels[kind], n_w),
        grid=(n // tn, m // tm),
        in_specs=in_specs,
        out_specs=pl.BlockSpec((tm, tn), lambda j, i: (i, j)),
        out_shape=jax.ShapeDtypeStruct((m, n), out_dtype),
        scratch_shapes=[pltpu.VMEM((k, tn), BF16)],
        compiler_params=_params("arbitrary", "arbitrary"),
        name="proj_" + kind,
    )(h, *([w] * n_w), *extra)


def _proj_res_kernel(n_parts, *refs):
    a_refs = refs[:n_parts]
    w_ref, x_ref, gt_ref, o_ref, wb_ref = refs[n_parts:]
    _cast_weights([w_ref], wb_ref)

    def lhs_block(r0, n):
        acc = None
        k0 = 0
        for a_ref in a_refs:
            kp = a_ref.shape[1]
            part = _dot(a_ref[r0:r0 + n, :], wb_ref[k0:k0 + kp, :])
            acc = part if acc is None else acc + part
            k0 += kp
        return acc

    def epilogue(acc, r):
        o_ref[r:r + ROW_CHUNK, :] = x_ref[r:r + ROW_CHUNK, :] + gt_ref[...] * acc

    _blocked_dot(lhs_block, x_ref.shape[0], epilogue)


def _proj_residual(parts, w, w_index, x, layer, mods, row, k_gate, tm, tn):
    m, n = x.shape
    k = sum(p.shape[1] for p in parts)
    tm = min(m, tm)
    in_specs = [pl.BlockSpec((tm, p.shape[1]), lambda j, i: (i, 0)) for p in parts]
    in_specs += [
        pl.BlockSpec((None, k, tn), lambda j, i: (w_index, 0, j)),
        pl.BlockSpec((tm, tn), lambda j, i: (i, j)),
        _mod_spec(layer, row, k_gate, tn, lambda j, i: j),
    ]
    return pl.pallas_call(
        functools.partial(_proj_res_kernel, len(parts)),
        grid=(n // tn, m // tm),
        in_specs=in_specs,
        out_specs=pl.BlockSpec((tm, tn), lambda j, i: (i, j)),
        out_shape=jax.ShapeDtypeStruct((m, n), F32),
        scratch_shapes=[pltpu.VMEM((k, tn), BF16)],
        compiler_params=_params("arbitrary", "arbitrary"),
        name="proj_residual",
    )(*parts, w, x, mods)


def _conv_proj_kernel(n_w, kw, left, gated, n_row_tiles, n_sub, h_ref, hp_ref, hn_ref, *refs):
    w_refs = refs[:n_w]
    cw_refs = refs[n_w:2 * n_w]
    cb_refs = refs[2 * n_w:3 * n_w]
    o_ref, wb_ref, ext_ref, z_even, z_odd = refs[3 * n_w:]
    s = pl.program_id(0)
    n_tiles = pl.num_programs(0) - 1
    i = jnp.minimum(s, n_tiles - 1) % n_row_tiles
    tm = h_ref.shape[0]
    tn = w_refs[0].shape[1]
    sub_rows = ext_ref.shape[0] // n_sub

    @pl.when(i == 0)
    def _():
        for p, w_ref in enumerate(w_refs):
            wb_ref[:, p * tn:(p + 1) * tn] = w_ref[...].astype(BF16)

    @pl.when(s == 0)
    def _():
        z_odd[...] = jnp.zeros_like(z_odd)

    ext_ref[0:HALO, :] = jnp.where(i > 0, hp_ref[...], jnp.zeros_like(hp_ref))
    ext_ref[HALO:HALO + tm, :] = h_ref[...]
    ext_ref[HALO + tm:, :] = jnp.where(i < n_row_tiles - 1, hn_ref[...], jnp.zeros_like(hn_ref))

    out_slabs = tn // LANES
    units = [(so, r) for so in range(out_slabs) for r in range(0, tm, ROW_CHUNK)]
    per_sub = -(-len(units) // n_sub)

    def epilogue_unit(z_old, so, r):
        c0 = so * LANES
        outs = []
        for p in range(n_w):
            slab = p * out_slabs + so
            u = cb_refs[p][:, c0:c0 + LANES]
            for tap in range(kw):
                r0 = HALO - left + tap + r
                u = u + z_old[slab, r0:r0 + ROW_CHUNK, :] * cw_refs[p][tap:tap + 1, c0:c0 + LANES]
            outs.append(u)
        if gated:
            g, v = outs
            res = g * _sigmoid(g) * v
        else:
            res = outs[0]
        o_ref[r:r + ROW_CHUNK, c0:c0 + LANES] = res.astype(o_ref.dtype)

    def step(z_new, z_old):
        for sub in range(n_sub):
            rows = slice(sub * sub_rows, (sub + 1) * sub_rows)
            zc = _dot(ext_ref[rows, :], wb_ref[...])
            for t in range(n_w * out_slabs):
                z_new[t, rows, :] = zc[:, t * LANES:(t + 1) * LANES]
            for so, r in units[sub * per_sub:(sub + 1) * per_sub]:
                epilogue_unit(z_old, so, r)

    @pl.when(s % 2 == 0)
    def _():
        step(z_even, z_odd)

    @pl.when(s % 2 == 1)
    def _():
        step(z_odd, z_even)


def _conv_proj(h, w, w_index, col_blocks, conv_blocks, n_out, cw, cb, kw, left, gated, tm, tn, out_dtype):
    m, k = h.shape
    tm = min(m, tm)
    n_w = len(col_blocks)
    hb = m // HALO
    tb = tm // HALO
    ni = m // tm
    n_tiles = (n_out // tn) * ni
    cur = lambda s: jnp.minimum(s, n_tiles - 1)
    prv = lambda s: jnp.maximum(s - 1, 0)
    in_specs = [
        pl.BlockSpec((tm, k), lambda s: (cur(s) % ni, 0)),
        pl.BlockSpec((HALO, k), lambda s: (jnp.maximum((cur(s) % ni) * tb - 1, 0), 0)),
        pl.BlockSpec((HALO, k), lambda s: (jnp.minimum((cur(s) % ni + 1) * tb, hb - 1), 0)),
    ]
    for c0 in col_blocks:
        in_specs.append(pl.BlockSpec((None, k, tn), lambda s, c0=c0: (w_index, 0, c0 + cur(s) // ni)))
    for c0 in conv_blocks:
        in_specs.append(pl.BlockSpec((None, kw, tn), lambda s, c0=c0: (w_index, 0, c0 + prv(s) // ni)))
    for c0 in conv_blocks:
        in_specs.append(pl.BlockSpec((None, 1, tn), lambda s, c0=c0: (w_index, 0, c0 + prv(s) // ni)))
    args = [h, h, h] + [w] * n_w + [cw] * n_w + [cb] * n_w
    z_shape = (n_w * tn // LANES, tm + 2 * HALO, LANES)
    packed = (tm + 2 * HALO) // HALO
    n_sub = max(d for d in range(1, CONV_SUB_BLOCKS + 1) if packed % d == 0)
    return pl.pallas_call(
        functools.partial(_conv_proj_kernel, n_w, kw, left, gated, ni, n_sub),
        grid=(n_tiles + 1,),
        in_specs=in_specs,
        out_specs=pl.BlockSpec((tm, tn), lambda s: (prv(s) % ni, prv(s) // ni)),
        out_shape=jax.ShapeDtypeStruct((m, n_out), out_dtype),
        scratch_shapes=[pltpu.VMEM((k, n_w * tn), BF16), pltpu.VMEM((tm + 2 * HALO, k), BF16),
                        pltpu.VMEM(z_shape, F32), pltpu.VMEM(z_shape, F32)],
        compiler_params=_params("arbitrary"),
        name="conv_proj",
    )(*args)


def _pair_operand(band, kv_head):
    b = band.astype(F32)
    rolled = pltpu.roll(b, HEAD_DIM, axis=1)
    low = lax.broadcasted_iota(jnp.int32, b.shape, 1) < HEAD_DIM
    zero = jnp.zeros_like(b)
    if kv_head == 0:
        top = jnp.where(low, b, zero)
        bot = jnp.where(low, zero, rolled)
    else:
        top = jnp.where(low, rolled, zero)
        bot = jnp.where(low, zero, b)
    return jnp.concatenate([top, bot], axis=0).astype(BF16)


ATT_ROWS = 32


def _attention_core(q_ref, k_band, v_band, biases, sink_ref, fa_index, o_ref, p_all, inv_all):
    tq = q_ref.shape[0]
    nk = k_band.shape[0]
    low = lax.broadcasted_iota(jnp.int32, (ATT_ROWS, LANES), 1) < HEAD_DIM
    heads_per_kv = N_HEADS // N_KV_HEADS
    for kv_head in range(N_KV_HEADS):
        k2 = _pair_operand(k_band, kv_head)
        v2 = _pair_operand(v_band, kv_head)
        for pair in range(heads_per_kv // 2):
            p_scr = p_all.at[pair % 2]
            inv_scr = inv_all.at[pair % 2]
            c0 = (kv_head * (heads_per_kv // 2) + pair) * LANES
            head = c0 // HEAD_DIM
            s = _dot_nt(q_ref[:, c0:c0 + LANES], k2)
            for r in range(0, tq, ATT_ROWS):
                inv = []
                for half in range(2):
                    sink = sink_ref[fa_index, head + half]
                    cols = []
                    for j in range(nk // LANES):
                        blk = s[r:r + ATT_ROWS, half * nk + j * LANES:half * nk + (j + 1) * LANES]
                        if j in biases:
                            blk = blk + biases[j][r:r + ATT_ROWS, :]
                        cols.append(blk)
                    top = cols[0]
                    for blk in cols[1:]:
                        top = jnp.maximum(top, blk)
                    mx = jnp.maximum(jnp.max(top, axis=1, keepdims=True), sink)
                    tot = None
                    for j, blk in enumerate(cols):
                        p = jnp.exp(blk - mx)
                        tot = p if tot is None else tot + p
                        p_scr[r:r + ATT_ROWS, half * nk + j * LANES:half * nk + (j + 1) * LANES] = p.astype(BF16)
                    den = jnp.sum(tot, axis=1, keepdims=True) + jnp.exp(sink - mx)
                    inv.append(1.0 / den)
                inv_scr[r:r + ATT_ROWS, :] = jnp.where(low, inv[0], inv[1])
            o = _dot(p_scr[...], v2)
            o_ref[:, c0:c0 + LANES] = (o * inv_scr[...]).astype(o_ref.dtype)


def _attn_lat_kernel(fa_index, sink_ref, q_ref, kp_ref, kc_ref, kn_ref, vp_ref, vc_ref, vn_ref,
                     kx_ref, vx_ref, o_ref, p_scr, inv_scr):
    n = pl.program_id(0)
    nb = pl.num_programs(0)
    k_band = jnp.concatenate([kp_ref[...], kc_ref[...], kn_ref[...], kx_ref[...]], axis=0)
    v_band = jnp.concatenate([vp_ref[...], vc_ref[...], vn_ref[...], vx_ref[...]], axis=0)
    qi = lax.broadcasted_iota(jnp.int32, (BLOCK, BLOCK), 0)
    kj = lax.broadcasted_iota(jnp.int32, (BLOCK, BLOCK), 1)
    bias_prev = jnp.where((kj >= qi) & (n > 0), 0.0, NEG_INF).astype(F32)
    bias_next = jnp.where((kj <= qi) & (n < nb - 1), 0.0, NEG_INF).astype(F32)
    _attention_core(q_ref, k_band, v_band, {0: bias_prev, 2: bias_next}, sink_ref, fa_index, o_ref,
                    p_scr, inv_scr)


def _attn_ctx_kernel(fa_index, sink_ref, q_ref, kx_ref, vx_ref, o_ref, p_scr, inv_scr):
    _attention_core(q_ref, kx_ref[...], vx_ref[...], {}, sink_ref, fa_index, o_ref, p_scr, inv_scr)


def _attention_lat(qkv, qkv_ctx, sink, fa_index):
    s = qkv.shape[0]
    n_ctx = qkv_ctx.shape[0]
    nb = s // BLOCK
    kcol = ATT_WIDTH // LANES
    vcol = kcol + 1
    prev = lambda n: jnp.maximum(n - 1, 0)
    nxt = lambda n: jnp.minimum(n + 1, nb - 1)
    in_specs = [
        pl.BlockSpec(memory_space=pltpu.SMEM),
        pl.BlockSpec((BLOCK, ATT_WIDTH), lambda n: (n, 0)),
        pl.BlockSpec((BLOCK, LANES), lambda n: (prev(n), kcol)),
        pl.BlockSpec((BLOCK, LANES), lambda n: (n, kcol)),
        pl.BlockSpec((BLOCK, LANES), lambda n: (nxt(n), kcol)),
        pl.BlockSpec((BLOCK, LANES), lambda n: (prev(n), vcol)),
        pl.BlockSpec((BLOCK, LANES), lambda n: (n, vcol)),
        pl.BlockSpec((BLOCK, LANES), lambda n: (nxt(n), vcol)),
        pl.BlockSpec((n_ctx, LANES), lambda n: (0, kcol)),
        pl.BlockSpec((n_ctx, LANES), lambda n: (0, vcol)),
    ]
    return pl.pallas_call(
        functools.partial(_attn_lat_kernel, fa_index),
        grid=(nb,),
        in_specs=in_specs,
        out_specs=pl.BlockSpec((BLOCK, ATT_WIDTH), lambda n: (n, 0)),
        out_shape=jax.ShapeDtypeStruct((s, ATT_WIDTH), BF16),
        scratch_shapes=[pltpu.VMEM((2, BLOCK, 2 * (3 * BLOCK + n_ctx)), BF16),
                        pltpu.VMEM((2, BLOCK, LANES), F32)],
        compiler_params=_params("arbitrary"),
        name="attention_latent",
    )(sink, qkv, qkv, qkv, qkv, qkv, qkv, qkv, qkv_ctx, qkv_ctx)


def _attention_ctx(qkv_ctx, sink, fa_index):
    n_ctx = qkv_ctx.shape[0]
    kcol = ATT_WIDTH // LANES
    return pl.pallas_call(
        functools.partial(_attn_ctx_kernel, fa_index),
        grid=(1,),
        in_specs=[
            pl.BlockSpec(memory_space=pltpu.SMEM),
            pl.BlockSpec((n_ctx, ATT_WIDTH), lambda n: (0, 0)),
            pl.BlockSpec((n_ctx, LANES), lambda n: (0, kcol)),
            pl.BlockSpec((n_ctx, LANES), lambda n: (0, kcol + 1)),
        ],
        out_specs=pl.BlockSpec((n_ctx, ATT_WIDTH), lambda n: (0, 0)),
        out_shape=jax.ShapeDtypeStruct((n_ctx, ATT_WIDTH), BF16),
        scratch_shapes=[pltpu.VMEM((2, n_ctx, 2 * n_ctx), BF16), pltpu.VMEM((2, n_ctx, LANES), F32)],
        compiler_params=_params("arbitrary"),
        name="attention_context",
    )(sink, qkv_ctx, qkv_ctx, qkv_ctx)


FFT_N1 = 64
FFT_N2 = 128


def _dft_cos_sin(n):
    idx = np.arange(n)
    ang = 2.0 * np.pi * ((idx[:, None] * idx[None, :]) % n) / n
    return np.cos(ang), np.sin(ang)


def _fourier_constants():
    c1, s1 = _dft_cos_sin(FFT_N1)
    stage1 = np.concatenate([c1, -s1], axis=0)
    c2, s2 = _dft_cos_sin(FFT_N2)
    stage2 = np.block([[c2, s2], [-s2, c2]])
    cc, sc = _dft_cos_sin(FG_W)
    chan = np.concatenate([cc, sc], axis=0)
    return (jnp.asarray(stage1, F32), jnp.asarray(stage2, F32), jnp.asarray(chan, F32))


def _twiddle_tables():
    k1 = jnp.arange(FFT_N1, dtype=jnp.int32)[:, None]
    n2 = jnp.arange(FFT_N2, dtype=jnp.int32)[None, :]
    ang = ((k1 * n2) % (FFT_N1 * FFT_N2)).astype(F32) * (2.0 * math.pi / (FFT_N1 * FFT_N2))
    wr = jnp.repeat(jnp.cos(ang), LANES, axis=1)
    wi = jnp.repeat(-jnp.sin(ang), LANES, axis=1)
    return wr, wi


def _fourier_stage1_kernel(x_ref, m_ref, wr_ref, wi_ref, tr_ref, ti_ref):
    y = _dot(m_ref[...].astype(BF16), x_ref[...].astype(BF16))
    reps = F_WIDTH // LANES
    for b in range(x_ref.shape[1] // F_WIDTH):
        cols = slice(b * F_WIDTH, (b + 1) * F_WIDTH)
        yr = y[:FFT_N1, cols]
        yi = y[FFT_N1:, cols]
        wr = jnp.tile(wr_ref[:, b * LANES:(b + 1) * LANES], (1, reps))
        wi = jnp.tile(wi_ref[:, b * LANES:(b + 1) * LANES], (1, reps))
        tr_ref[:, cols] = (yr * wr - yi * wi).astype(tr_ref.dtype)
        ti_ref[:, cols] = (yr * wi + yi * wr).astype(ti_ref.dtype)


def _channel_stage(p, chan_ref, scale, o_ref):
    r = p.shape[0] // 2
    pr = p[:r].astype(BF16)
    pi = p[r:].astype(BF16)
    chan_c = chan_ref[:FG_W, :].astype(BF16)
    chan_s = chan_ref[FG_W:, :].astype(BF16)
    for g in range(N_FG):
        cols = slice(g * FG_W, (g + 1) * FG_W)
        y = _dot(pr[:, cols], chan_c) + _dot(pi[:, cols], chan_s)
        o_ref[:, cols] = (y * scale).astype(o_ref.dtype)


def _fourier_stage2_kernel(scale, tr_ref, ti_ref, m_ref, chan_ref, o_ref):
    m = m_ref[...].astype(BF16)
    for kk in range(tr_ref.shape[0]):
        t = jnp.concatenate([tr_ref[kk], ti_ref[kk]], axis=0)
        _channel_stage(_dot(m, t), chan_ref, scale, o_ref.at[:, kk * F_WIDTH:(kk + 1) * F_WIDTH])


def _fourier_ctx_kernel(scale, x_ref, m_ref, chan_ref, o_ref):
    _channel_stage(_dot(m_ref[...].astype(BF16), x_ref[...].astype(BF16)), chan_ref, scale, o_ref)


def _fourier_lat(f, consts, twiddles):
    n = f.shape[0]
    assert n == FFT_N1 * FFT_N2
    stage1, stage2, chan = consts
    wr, wi = twiddles
    n2_blk = 8
    cols = n2_blk * F_WIDTH
    full = lambda a: pl.BlockSpec(a.shape, lambda j: (0,) * a.ndim)
    tr, ti = pl.pallas_call(
        _fourier_stage1_kernel,
        grid=(FFT_N2 // n2_blk,),
        in_specs=[
            pl.BlockSpec((FFT_N1, cols), lambda j: (0, j)),
            full(stage1),
            pl.BlockSpec((FFT_N1, n2_blk * LANES), lambda j: (0, j)),
            pl.BlockSpec((FFT_N1, n2_blk * LANES), lambda j: (0, j)),
        ],
        out_specs=[pl.BlockSpec((FFT_N1, cols), lambda j: (0, j))] * 2,
        out_shape=[jax.ShapeDtypeStruct((FFT_N1, FFT_N2 * F_WIDTH), BF16)] * 2,
        compiler_params=_params("arbitrary"),
        name="fourier_stage1",
    )(f.reshape(FFT_N1, FFT_N2 * F_WIDTH), stage1, wr, wi)
    tr = tr.reshape(FFT_N1, FFT_N2, F_WIDTH)
    ti = ti.reshape(FFT_N1, FFT_N2, F_WIDTH)
    scale = 1.0 / math.sqrt(n * FG_W)
    k1_blk = 4
    out = pl.pallas_call(
        functools.partial(_fourier_stage2_kernel, scale),
        grid=(FFT_N1 // k1_blk,),
        in_specs=[
            pl.BlockSpec((k1_blk, FFT_N2, F_WIDTH), lambda k1: (k1, 0, 0)),
            pl.BlockSpec((k1_blk, FFT_N2, F_WIDTH), lambda k1: (k1, 0, 0)),
            full(stage2),
            full(chan),
        ],
        out_specs=pl.BlockSpec((FFT_N2, k1_blk * F_WIDTH), lambda k1: (0, k1)),
        out_shape=jax.ShapeDtypeStruct((FFT_N2, FFT_N1 * F_WIDTH), BF16),
        compiler_params=_params("arbitrary"),
        name="fourier_stage2",
    )(tr, ti, stage2, chan)
    return out.reshape(n, F_WIDTH)


def _fourier_ctx(f, chan):
    n = f.shape[0]
    c, s = _dft_cos_sin(n)
    m = jnp.asarray(np.concatenate([c, -s], axis=0), F32)
    full = lambda a: pl.BlockSpec(a.shape, lambda j: (0,) * a.ndim)
    return pl.pallas_call(
        functools.partial(_fourier_ctx_kernel, 1.0 / math.sqrt(n * FG_W)),
        grid=(1,),
        in_specs=[full(f), full(m), full(chan)],
        out_specs=pl.BlockSpec((n, F_WIDTH), lambda j: (0, 0)),
        out_shape=jax.ShapeDtypeStruct((n, F_WIDTH), BF16),
        compiler_params=_params("arbitrary"),
        name="fourier_context",
    )(f, m, chan)


def _scan8(a, b, row, reverse):
    for d in (1, 2, 4):
        shift = (SUBLANES - d) if reverse else d
        keep = (row < SUBLANES - d) if reverse else (row >= d)
        b = jnp.where(keep, b + a * pltpu.roll(b, shift, axis=0), b)
        a = jnp.where(keep, a * pltpu.roll(a, shift, axis=0), a)
    return a, b


def _rows(x, n):
    return jnp.broadcast_to(x, (n, x.shape[1]))


def _rglru_kernel(reverse, combine, xs_ref, wa_ref, wi_ref, ba_ref, bi_ref, lam_ref, h0_ref, *refs):
    if combine:
        hb_ref, gg_ref, o_ref, last_ref, w_scr, a_scr, b_scr, c_scr, carry_scr = refs
    else:
        o_ref, last_ref, w_scr, a_scr, b_scr, c_scr, carry_scr = refs
    t = pl.program_id(1)
    tt, c = xs_ref.shape
    n_slab = c // LANES
    groups = tt // SUBLANES
    blocks = groups // SUBLANES
    row = lax.broadcasted_iota(jnp.int32, (SUBLANES, LANES), 0)
    edge = 0 if reverse else SUBLANES - 1

    @pl.when(t == 0)
    def _():
        carry_scr[...] = jnp.broadcast_to(h0_ref[...], carry_scr.shape)
        w_scr[:, :c] = (0.5 * wa_ref[...]).astype(BF16)
        w_scr[:, c:] = (0.5 * wi_ref[...]).astype(BF16)

    pre = _dot(xs_ref[...].astype(BF16), w_scr[...])
    neg_lam = -lam_ref[...]
    softplus = jnp.maximum(neg_lam, 0.0) + jnp.log1p(jnp.exp(-jnp.abs(neg_lam)))
    k_all = softplus * (-0.5 * LRU_C * math.log2(math.e))
    ba_all = 0.5 * ba_ref[...]
    bi_all = 0.5 * bi_ref[...]

    for s in range(n_slab):
        lanes = slice(s * LANES, (s + 1) * LANES)
        k = _rows(k_all[:, lanes], SUBLANES)
        ba = _rows(ba_all[:, lanes], SUBLANES)
        bi = _rows(bi_all[:, lanes], SUBLANES)
        for g in range(groups):
            r0 = g * SUBLANES
            tr = jnp.tanh(pre[r0:r0 + SUBLANES, s * LANES:(s + 1) * LANES] + ba)
            ti = jnp.tanh(pre[r0:r0 + SUBLANES, c + s * LANES:c + (s + 1) * LANES] + bi)
            a = jnp.exp2((1.0 + tr) * k)
            y = 1.0 - a * a
            root = jnp.where(y > 0.0, y * lax.rsqrt(y), 0.0)
            b = root * ((1.0 + ti) * (0.5 * xs_ref[r0:r0 + SUBLANES, lanes]))
            a_cum, b_loc = _scan8(a, b, row, reverse)
            a_scr[s, r0:r0 + SUBLANES, :] = a_cum
            b_scr[s, r0:r0 + SUBLANES, :] = b_loc

    for s in range(n_slab):
        a2 = a_scr[s, pl.ds(edge, groups, stride=SUBLANES), :]
        b2 = b_scr[s, pl.ds(edge, groups, stride=SUBLANES), :]
        carry = carry_scr[:, s * LANES:(s + 1) * LANES]
        enter_row = groups if reverse else SUBLANES - 1
        base = 0 if reverse else SUBLANES
        c_scr[s, enter_row:enter_row + 1, :] = carry[0:1, :]
        for j in (range(blocks - 1, -1, -1) if reverse else range(blocks)):
            r0 = j * SUBLANES
            a_cum, b_loc = _scan8(a2[r0:r0 + SUBLANES, :], b2[r0:r0 + SUBLANES, :], row, reverse)
            st = b_loc + a_cum * carry
            c_scr[s, base + r0:base + r0 + SUBLANES, :] = st
            carry = _rows(st[edge:edge + 1, :], SUBLANES)
        carry_scr[:, s * LANES:(s + 1) * LANES] = carry
        last_ref[:, s * LANES:(s + 1) * LANES] = carry[0:1, :]

    pair = 2 * SUBLANES
    for s in range(n_slab):
        lanes = slice(s * LANES, (s + 1) * LANES)
        for g in range(0, groups, 2):
            r0 = g * SUBLANES
            src = g + 1 if reverse else g + SUBLANES - 1
            enter = jnp.concatenate([_rows(c_scr[s, src:src + 1, :], SUBLANES),
                                     _rows(c_scr[s, src + 1:src + 2, :], SUBLANES)], axis=0)
            h = b_scr[s, r0:r0 + pair, :] + a_scr[s, r0:r0 + pair, :] * enter
            if combine:
                h = (h + hb_ref[r0:r0 + pair, lanes]) * gg_ref[r0:r0 + pair, lanes].astype(F32)
            o_ref[r0:r0 + pair, lanes] = h.astype(o_ref.dtype)


def _rglru_scan(xs, w_a, w_i, b_a, b_i, lam, rg_index, direction, h0, h_other=None, gelu_gate=None):
    m = xs.shape[0]
    tt = min(m, 1024)
    nt = m // tt
    c = RNN_BLOCK
    reverse = direction == 1
    combine = h_other is not None
    tix = (lambda t: nt - 1 - t) if reverse else (lambda t: t)
    wspec = pl.BlockSpec((None, None, None, c, c), lambda cb, t: (rg_index, direction, cb, 0, 0))
    vspec = pl.BlockSpec((None, None, 1, c), lambda cb, t: (rg_index, direction, 0, cb))
    tile = pl.BlockSpec((tt, c), lambda cb, t: (tix(t), cb))
    in_specs = [tile, wspec, wspec, vspec, vspec, vspec, pl.BlockSpec((1, c), lambda cb, t: (0, cb))]
    args = [xs, w_a, w_i, b_a, b_i, lam, h0]
    if combine:
        in_specs += [tile, tile]
        args += [h_other, gelu_gate]
    return pl.pallas_call(
        functools.partial(_rglru_kernel, reverse, combine),
        grid=(D_RNN // c, nt),
        in_specs=in_specs,
        out_specs=[tile, pl.BlockSpec((1, c), lambda cb, t: (0, cb))],
        out_shape=[jax.ShapeDtypeStruct((m, D_RNN), BF16 if combine else F32),
                   jax.ShapeDtypeStruct((1, D_RNN), F32)],
        scratch_shapes=[pltpu.VMEM((c, 2 * c), BF16),
                        pltpu.VMEM((c // LANES, tt, LANES), F32),
                        pltpu.VMEM((c // LANES, tt, LANES), F32),
                        pltpu.VMEM((c // LANES, tt // SUBLANES + 2 * SUBLANES, LANES), F32),
                        pltpu.VMEM((SUBLANES, c), F32)],
        compiler_params=_params("arbitrary", "arbitrary"),
        name="rglru_scan",
    )(*args)


def _final_norm_kernel(x_ref, g_ref, o_ref):
    g = g_ref[...]
    for r0 in range(0, x_ref.shape[0], NORM_ROWS):
        x = x_ref[r0:r0 + NORM_ROWS, :]
        r = lax.rsqrt(jnp.mean(x * x, axis=-1, keepdims=True) + EPS)
        o_ref[r0:r0 + NORM_ROWS, :] = (x * r) * g


def _final_norm(x, g):
    m, d = x.shape
    tm = 512
    return pl.pallas_call(
        _final_norm_kernel,
        grid=(m // tm,),
        in_specs=[pl.BlockSpec((tm, d), lambda i: (i, 0)), pl.BlockSpec((1, d), lambda i: (0, 0))],
        out_specs=pl.BlockSpec((tm, d), lambda i: (i, 0)),
        out_shape=jax.ShapeDtypeStruct((m, d), F32),
        compiler_params=_params("arbitrary"),
        name="final_norm",
    )(x, g.reshape(1, d))


def _rope_tables(n):
    f = HEAD_DIM // 4
    inv = ROPE_BASE ** (-jnp.arange(f, dtype=F32) / f)
    pos = jnp.arange(n, dtype=jnp.int32)
    ang_r = (pos // GRID_W).astype(F32)[:, None] * inv[None, :]
    ang_c = (pos % GRID_W).astype(F32)[:, None] * inv[None, :]
    cr, sr, cc, sc = jnp.cos(ang_r), jnp.sin(ang_r), jnp.cos(ang_c), jnp.sin(ang_c)
    cos = jnp.concatenate([cr, cr, cc, cc], axis=1)
    sin = jnp.concatenate([-sr, sr, -sc, sc], axis=1)
    reps = LANES // HEAD_DIM
    return jnp.tile(cos, (1, reps)), jnp.tile(sin, (1, reps))


def _fourier_attn_layer(layer, i, x_lat, x_ctx, h_lat, h_ctx, mods, fa_w_in, fa_w_out, attn_sink,
                        tables, ctx_out):
    rope, consts, twiddles = tables
    qkv_w = ATT_WIDTH + 2 * KV_WIDTH
    f_lat = _proj("plain", h_lat, fa_w_in, i, 0, F_WIDTH, F_WIDTH, F_WIDTH, BF16)
    qkv_lat = _proj("qkv_rope", h_lat, fa_w_in, i, F_WIDTH, qkv_w, qkv_w, 256, BF16, tm=512, extra=rope)
    qkv_ctx = _proj("qkv", h_ctx, fa_w_in, i, F_WIDTH, qkv_w, qkv_w, 256, BF16)
    fo_lat = _fourier_lat(f_lat, consts, twiddles)
    ao_lat = _attention_lat(qkv_lat, qkv_ctx, attn_sink, i)
    x_lat = _proj_residual([fo_lat, ao_lat], fa_w_out, i, x_lat, layer, mods, 0, 2, 1024, 1024)
    if ctx_out:
        f_ctx = _proj("plain", h_ctx, fa_w_in, i, 0, F_WIDTH, F_WIDTH, F_WIDTH, BF16)
        fo_ctx = _fourier_ctx(f_ctx, consts[2])
        ao_ctx = _attention_ctx(qkv_ctx, attn_sink, i)
        x_ctx = _proj_residual([fo_ctx, ao_ctx], fa_w_out, i, x_ctx, layer, mods, 1, 2, 1024, 1024)
    return x_lat, x_ctx


def _rglru_layer(layer, i, x_lat, x_ctx, h_lat, h_ctx, mods, rg_w_in, rg_conv_w, rg_conv_b, w_a, b_a,
                 w_i, b_i, lam, rg_w_out, ctx_out):
    tn = 1024
    xcol = D_RNN // tn
    zero_state = jnp.zeros((1, D_RNN), F32)
    gate_lat = _proj("gelu", h_lat, rg_w_in, i, 0, D_RNN, tn, tn, BF16)
    xs_lat = _conv_proj(h_lat, rg_w_in, i, [xcol], [0], D_RNN, rg_conv_w, rg_conv_b, CONV_W, CONV_LEFT,
                        False, 1024, tn, F32)
    xs_ctx = _conv_proj(h_ctx, rg_w_in, i, [xcol], [0], D_RNN, rg_conv_w, rg_conv_b, CONV_W, CONV_LEFT,
                        False, 1024, tn, F32)
    scan = functools.partial(_rglru_scan, w_a=w_a, w_i=w_i, b_a=b_a, b_i=b_i, lam=lam, rg_index=i)
    hb_ctx, s_bwd = scan(xs_ctx, direction=1, h0=zero_state)
    if ctx_out:
        gate_ctx = _proj("gelu", h_ctx, rg_w_in, i, 0, D_RNN, tn, tn, BF16)
        y_ctx, s_fwd = scan(xs_ctx, direction=0, h0=zero_state, h_other=hb_ctx, gelu_gate=gate_ctx)
    else:
        _, s_fwd = scan(xs_ctx, direction=0, h0=zero_state)
    hb_lat, _ = scan(xs_lat, direction=1, h0=s_bwd)
    y_lat, _ = scan(xs_lat, direction=0, h0=s_fwd, h_other=hb_lat, gelu_gate=gate_lat)
    x_lat = _proj_residual([y_lat], rg_w_out, i, x_lat, layer, mods, 0, 2, 1024, 1024)
    if ctx_out:
        x_ctx = _proj_residual([y_ctx], rg_w_out, i, x_ctx, layer, mods, 1, 2, 1024, 1024)
    return x_lat, x_ctx


def _conv_ffn(layer, x, g_ffn, mods, row, w_up, conv_w, conv_b, w_down):
    h = _norm_mod(x, g_ffn, layer, mods, row, 3, 4)
    tf = 512
    blocks = [0, D_FF // tf]
    act = _conv_proj(h, w_up, layer, blocks, blocks, D_FF, conv_w, conv_b, FFN_CONV_W, FFN_CONV_LEFT, True,
                     1024, tf, BF16)
    return _proj_residual([act], w_down, layer, x, layer, mods, row, 5, 512, 512)


def kernel(x, c, ctx, c_ctx, w_mod, b_mod, g_mix, g_ffn, fa_w_in, fa_w_out, attn_sink, rg_w_in, rg_conv_w,
           rg_conv_b, rg_w_a, rg_b_a, rg_w_i, rg_b_i, rg_lambda, rg_w_out, ffn_w_up, ffn_conv_w, ffn_conv_b,
           ffn_w_down, g_final):
    assert x.shape[0] == 1 and ctx.shape[0] == 1
    n = x.shape[1]
    x_lat = x[0]
    x_ctx = ctx[0]
    mods = _modulation(c, c_ctx, w_mod, b_mod)
    tables = (_rope_tables(n), _fourier_constants(), _twiddle_tables())
    g_mix3 = g_mix.reshape(DEPTH, 1, D_MODEL)
    g_ffn3 = g_ffn.reshape(DEPTH, 1, D_MODEL)
    n_rg = rg_conv_b.shape[0]
    rg_conv_b3 = rg_conv_b.reshape(n_rg, 1, D_RNN)
    rg_b_a4 = rg_b_a.reshape(n_rg, 2, 1, D_RNN)
    rg_b_i4 = rg_b_i.reshape(n_rg, 2, 1, D_RNN)
    rg_lam4 = rg_lambda.reshape(n_rg, 2, 1, D_RNN)
    ffn_conv_b3 = ffn_conv_b.reshape(DEPTH, 1, 2 * D_FF)
    for layer in range(DEPTH):
        ctx_out = layer < DEPTH - 1
        i = layer // 2
        h_lat = _norm_mod(x_lat, g_mix3, layer, mods, 0, 0, 1)
        h_ctx = _norm_mod(x_ctx, g_mix3, layer, mods, 1, 0, 1)
        if layer % 2 == 0:
            x_lat, x_ctx = _fourier_attn_layer(layer, i, x_lat, x_ctx, h_lat, h_ctx, mods, fa_w_in, fa_w_out,
                                               attn_sink, tables, ctx_out)
        else:
            x_lat, x_ctx = _rglru_layer(layer, i, x_lat, x_ctx, h_lat, h_ctx, mods, rg_w_in, rg_conv_w,
                                        rg_conv_b3, rg_w_a, rg_b_a4, rg_w_i, rg_b_i4, rg_lam4, rg_w_out,
                                        ctx_out)
        x_lat = _conv_ffn(layer, x_lat, g_ffn3, mods, 0, ffn_w_up, ffn_conv_w, ffn_conv_b3, ffn_w_down)
        if ctx_out:
            x_ctx = _conv_ffn(layer, x_ctx, g_ffn3, mods, 1, ffn_w_up, ffn_conv_w, ffn_conv_b3, ffn_w_down)
    return _final_norm(x_lat, g_final)[None]
```

```python
import functools
import math

import numpy as np
import jax
import jax.numpy as jnp
from jax import lax
from jax.experimental import pallas as pl
from jax.experimental.pallas import tpu as pltpu

D_MODEL = 2048
DEPTH = 4
GRID_W = 64
N_FG = 4
FG_W = 256
F_WIDTH = N_FG * FG_W
N_HEADS = 16
N_KV_HEADS = 2
HEAD_DIM = 64
ATT_WIDTH = N_HEADS * HEAD_DIM
KV_WIDTH = N_KV_HEADS * HEAD_DIM
WINDOW = 128
BLOCK = 128
ROPE_BASE = 10000.0
D_RNN = D_MODEL
N_RNN_BLOCKS = 8
RNN_BLOCK = D_RNN // N_RNN_BLOCKS
CONV_W = 4
CONV_LEFT = 2
LRU_C = 8.0
D_FF = 5632
FFN_CONV_W = 3
FFN_CONV_LEFT = 1
N_MOD = 6
EPS = 1e-6
NEG_INF = -1e30
LOG2E = math.log2(math.e)

LANES = 128
SUBLANES = 8
HALO = 16
VMEM_LIMIT = 56 * 1024 * 1024
ROW_CHUNK = 64
PROJ_BLOCK = 256
CONV_SUB_BLOCKS = 6

BF16 = jnp.bfloat16
F32 = jnp.float32


def _params(*sem):
    return pltpu.CompilerParams(dimension_semantics=sem, vmem_limit_bytes=VMEM_LIMIT)


def _dot(a, b):
    return jnp.dot(a, b, preferred_element_type=F32)


def _dot_nt(a, b):
    return lax.dot_general(a, b, (((1,), (1,)), ((), ())), preferred_element_type=F32)


def _gelu_tanh(x):
    return 0.5 * x * (1.0 + jnp.tanh(math.sqrt(2.0 / math.pi) * (x + 0.044715 * (x * x * x))))


def _sigmoid(x):
    return 0.5 * (1.0 + jnp.tanh(0.5 * x))


def _mod_kernel(cl_ref, cc_ref, w_ref, b_ref, o_ref, sl_ref, sc_ref):
    @pl.when((pl.program_id(0) == 0) & (pl.program_id(1) == 0))
    def _():
        cl = cl_ref[...]
        cc = cc_ref[...]
        sl_ref[...] = cl * _sigmoid(cl)
        sc_ref[...] = cc * _sigmoid(cc)

    tn = w_ref.shape[1]
    reps = tn // LANES

    def body(kg, carry):
        al, ac = carry
        r0 = pl.multiple_of(kg * SUBLANES, SUBLANES)
        w8 = w_ref[pl.ds(r0, SUBLANES), :]
        s8l = jnp.tile(sl_ref[pl.ds(r0, SUBLANES), :], (1, reps))
        s8c = jnp.tile(sc_ref[pl.ds(r0, SUBLANES), :], (1, reps))
        return al + w8 * s8l, ac + w8 * s8c

    zero = jnp.zeros((SUBLANES, tn), F32)
    al, ac = lax.fori_loop(0, w_ref.shape[0] // SUBLANES, body, (zero, zero), unroll=4)
    b = b_ref[...]
    o_ref[0] = jnp.sum(al, axis=0, keepdims=True) + b
    o_ref[1] = jnp.sum(ac, axis=0, keepdims=True) + b


def _modulation(c, c_ctx, w_mod, b_mod):
    d = D_MODEL
    n = N_MOD * d
    tn = 1024
    cl = jnp.broadcast_to(c.reshape(d, 1), (d, LANES))
    cc = jnp.broadcast_to(c_ctx.reshape(d, 1), (d, LANES))
    return pl.pallas_call(
        _mod_kernel,
        grid=(DEPTH, n // tn),
        in_specs=[
            pl.BlockSpec((d, LANES), lambda l, j: (0, 0)),
            pl.BlockSpec((d, LANES), lambda l, j: (0, 0)),
            pl.BlockSpec((None, d, tn), lambda l, j: (l, 0, j)),
            pl.BlockSpec((None, 1, tn), lambda l, j: (l, 0, j)),
        ],
        out_specs=pl.BlockSpec((None, 2, 1, tn), lambda l, j: (l, 0, 0, j)),
        out_shape=jax.ShapeDtypeStruct((DEPTH, 2, 1, n), F32),
        scratch_shapes=[pltpu.VMEM((d, LANES), F32), pltpu.VMEM((d, LANES), F32)],
        compiler_params=_params("arbitrary", "arbitrary"),
        name="modulation",
    )(cl, cc, w_mod, b_mod.reshape(DEPTH, 1, n))


def _mod_spec(layer, row, k, tn, col_of):
    per = D_MODEL // tn
    return pl.BlockSpec((None, None, 1, tn), lambda *g: (layer, row, 0, k * per + col_of(*g)))


NORM_ROWS = 16


def _norm_mod_kernel(x_ref, g_ref, sh_ref, sc_ref, o_ref):
    g = g_ref[...]
    gain = g + g * sc_ref[...]
    shift = sh_ref[...]
    for r0 in range(0, x_ref.shape[0], NORM_ROWS):
        x = x_ref[r0:r0 + NORM_ROWS, :]
        r = lax.rsqrt(jnp.mean(x * x, axis=-1, keepdims=True) + EPS)
        o_ref[r0:r0 + NORM_ROWS, :] = ((x * r) * gain + shift).astype(o_ref.dtype)


def _norm_mod(x, g, layer, mods, row, k_shift, k_scale):
    m, d = x.shape
    tm = min(m, 512)
    zero = lambda i: 0
    return pl.pallas_call(
        _norm_mod_kernel,
        grid=(m // tm,),
        in_specs=[
            pl.BlockSpec((tm, d), lambda i: (i, 0)),
            pl.BlockSpec((None, 1, d), lambda i: (layer, 0, 0)),
            _mod_spec(layer, row, k_shift, d, zero),
            _mod_spec(layer, row, k_scale, d, zero),
        ],
        out_specs=pl.BlockSpec((tm, d), lambda i: (i, 0)),
        out_shape=jax.ShapeDtypeStruct((m, d), BF16),
        compiler_params=_params("arbitrary"),
        name="norm_mod",
    )(x, g, mods, mods)


def _cast_weights(w_refs, wb_ref):
    @pl.when(pl.program_id(1) == 0)
    def _():
        c0 = 0
        for w_ref in w_refs:
            wn = w_ref.shape[1]
            wb_ref[:, c0:c0 + wn] = w_ref[...].astype(BF16)
            c0 += wn


def _proj_plain_kernel(n_w, blk, h_ref, *refs):
    o_ref, wb_ref = refs[n_w:]
    _cast_weights(refs[:n_w], wb_ref)
    o_ref[...] = _dot(h_ref[...], wb_ref[...]).astype(o_ref.dtype)


def _blocked_dot(lhs_block, rows, blk, epilogue):
    blk = min(rows, blk)
    for r0 in range(0, rows, blk):
        acc = lhs_block(r0, blk)
        for r in range(0, blk, ROW_CHUNK):
            epilogue(acc[r:r + ROW_CHUNK, :], r0 + r)


def _proj_gelu_kernel(n_w, blk, h_ref, *refs):
    o_ref, wb_ref = refs[n_w:]
    _cast_weights(refs[:n_w], wb_ref)

    def epilogue(acc, r):
        o_ref[r:r + ROW_CHUNK, :] = _gelu_tanh(acc).astype(o_ref.dtype)

    _blocked_dot(lambda r0, n: _dot(h_ref[r0:r0 + n, :], wb_ref[...]), h_ref.shape[0], blk, epilogue)


def _swap16(x, even):
    return jnp.where(even, pltpu.roll(x, LANES - 16, axis=1), pltpu.roll(x, 16, axis=1))


def _proj_qkv_kernel(rope, base2, n_w, blk, h_ref, *refs):
    if rope:
        cos_ref, sin_ref, o_ref, wb_ref = refs[n_w:]
    else:
        o_ref, wb_ref = refs[n_w:]
    _cast_weights(refs[:n_w], wb_ref)
    q_scale = HEAD_DIM ** -0.5 * (LOG2E if base2 else 1.0)
    n_rot = (ATT_WIDTH + KV_WIDTH) // LANES
    even = (lax.broadcasted_iota(jnp.int32, (ROW_CHUNK, LANES), 1) & 16) == 0

    def epilogue(acc, r):
        if rope:
            cos = cos_ref[r:r + ROW_CHUNK, :]
            sin = sin_ref[r:r + ROW_CHUNK, :]
        for cidx in range(acc.shape[1] // LANES):
            t = acc[:, cidx * LANES:(cidx + 1) * LANES]
            if cidx < ATT_WIDTH // LANES:
                t = t * q_scale
            if rope and cidx < n_rot:
                t = t * cos + _swap16(t, even) * sin
            o_ref[r:r + ROW_CHUNK, cidx * LANES:(cidx + 1) * LANES] = t.astype(o_ref.dtype)

    _blocked_dot(lambda r0, n: _dot(h_ref[r0:r0 + n, :], wb_ref[...]), h_ref.shape[0], blk, epilogue)


def _proj(kind, h, w, w_index, col0, n, tn, wtn, out_dtype, tm=None, extra=(), blk=PROJ_BLOCK):
    m, k = h.shape
    tm = tm or min(m, 1024)
    assert col0 % wtn == 0 and tn % wtn == 0 and n % tn == 0 and m % tm == 0
    n_w = tn // wtn
    kernels = {
        "plain": _proj_plain_kernel,
        "gelu": _proj_gelu_kernel,
        "qkv": functools.partial(_proj_qkv_kernel, False, False),
        "qkv_rope": functools.partial(_proj_qkv_kernel, True, False),
        "qkv2": functools.partial(_proj_qkv_kernel, False, True),
        "qkv2_rope": functools.partial(_proj_qkv_kernel, True, True),
    }
    in_specs = [pl.BlockSpec((tm, k), lambda j, i: (i, 0))]
    for p in range(n_w):
        in_specs.append(pl.BlockSpec((None, k, wtn),
                                     lambda j, i, p=p: (w_index, 0, col0 // wtn + j * n_w + p)))
    in_specs += [pl.BlockSpec((tm, LANES), lambda j, i: (i, 0)) for _ in extra]
    return pl.pallas_call(
        functools.partial(kernels[kind], n_w, blk),
        grid=(n // tn, m // tm),
        in_specs=in_specs,
        out_specs=pl.BlockSpec((tm, tn), lambda j, i: (i, j)),
        out_shape=jax.ShapeDtypeStruct((m, n), out_dtype),
        scratch_shapes=[pltpu.VMEM((k, tn), BF16)],
        compiler_params=_params("arbitrary", "arbitrary"),
        name="proj_" + kind,
    )(h, *([w] * n_w), *extra)


def _proj_res_kernel(n_parts, blk, *refs):
    a_refs = refs[:n_parts]
    w_ref, x_ref, gt_ref, o_ref, wb_ref = refs[n_parts:]
    _cast_weights([w_ref], wb_ref)

    def lhs_block(r0, n):
        acc = None
        k0 = 0
        for a_ref in a_refs:
            kp = a_ref.shape[1]
            part = _dot(a_ref[r0:r0 + n, :], wb_ref[k0:k0 + kp, :])
            acc = part if acc is None else acc + part
            k0 += kp
        return acc

    def epilogue(acc, r):
        o_ref[r:r + ROW_CHUNK, :] = x_ref[r:r + ROW_CHUNK, :] + gt_ref[...] * acc

    _blocked_dot(lhs_block, x_ref.shape[0], blk, epilogue)


def _proj_residual(parts, w, w_index, x, layer, mods, row, k_gate, tm, tn, blk=PROJ_BLOCK):
    m, n = x.shape
    k = sum(p.shape[1] for p in parts)
    tm = min(m, tm)
    in_specs = [pl.BlockSpec((tm, p.shape[1]), lambda j, i: (i, 0)) for p in parts]
    in_specs += [
        pl.BlockSpec((None, k, tn), lambda j, i: (w_index, 0, j)),
        pl.BlockSpec((tm, tn), lambda j, i: (i, j)),
        _mod_spec(layer, row, k_gate, tn, lambda j, i: j),
    ]
    return pl.pallas_call(
        functools.partial(_proj_res_kernel, len(parts), blk),
        grid=(n // tn, m // tm),
        in_specs=in_specs,
        out_specs=pl.BlockSpec((tm, tn), lambda j, i: (i, j)),
        out_shape=jax.ShapeDtypeStruct((m, n), F32),
        scratch_shapes=[pltpu.VMEM((k, tn), BF16)],
        compiler_params=_params("arbitrary", "arbitrary"),
        name="proj_residual",
    )(*parts, w, x, mods)


def _conv_proj_kernel(n_w, kw, left, gated, n_row_tiles, plan, h_ref, hp_ref, hn_ref, *refs):
    lag, axis, n_split, slabs = plan
    w_refs = refs[:n_w]
    cw_refs = refs[n_w:2 * n_w]
    cb_refs = refs[2 * n_w:3 * n_w]
    o_ref, wb_ref, ext_ref = refs[3 * n_w:3 * n_w + 3]
    z_refs = refs[3 * n_w + 3:]
    tm = h_ref.shape[0]
    tn = w_refs[0].shape[1]
    ext_rows = ext_ref.shape[0]
    if lag:
        s = pl.program_id(0)
        n_tiles = pl.num_programs(0) - 1
        i = jnp.minimum(s, n_tiles - 1) % n_row_tiles
    else:
        i = pl.program_id(1)

    @pl.when(i == 0)
    def _():
        for p, w_ref in enumerate(w_refs):
            wb_ref[:, p * tn:(p + 1) * tn] = w_ref[...].astype(BF16)

    if lag:
        @pl.when(s == 0)
        def _():
            z_refs[1][...] = jnp.zeros_like(z_refs[1])

    ext_ref[0:HALO, :] = jnp.where(i > 0, hp_ref[...], jnp.zeros_like(hp_ref))
    ext_ref[HALO:HALO + tm, :] = h_ref[...]
    ext_ref[HALO + tm:, :] = jnp.where(i < n_row_tiles - 1, hn_ref[...], jnp.zeros_like(hn_ref))

    out_slabs = tn // LANES
    units = [(so, r) for so in range(out_slabs) for r in range(0, tm, ROW_CHUNK)]

    def epilogue_unit(window, so, r):
        c0 = so * LANES
        outs = []
        for p in range(n_w):
            u = cb_refs[p][:, c0:c0 + LANES]
            for tap in range(kw):
                u = u + window(p * out_slabs + so, HALO - left + tap + r) * cw_refs[p][tap:tap + 1, c0:c0 + LANES]
            outs.append(u)
        if gated:
            g, v = outs
            res = g * _sigmoid(g) * v
        else:
            res = outs[0]
        o_ref[r:r + ROW_CHUNK, c0:c0 + LANES] = res.astype(o_ref.dtype)

    def project_piece(z_new, piece):
        if axis == 'm':
            rows = slice(piece * (ext_rows // n_split), (piece + 1) * (ext_rows // n_split))
            zc = _dot(ext_ref[rows, :], wb_ref[...])
            for t in range(n_w * out_slabs):
                z_new[t, rows, :] = zc[:, t * LANES:(t + 1) * LANES]
        else:
            width = n_w * tn // n_split
            zc = _dot(ext_ref[...], wb_ref[:, piece * width:(piece + 1) * width])
            for t in range(width // LANES):
                z_new[piece * (width // LANES) + t] = zc[:, t * LANES:(t + 1) * LANES]

    def slab_window(z):
        return lambda slab, r0: z[slab, r0:r0 + ROW_CHUNK, :]

    if lag:
        per_piece = -(-len(units) // n_split)

        def step(z_new, z_old):
            for piece in range(n_split):
                project_piece(z_new, piece)
                for so, r in units[piece * per_piece:(piece + 1) * per_piece]:
                    epilogue_unit(slab_window(z_old), so, r)

        @pl.when(s % 2 == 0)
        def _():
            step(z_refs[0], z_refs[1])

        @pl.when(s % 2 == 1)
        def _():
            step(z_refs[1], z_refs[0])
    elif slabs:
        project_piece(z_refs[0], 0)
        for so, r in units:
            epilogue_unit(slab_window(z_refs[0]), so, r)
    else:
        z = _dot(ext_ref[...], wb_ref[...])
        for so, r in units:
            epilogue_unit(lambda slab, r0: z[r0:r0 + ROW_CHUNK, slab * LANES:(slab + 1) * LANES], so, r)


def _conv_proj(h, w, w_index, col_blocks, conv_blocks, n_out, cw, cb, kw, left, gated, tm, tn, out_dtype,
               plan=(True, 'm', 6, True)):
    m, k = h.shape
    tm = min(m, tm)
    n_w = len(col_blocks)
    hb = m // HALO
    tb = tm // HALO
    ni = m // tm
    nj = n_out // tn
    lag, axis, n_split, slabs = plan
    if lag:
        n_tiles = nj * ni
        grid = (n_tiles + 1,)
        cur = lambda s: jnp.minimum(s, n_tiles - 1)
        prv = lambda s: jnp.maximum(s - 1, 0)
        row_of = lambda s: cur(s) % ni
        col_of = lambda s: cur(s) // ni
        ecol_of = lambda s: prv(s) // ni
        out_of = lambda s: (prv(s) % ni, prv(s) // ni)
        sem = ("arbitrary",)
    else:
        grid = (nj, ni)
        row_of = lambda j, i: i
        col_of = ecol_of = lambda j, i: j
        out_of = lambda j, i: (i, j)
        sem = ("arbitrary", "arbitrary")
    in_specs = [
        pl.BlockSpec((tm, k), lambda *g: (row_of(*g), 0)),
        pl.BlockSpec((HALO, k), lambda *g: (jnp.maximum(row_of(*g) * tb - 1, 0), 0)),
        pl.BlockSpec((HALO, k), lambda *g: (jnp.minimum((row_of(*g) + 1) * tb, hb - 1), 0)),
    ]
    for c0 in col_blocks:
        in_specs.append(pl.BlockSpec((None, k, tn), lambda *g, c0=c0: (w_index, 0, c0 + col_of(*g))))
    for c0 in conv_blocks:
        in_specs.append(pl.BlockSpec((None, kw, tn), lambda *g, c0=c0: (w_index, 0, c0 + ecol_of(*g))))
    for c0 in conv_blocks:
        in_specs.append(pl.BlockSpec((None, 1, tn), lambda *g, c0=c0: (w_index, 0, c0 + ecol_of(*g))))
    args = [h, h, h] + [w] * n_w + [cw] * n_w + [cb] * n_w
    ext_rows = tm + 2 * HALO
    if axis == 'm':
        packed = ext_rows // HALO
        n_split = max(d for d in range(1, n_split + 1) if packed % d == 0)
    z_shape = (n_w * tn // LANES, ext_rows, LANES)
    n_z = 2 if lag else (1 if slabs else 0)
    return pl.pallas_call(
        functools.partial(_conv_proj_kernel, n_w, kw, left, gated, ni, (lag, axis, n_split, slabs)),
        grid=grid,
        in_specs=in_specs,
        out_specs=pl.BlockSpec((tm, tn), out_of),
        out_shape=jax.ShapeDtypeStruct((m, n_out), out_dtype),
        scratch_shapes=[pltpu.VMEM((k, n_w * tn), BF16), pltpu.VMEM((ext_rows, k), BF16)]
                       + [pltpu.VMEM(z_shape, F32)] * n_z,
        compiler_params=_params(*sem),
        name="conv_proj",
    )(*args)


def _pair_operand(band, kv_head):
    b = band.astype(F32)
    rolled = pltpu.roll(b, HEAD_DIM, axis=1)
    low = lax.broadcasted_iota(jnp.int32, b.shape, 1) < HEAD_DIM
    zero = jnp.zeros_like(b)
    if kv_head == 0:
        top = jnp.where(low, b, zero)
        bot = jnp.where(low, zero, rolled)
    else:
        top = jnp.where(low, rolled, zero)
        bot = jnp.where(low, zero, b)
    return jnp.concatenate([top, bot], axis=0).astype(BF16)


def _attention_core(q_ref, k_band, v_band, biases, sink_ref, fa_index, base2, o_ref):
    tq = q_ref.shape[0]
    nk = k_band.shape[0]
    low = lax.broadcasted_iota(jnp.int32, (tq, LANES), 1) < HEAD_DIM
    heads_per_kv = N_HEADS // N_KV_HEADS
    power = jnp.exp2 if base2 else jnp.exp
    for kv_head in range(N_KV_HEADS):
        k2 = _pair_operand(k_band, kv_head)
        v2 = _pair_operand(v_band, kv_head)
        for pair in range(heads_per_kv // 2):
            c0 = (kv_head * (heads_per_kv // 2) + pair) * LANES
            head = c0 // HEAD_DIM
            s = _dot_nt(q_ref[:, c0:c0 + LANES], k2)
            ps, inv = [], []
            for half in range(2):
                sink = sink_ref[fa_index, head + half]
                if base2:
                    sink = sink * LOG2E
                cols = []
                for j in range(nk // LANES):
                    blk = s[:, half * nk + j * LANES:half * nk + (j + 1) * LANES]
                    cols.append(blk + biases[j] if j in biases else blk)
                top = cols[0]
                for blk in cols[1:]:
                    top = jnp.maximum(top, blk)
                mx = jnp.maximum(jnp.max(top, axis=1, keepdims=True), sink)
                tot = None
                for blk in cols:
                    p = power(blk - mx)
                    tot = p if tot is None else tot + p
                    ps.append(p.astype(BF16))
                den = jnp.sum(tot, axis=1, keepdims=True) + power(sink - mx)
                inv.append(1.0 / den)
            o = _dot(jnp.concatenate(ps, axis=1), v2)
            o_ref[:, c0:c0 + LANES] = (o * jnp.where(low, inv[0], inv[1])).astype(o_ref.dtype)


def _attn_lat_kernel(fa_index, base2, sink_ref, q_ref, kp_ref, kc_ref, kn_ref, vp_ref, vc_ref, vn_ref,
                     kx_ref, vx_ref, o_ref):
    n = pl.program_id(0)
    nb = pl.num_programs(0)
    k_band = jnp.concatenate([kp_ref[...], kc_ref[...], kn_ref[...], kx_ref[...]], axis=0)
    v_band = jnp.concatenate([vp_ref[...], vc_ref[...], vn_ref[...], vx_ref[...]], axis=0)
    qi = lax.broadcasted_iota(jnp.int32, (BLOCK, BLOCK), 0)
    kj = lax.broadcasted_iota(jnp.int32, (BLOCK, BLOCK), 1)
    bias_prev = jnp.where((kj >= qi) & (n > 0), 0.0, NEG_INF).astype(F32)
    bias_next = jnp.where((kj <= qi) & (n < nb - 1), 0.0, NEG_INF).astype(F32)
    _attention_core(q_ref, k_band, v_band, {0: bias_prev, 2: bias_next}, sink_ref, fa_index, base2, o_ref)


def _attn_ctx_kernel(fa_index, base2, sink_ref, q_ref, kx_ref, vx_ref, o_ref):
    _attention_core(q_ref, kx_ref[...], vx_ref[...], {}, sink_ref, fa_index, base2, o_ref)


def _attention_lat(qkv, qkv_ctx, sink, fa_index, base2):
    s = qkv.shape[0]
    n_ctx = qkv_ctx.shape[0]
    nb = s // BLOCK
    kcol = ATT_WIDTH // LANES
    vcol = kcol + 1
    prev = lambda n: jnp.maximum(n - 1, 0)
    nxt = lambda n: jnp.minimum(n + 1, nb - 1)
    in_specs = [
        pl.BlockSpec(memory_space=pltpu.SMEM),
        pl.BlockSpec((BLOCK, ATT_WIDTH), lambda n: (n, 0)),
        pl.BlockSpec((BLOCK, LANES), lambda n: (prev(n), kcol)),
        pl.BlockSpec((BLOCK, LANES), lambda n: (n, kcol)),
        pl.BlockSpec((BLOCK, LANES), lambda n: (nxt(n), kcol)),
        pl.BlockSpec((BLOCK, LANES), lambda n: (prev(n), vcol)),
        pl.BlockSpec((BLOCK, LANES), lambda n: (n, vcol)),
        pl.BlockSpec((BLOCK, LANES), lambda n: (nxt(n), vcol)),
        pl.BlockSpec((n_ctx, LANES), lambda n: (0, kcol)),
        pl.BlockSpec((n_ctx, LANES), lambda n: (0, vcol)),
    ]
    return pl.pallas_call(
        functools.partial(_attn_lat_kernel, fa_index, base2),
        grid=(nb,),
        in_specs=in_specs,
        out_specs=pl.BlockSpec((BLOCK, ATT_WIDTH), lambda n: (n, 0)),
        out_shape=jax.ShapeDtypeStruct((s, ATT_WIDTH), BF16),
        compiler_params=_params("arbitrary"),
        name="attention_latent",
    )(sink, qkv, qkv, qkv, qkv, qkv, qkv, qkv, qkv_ctx, qkv_ctx)


def _attention_ctx(qkv_ctx, sink, fa_index, base2):
    n_ctx = qkv_ctx.shape[0]
    kcol = ATT_WIDTH // LANES
    return pl.pallas_call(
        functools.partial(_attn_ctx_kernel, fa_index, base2),
        grid=(1,),
        in_specs=[
            pl.BlockSpec(memory_space=pltpu.SMEM),
            pl.BlockSpec((n_ctx, ATT_WIDTH), lambda n: (0, 0)),
            pl.BlockSpec((n_ctx, LANES), lambda n: (0, kcol)),
            pl.BlockSpec((n_ctx, LANES), lambda n: (0, kcol + 1)),
        ],
        out_specs=pl.BlockSpec((n_ctx, ATT_WIDTH), lambda n: (0, 0)),
        out_shape=jax.ShapeDtypeStruct((n_ctx, ATT_WIDTH), BF16),
        compiler_params=_params("arbitrary"),
        name="attention_context",
    )(sink, qkv_ctx, qkv_ctx, qkv_ctx)


FFT_N1 = 64
FFT_N2 = 128


def _dft_cos_sin(n):
    idx = np.arange(n)
    ang = 2.0 * np.pi * ((idx[:, None] * idx[None, :]) % n) / n
    return np.cos(ang), np.sin(ang)


def _fourier_constants():
    c1, s1 = _dft_cos_sin(FFT_N1)
    stage1 = np.concatenate([c1, -s1], axis=0)
    c2, s2 = _dft_cos_sin(FFT_N2)
    stage2 = np.block([[c2, s2], [-s2, c2]])
    cc, sc = _dft_cos_sin(FG_W)
    chan = np.concatenate([cc, sc], axis=0)
    return (jnp.asarray(stage1, F32), jnp.asarray(stage2, F32), jnp.asarray(chan, F32))


def _twiddle_tables():
    k1 = jnp.arange(FFT_N1, dtype=jnp.int32)[:, None]
    n2 = jnp.arange(FFT_N2, dtype=jnp.int32)[None, :]
    ang = ((k1 * n2) % (FFT_N1 * FFT_N2)).astype(F32) * (2.0 * math.pi / (FFT_N1 * FFT_N2))
    wr = jnp.repeat(jnp.cos(ang), LANES, axis=1)
    wi = jnp.repeat(-jnp.sin(ang), LANES, axis=1)
    return wr, wi


def _fourier_stage1_kernel(x_ref, m_ref, wr_ref, wi_ref, tr_ref, ti_ref):
    y = _dot(m_ref[...].astype(BF16), x_ref[...].astype(BF16))
    reps = F_WIDTH // LANES
    for b in range(x_ref.shape[1] // F_WIDTH):
        cols = slice(b * F_WIDTH, (b + 1) * F_WIDTH)
        yr = y[:FFT_N1, cols]
        yi = y[FFT_N1:, cols]
        wr = jnp.tile(wr_ref[:, b * LANES:(b + 1) * LANES], (1, reps))
        wi = jnp.tile(wi_ref[:, b * LANES:(b + 1) * LANES], (1, reps))
        tr_ref[:, cols] = (yr * wr - yi * wi).astype(tr_ref.dtype)
        ti_ref[:, cols] = (yr * wi + yi * wr).astype(ti_ref.dtype)


def _channel_stage(p, chan_ref, scale, o_ref):
    r = p.shape[0] // 2
    pr = p[:r].astype(BF16)
    pi = p[r:].astype(BF16)
    chan_c = chan_ref[:FG_W, :].astype(BF16)
    chan_s = chan_ref[FG_W:, :].astype(BF16)
    for g in range(N_FG):
        cols = slice(g * FG_W, (g + 1) * FG_W)
        y = _dot(pr[:, cols], chan_c) + _dot(pi[:, cols], chan_s)
        o_ref[:, cols] = (y * scale).astype(o_ref.dtype)


def _fourier_stage2_kernel(scale, tr_ref, ti_ref, m_ref, chan_ref, o_ref):
    m = m_ref[...].astype(BF16)
    for kk in range(tr_ref.shape[0]):
        t = jnp.concatenate([tr_ref[kk], ti_ref[kk]], axis=0)
        _channel_stage(_dot(m, t), chan_ref, scale, o_ref.at[:, kk * F_WIDTH:(kk + 1) * F_WIDTH])


def _fourier_ctx_kernel(scale, x_ref, m_ref, chan_ref, o_ref):
    _channel_stage(_dot(m_ref[...].astype(BF16), x_ref[...].astype(BF16)), chan_ref, scale, o_ref)


def _fourier_lat(f, consts, twiddles):
    n = f.shape[0]
    assert n == FFT_N1 * FFT_N2
    stage1, stage2, chan = consts
    wr, wi = twiddles
    n2_blk = 8
    cols = n2_blk * F_WIDTH
    full = lambda a: pl.BlockSpec(a.shape, lambda j: (0,) * a.ndim)
    tr, ti = pl.pallas_call(
        _fourier_stage1_kernel,
        grid=(FFT_N2 // n2_blk,),
        in_specs=[
            pl.BlockSpec((FFT_N1, cols), lambda j: (0, j)),
            full(stage1),
            pl.BlockSpec((FFT_N1, n2_blk * LANES), lambda j: (0, j)),
            pl.BlockSpec((FFT_N1, n2_blk * LANES), lambda j: (0, j)),
        ],
        out_specs=[pl.BlockSpec((FFT_N1, cols), lambda j: (0, j))] * 2,
        out_shape=[jax.ShapeDtypeStruct((FFT_N1, FFT_N2 * F_WIDTH), BF16)] * 2,
        compiler_params=_params("arbitrary"),
        name="fourier_stage1",
    )(f.reshape(FFT_N1, FFT_N2 * F_WIDTH), stage1, wr, wi)
    tr = tr.reshape(FFT_N1, FFT_N2, F_WIDTH)
    ti = ti.reshape(FFT_N1, FFT_N2, F_WIDTH)
    scale = 1.0 / math.sqrt(n * FG_W)
    k1_blk = 4
    out = pl.pallas_call(
        functools.partial(_fourier_stage2_kernel, scale),
        grid=(FFT_N1 // k1_blk,),
        in_specs=[
            pl.BlockSpec((k1_blk, FFT_N2, F_WIDTH), lambda k1: (k1, 0, 0)),
            pl.BlockSpec((k1_blk, FFT_N2, F_WIDTH), lambda k1: (k1, 0, 0)),
            full(stage2),
            full(chan),
        ],
        out_specs=pl.BlockSpec((FFT_N2, k1_blk * F_WIDTH), lambda k1: (0, k1)),
        out_shape=jax.ShapeDtypeStruct((FFT_N2, FFT_N1 * F_WIDTH), BF16),
        compiler_params=_params("arbitrary"),
        name="fourier_stage2",
    )(tr, ti, stage2, chan)
    return out.reshape(n, F_WIDTH)


def _fourier_ctx(f, chan):
    n = f.shape[0]
    c, s = _dft_cos_sin(n)
    m = jnp.asarray(np.concatenate([c, -s], axis=0), F32)
    full = lambda a: pl.BlockSpec(a.shape, lambda j: (0,) * a.ndim)
    return pl.pallas_call(
        functools.partial(_fourier_ctx_kernel, 1.0 / math.sqrt(n * FG_W)),
        grid=(1,),
        in_specs=[full(f), full(m), full(chan)],
        out_specs=pl.BlockSpec((n, F_WIDTH), lambda j: (0, 0)),
        out_shape=jax.ShapeDtypeStruct((n, F_WIDTH), BF16),
        compiler_params=_params("arbitrary"),
        name="fourier_context",
    )(f, m, chan)


def _scan8(a, b, row, reverse):
    for d in (1, 2, 4):
        shift = (SUBLANES - d) if reverse else d
        keep = (row < SUBLANES - d) if reverse else (row >= d)
        b = jnp.where(keep, b + a * pltpu.roll(b, shift, axis=0), b)
        a = jnp.where(keep, a * pltpu.roll(a, shift, axis=0), a)
    return a, b


def _rows(x, n):
    return jnp.broadcast_to(x, (n, x.shape[1]))


def _rglru_kernel(reverse, combine, xs_ref, wa_ref, wi_ref, ba_ref, bi_ref, lam_ref, h0_ref, *refs):
    if combine:
        hb_ref, gg_ref, o_ref, last_ref, w_scr, a_scr, b_scr, c_scr, carry_scr = refs
    else:
        o_ref, last_ref, w_scr, a_scr, b_scr, c_scr, carry_scr = refs
    t = pl.program_id(1)
    tt, c = xs_ref.shape
    n_slab = c // LANES
    groups = tt // SUBLANES
    blocks = groups // SUBLANES
    row = lax.broadcasted_iota(jnp.int32, (SUBLANES, LANES), 0)
    edge = 0 if reverse else SUBLANES - 1

    @pl.when(t == 0)
    def _():
        carry_scr[...] = jnp.broadcast_to(h0_ref[...], carry_scr.shape)
        w_scr[:, :c] = (0.5 * wa_ref[...]).astype(BF16)
        w_scr[:, c:] = (0.5 * wi_ref[...]).astype(BF16)

    pre = _dot(xs_ref[...].astype(BF16), w_scr[...])
    neg_lam = -lam_ref[...]
    softplus = jnp.maximum(neg_lam, 0.0) + jnp.log1p(jnp.exp(-jnp.abs(neg_lam)))
    k_all = softplus * (-0.5 * LRU_C * math.log2(math.e))
    ba_all = 0.5 * ba_ref[...]
    bi_all = 0.5 * bi_ref[...]

    for s in range(n_slab):
        lanes = slice(s * LANES, (s + 1) * LANES)
        k = _rows(k_all[:, lanes], SUBLANES)
        ba = _rows(ba_all[:, lanes], SUBLANES)
        bi = _rows(bi_all[:, lanes], SUBLANES)
        for g in range(groups):
            r0 = g * SUBLANES
            tr = jnp.tanh(pre[r0:r0 + SUBLANES, s * LANES:(s + 1) * LANES] + ba)
            ti = jnp.tanh(pre[r0:r0 + SUBLANES, c + s * LANES:c + (s + 1) * LANES] + bi)
            a = jnp.exp2((1.0 + tr) * k)
            y = 1.0 - a * a
            root = jnp.where(y > 0.0, y * lax.rsqrt(y), 0.0)
            b = root * ((1.0 + ti) * (0.5 * xs_ref[r0:r0 + SUBLANES, lanes]))
            a_cum, b_loc = _scan8(a, b, row, reverse)
            a_scr[s, r0:r0 + SUBLANES, :] = a_cum
            b_scr[s, r0:r0 + SUBLANES, :] = b_loc

    for s in range(n_slab):
        a2 = a_scr[s, pl.ds(edge, groups, stride=SUBLANES), :]
        b2 = b_scr[s, pl.ds(edge, groups, stride=SUBLANES), :]
        carry = carry_scr[:, s * LANES:(s + 1) * LANES]
        enter_row = groups if reverse else SUBLANES - 1
        base = 0 if reverse else SUBLANES
        c_scr[s, enter_row:enter_row + 1, :] = carry[0:1, :]
        for j in (range(blocks - 1, -1, -1) if reverse else range(blocks)):
            r0 = j * SUBLANES
            a_cum, b_loc = _scan8(a2[r0:r0 + SUBLANES, :], b2[r0:r0 + SUBLANES, :], row, reverse)
            st = b_loc + a_cum * carry
            c_scr[s, base + r0:base + r0 + SUBLANES, :] = st
            carry = _rows(st[edge:edge + 1, :], SUBLANES)
        carry_scr[:, s * LANES:(s + 1) * LANES] = carry
        last_ref[:, s * LANES:(s + 1) * LANES] = carry[0:1, :]

    pair = 2 * SUBLANES
    for s in range(n_slab):
        lanes = slice(s * LANES, (s + 1) * LANES)
        for g in range(0, groups, 2):
            r0 = g * SUBLANES
            src = g + 1 if reverse else g + SUBLANES - 1
            enter = jnp.concatenate([_rows(c_scr[s, src:src + 1, :], SUBLANES),
                                     _rows(c_scr[s, src + 1:src + 2, :], SUBLANES)], axis=0)
            h = b_scr[s, r0:r0 + pair, :] + a_scr[s, r0:r0 + pair, :] * enter
            if combine:
                h = (h + hb_ref[r0:r0 + pair, lanes]) * gg_ref[r0:r0 + pair, lanes].astype(F32)
            o_ref[r0:r0 + pair, lanes] = h.astype(o_ref.dtype)


def _rglru_scan(xs, w_a, w_i, b_a, b_i, lam, rg_index, direction, h0, h_other=None, gelu_gate=None):
    m = xs.shape[0]
    tt = min(m, 1024)
    nt = m // tt
    c = RNN_BLOCK
    reverse = direction == 1
    combine = h_other is not None
    tix = (lambda t: nt - 1 - t) if reverse else (lambda t: t)
    wspec = pl.BlockSpec((None, None, None, c, c), lambda cb, t: (rg_index, direction, cb, 0, 0))
    vspec = pl.BlockSpec((None, None, 1, c), lambda cb, t: (rg_index, direction, 0, cb))
    tile = pl.BlockSpec((tt, c), lambda cb, t: (tix(t), cb))
    in_specs = [tile, wspec, wspec, vspec, vspec, vspec, pl.BlockSpec((1, c), lambda cb, t: (0, cb))]
    args = [xs, w_a, w_i, b_a, b_i, lam, h0]
    if combine:
        in_specs += [tile, tile]
        args += [h_other, gelu_gate]
    return pl.pallas_call(
        functools.partial(_rglru_kernel, reverse, combine),
        grid=(D_RNN // c, nt),
        in_specs=in_specs,
        out_specs=[tile, pl.BlockSpec((1, c), lambda cb, t: (0, cb))],
        out_shape=[jax.ShapeDtypeStruct((m, D_RNN), BF16 if combine else F32),
                   jax.ShapeDtypeStruct((1, D_RNN), F32)],
        scratch_shapes=[pltpu.VMEM((c, 2 * c), BF16),
                        pltpu.VMEM((c // LANES, tt, LANES), F32),
                        pltpu.VMEM((c // LANES, tt, LANES), F32),
                        pltpu.VMEM((c // LANES, tt // SUBLANES + 2 * SUBLANES, LANES), F32),
                        pltpu.VMEM((SUBLANES, c), F32)],
        compiler_params=_params("arbitrary", "arbitrary"),
        name="rglru_scan",
    )(*args)


def _final_norm_kernel(x_ref, g_ref, o_ref):
    g = g_ref[...]
    for r0 in range(0, x_ref.shape[0], NORM_ROWS):
        x = x_ref[r0:r0 + NORM_ROWS, :]
        r = lax.rsqrt(jnp.mean(x * x, axis=-1, keepdims=True) + EPS)
        o_ref[r0:r0 + NORM_ROWS, :] = (x * r) * g


def _final_norm(x, g):
    m, d = x.shape
    tm = 512
    return pl.pallas_call(
        _final_norm_kernel,
        grid=(m // tm,),
        in_specs=[pl.BlockSpec((tm, d), lambda i: (i, 0)), pl.BlockSpec((1, d), lambda i: (0, 0))],
        out_specs=pl.BlockSpec((tm, d), lambda i: (i, 0)),
        out_shape=jax.ShapeDtypeStruct((m, d), F32),
        compiler_params=_params("arbitrary"),
        name="final_norm",
    )(x, g.reshape(1, d))


def _rope_tables(n):
    f = HEAD_DIM // 4
    inv = ROPE_BASE ** (-jnp.arange(f, dtype=F32) / f)
    pos = jnp.arange(n, dtype=jnp.int32)
    ang_r = (pos // GRID_W).astype(F32)[:, None] * inv[None, :]
    ang_c = (pos % GRID_W).astype(F32)[:, None] * inv[None, :]
    cr, sr, cc, sc = jnp.cos(ang_r), jnp.sin(ang_r), jnp.cos(ang_c), jnp.sin(ang_c)
    cos = jnp.concatenate([cr, cr, cc, cc], axis=1)
    sin = jnp.concatenate([-sr, sr, -sc, sc], axis=1)
    reps = LANES // HEAD_DIM
    return jnp.tile(cos, (1, reps)), jnp.tile(sin, (1, reps))


FFN_UP_PLAN = {0: (False, 'm', 1, False), 1: (False, 'm', 1, True), 2: (True, 'n', 4, True), 3: (True, 'm', 2, True)}
FFN_DOWN_PLAN = {0: (512, 512, 256), 1: (512, 512, 512), 2: (1024, 256, 256), 3: (1024, 256, 1024)}
RG_CONV_PLAN = {1: (False, 'm', 1, True), 3: (True, 'n', 4, True)}
GELU_BLOCK = {1: 1024, 3: 512}
OUT_PROJ_BLOCK = {0: 1024, 1: 1024, 2: 256, 3: 512}
QKV_PLAN = {0: (512, 512), 2: (1024, 1024)}
ATT_BASE2 = {0: False, 2: True}
CTX_CONV_PLAN = (False, 'm', 1, True)


def _fourier_attn_layer(layer, i, x_lat, x_ctx, h_lat, h_ctx, mods, fa_w_in, fa_w_out, attn_sink,
                        tables, ctx_out):
    rope, consts, twiddles = tables
    qkv_w = ATT_WIDTH + 2 * KV_WIDTH
    base2 = ATT_BASE2[layer]
    qkv_kind = "qkv2" if base2 else "qkv"
    tm_q, blk_q = QKV_PLAN[layer]
    f_lat = _proj("plain", h_lat, fa_w_in, i, 0, F_WIDTH, F_WIDTH, F_WIDTH, BF16)
    qkv_lat = _proj(qkv_kind + "_rope", h_lat, fa_w_in, i, F_WIDTH, qkv_w, qkv_w, 256, BF16, tm=tm_q,
                    extra=rope, blk=blk_q)
    qkv_ctx = _proj(qkv_kind, h_ctx, fa_w_in, i, F_WIDTH, qkv_w, qkv_w, 256, BF16)
    fo_lat = _fourier_lat(f_lat, consts, twiddles)
    ao_lat = _attention_lat(qkv_lat, qkv_ctx, attn_sink, i, base2)
    blk = OUT_PROJ_BLOCK[layer]
    x_lat = _proj_residual([fo_lat, ao_lat], fa_w_out, i, x_lat, layer, mods, 0, 2, 1024, 1024, blk)
    if ctx_out:
        f_ctx = _proj("plain", h_ctx, fa_w_in, i, 0, F_WIDTH, F_WIDTH, F_WIDTH, BF16)
        fo_ctx = _fourier_ctx(f_ctx, consts[2])
        ao_ctx = _attention_ctx(qkv_ctx, attn_sink, i, base2)
        x_ctx = _proj_residual([fo_ctx, ao_ctx], fa_w_out, i, x_ctx, layer, mods, 1, 2, 1024, 1024)
    return x_lat, x_ctx


def _rglru_layer(layer, i, x_lat, x_ctx, h_lat, h_ctx, mods, rg_w_in, rg_conv_w, rg_conv_b, w_a, b_a,
                 w_i, b_i, lam, rg_w_out, ctx_out):
    tn = 1024
    xcol = D_RNN // tn
    zero_state = jnp.zeros((1, D_RNN), F32)
    gate_lat = _proj("gelu", h_lat, rg_w_in, i, 0, D_RNN, tn, tn, BF16, blk=GELU_BLOCK[layer])
    xs_lat = _conv_proj(h_lat, rg_w_in, i, [xcol], [0], D_RNN, rg_conv_w, rg_conv_b, CONV_W, CONV_LEFT,
                        False, 1024, tn, F32, RG_CONV_PLAN[layer])
    xs_ctx = _conv_proj(h_ctx, rg_w_in, i, [xcol], [0], D_RNN, rg_conv_w, rg_conv_b, CONV_W, CONV_LEFT,
                        False, 1024, tn, F32, CTX_CONV_PLAN)
    scan = functools.partial(_rglru_scan, w_a=w_a, w_i=w_i, b_a=b_a, b_i=b_i, lam=lam, rg_index=i)
    hb_ctx, s_bwd = scan(xs_ctx, direction=1, h0=zero_state)
    if ctx_out:
        gate_ctx = _proj("gelu", h_ctx, rg_w_in, i, 0, D_RNN, tn, tn, BF16)
        y_ctx, s_fwd = scan(xs_ctx, direction=0, h0=zero_state, h_other=hb_ctx, gelu_gate=gate_ctx)
    else:
        _, s_fwd = scan(xs_ctx, direction=0, h0=zero_state)
    hb_lat, _ = scan(xs_lat, direction=1, h0=s_bwd)
    y_lat, _ = scan(xs_lat, direction=0, h0=s_fwd, h_other=hb_lat, gelu_gate=gate_lat)
    x_lat = _proj_residual([y_lat], rg_w_out, i, x_lat, layer, mods, 0, 2, 1024, 1024, OUT_PROJ_BLOCK[layer])
    if ctx_out:
        x_ctx = _proj_residual([y_ctx], rg_w_out, i, x_ctx, layer, mods, 1, 2, 1024, 1024)
    return x_lat, x_ctx


def _conv_ffn(layer, x, g_ffn, mods, row, w_up, conv_w, conv_b, w_down):
    h = _norm_mod(x, g_ffn, layer, mods, row, 3, 4)
    tf = 512
    blocks = [0, D_FF // tf]
    latent = row == 0
    act = _conv_proj(h, w_up, layer, blocks, blocks, D_FF, conv_w, conv_b, FFN_CONV_W, FFN_CONV_LEFT, True,
                     1024, tf, BF16, FFN_UP_PLAN[layer] if latent else CTX_CONV_PLAN)
    tm, tn, blk = FFN_DOWN_PLAN[layer] if latent else (512, 512, 256)
    return _proj_residual([act], w_down, layer, x, layer, mods, row, 5, tm, tn, blk)


def kernel(x, c, ctx, c_ctx, w_mod, b_mod, g_mix, g_ffn, fa_w_in, fa_w_out, attn_sink, rg_w_in, rg_conv_w,
           rg_conv_b, rg_w_a, rg_b_a, rg_w_i, rg_b_i, rg_lambda, rg_w_out, ffn_w_up, ffn_conv_w, ffn_conv_b,
           ffn_w_down, g_final):
    assert x.shape[0] == 1 and ctx.shape[0] == 1
    n = x.shape[1]
    x_lat = x[0]
    x_ctx = ctx[0]
    mods = _modulation(c, c_ctx, w_mod, b_mod)
    tables = (_rope_tables(n), _fourier_constants(), _twiddle_tables())
    g_mix3 = g_mix.reshape(DEPTH, 1, D_MODEL)
    g_ffn3 = g_ffn.reshape(DEPTH, 1, D_MODEL)
    n_rg = rg_conv_b.shape[0]
    rg_conv_b3 = rg_conv_b.reshape(n_rg, 1, D_RNN)
    rg_b_a4 = rg_b_a.reshape(n_rg, 2, 1, D_RNN)
    rg_b_i4 = rg_b_i.reshape(n_rg, 2, 1, D_RNN)
    rg_lam4 = rg_lambda.reshape(n_rg, 2, 1, D_RNN)
    ffn_conv_b3 = ffn_conv_b.reshape(DEPTH, 1, 2 * D_FF)
    for layer in range(DEPTH):
        ctx_out = layer < DEPTH - 1
        i = layer // 2
        h_lat = _norm_mod(x_lat, g_mix3, layer, mods, 0, 0, 1)
        h_ctx = _norm_mod(x_ctx, g_mix3, layer, mods, 1, 0, 1)
        if layer % 2 == 0:
            x_lat, x_ctx = _fourier_attn_layer(layer, i, x_lat, x_ctx, h_lat, h_ctx, mods, fa_w_in, fa_w_out,
                                               attn_sink, tables, ctx_out)
        else:
            x_lat, x_ctx = _rglru_layer(layer, i, x_lat, x_ctx, h_lat, h_ctx, mods, rg_w_in, rg_conv_w,
                                        rg_conv_b3, rg_w_a, rg_b_a4, rg_w_i, rg_b_i4, rg_lam4, rg_w_out,
                                        ctx_out)
        x_lat = _conv_ffn(layer, x_lat, g_ffn3, mods, 0, ffn_w_up, ffn_conv_w, ffn_conv_b3, ffn_w_down)
        if ctx_out:
            x_ctx = _conv_ffn(layer, x_ctx, g_ffn3, mods, 1, ffn_w_up, ffn_conv_w, ffn_conv_b3, ffn_w_down)
    return _final_norm(x_lat, g_final)[None]
```

```python
import functools
import math

import numpy as np
import jax
import jax.numpy as jnp
from jax import lax
from jax.experimental import pallas as pl
from jax.experimental.pallas import tpu as pltpu

D_MODEL = 2048
DEPTH = 4
GRID_W = 64
N_FG = 4
FG_W = 256
F_WIDTH = N_FG * FG_W
N_HEADS = 16
N_KV_HEADS = 2
HEAD_DIM = 64
ATT_WIDTH = N_HEADS * HEAD_DIM
KV_WIDTH = N_KV_HEADS * HEAD_DIM
WINDOW = 128
BLOCK = 128
ROPE_BASE = 10000.0
D_RNN = D_MODEL
N_RNN_BLOCKS = 8
RNN_BLOCK = D_RNN // N_RNN_BLOCKS
CONV_W = 4
CONV_LEFT = 2
LRU_C = 8.0
D_FF = 5632
FFN_CONV_W = 3
FFN_CONV_LEFT = 1
N_MOD = 6
EPS = 1e-6
NEG_INF = -1e30

LANES = 128
SUBLANES = 8
HALO = 16
VMEM_LIMIT = 56 * 1024 * 1024
ROW_CHUNK = 64
PROJ_BLOCK = 1024

BF16 = jnp.bfloat16
F32 = jnp.float32


def _params(*sem):
    return pltpu.CompilerParams(dimension_semantics=sem, vmem_limit_bytes=VMEM_LIMIT)


def _dot(a, b):
    return jnp.dot(a, b, preferred_element_type=F32)


def _dot_nt(a, b):
    return lax.dot_general(a, b, (((1,), (1,)), ((), ())), preferred_element_type=F32)


def _gelu_tanh(x):
    return 0.5 * x * (1.0 + jnp.tanh(math.sqrt(2.0 / math.pi) * (x + 0.044715 * (x * x * x))))


def _sigmoid(x):
    return 0.5 * (1.0 + jnp.tanh(0.5 * x))


def _mod_kernel(cl_ref, cc_ref, w_ref, b_ref, o_ref, sl_ref, sc_ref):
    @pl.when((pl.program_id(0) == 0) & (pl.program_id(1) == 0))
    def _():
        cl = cl_ref[...]
        cc = cc_ref[...]
        sl_ref[...] = cl * _sigmoid(cl)
        sc_ref[...] = cc * _sigmoid(cc)

    tn = w_ref.shape[1]
    reps = tn // LANES

    def body(kg, carry):
        al, ac = carry
        r0 = pl.multiple_of(kg * SUBLANES, SUBLANES)
        w8 = w_ref[pl.ds(r0, SUBLANES), :]
        s8l = jnp.tile(sl_ref[pl.ds(r0, SUBLANES), :], (1, reps))
        s8c = jnp.tile(sc_ref[pl.ds(r0, SUBLANES), :], (1, reps))
        return al + w8 * s8l, ac + w8 * s8c

    zero = jnp.zeros((SUBLANES, tn), F32)
    al, ac = lax.fori_loop(0, w_ref.shape[0] // SUBLANES, body, (zero, zero), unroll=4)
    b = b_ref[...]
    o_ref[0] = jnp.sum(al, axis=0, keepdims=True) + b
    o_ref[1] = jnp.sum(ac, axis=0, keepdims=True) + b


def _modulation(c, c_ctx, w_mod, b_mod):
    d = D_MODEL
    n = N_MOD * d
    tn = 1024
    cl = jnp.broadcast_to(c.reshape(d, 1), (d, LANES))
    cc = jnp.broadcast_to(c_ctx.reshape(d, 1), (d, LANES))
    return pl.pallas_call(
        _mod_kernel,
        grid=(DEPTH, n // tn),
        in_specs=[
            pl.BlockSpec((d, LANES), lambda l, j: (0, 0)),
            pl.BlockSpec((d, LANES), lambda l, j: (0, 0)),
            pl.BlockSpec((None, d, tn), lambda l, j: (l, 0, j)),
            pl.BlockSpec((None, 1, tn), lambda l, j: (l, 0, j)),
        ],
        out_specs=pl.BlockSpec((None, 2, 1, tn), lambda l, j: (l, 0, 0, j)),
        out_shape=jax.ShapeDtypeStruct((DEPTH, 2, 1, n), F32),
        scratch_shapes=[pltpu.VMEM((d, LANES), F32), pltpu.VMEM((d, LANES), F32)],
        compiler_params=_params("arbitrary", "arbitrary"),
        name="modulation",
    )(cl, cc, w_mod, b_mod.reshape(DEPTH, 1, n))


def _mod_spec(layer, row, k, tn, col_of):
    per = D_MODEL // tn
    return pl.BlockSpec((None, None, 1, tn), lambda *g: (layer, row, 0, k * per + col_of(*g)))


NORM_ROWS = 16


def _norm_mod_kernel(x_ref, g_ref, sh_ref, sc_ref, o_ref):
    g = g_ref[...]
    gain = g + g * sc_ref[...]
    shift = sh_ref[...]
    for r0 in range(0, x_ref.shape[0], NORM_ROWS):
        x = x_ref[r0:r0 + NORM_ROWS, :]
        r = lax.rsqrt(jnp.mean(x * x, axis=-1, keepdims=True) + EPS)
        o_ref[r0:r0 + NORM_ROWS, :] = ((x * r) * gain + shift).astype(o_ref.dtype)


def _norm_mod(x, g, layer, mods, row, k_shift, k_scale):
    m, d = x.shape
    tm = min(m, 512)
    zero = lambda i: 0
    return pl.pallas_call(
        _norm_mod_kernel,
        grid=(m // tm,),
        in_specs=[
            pl.BlockSpec((tm, d), lambda i: (i, 0)),
            pl.BlockSpec((None, 1, d), lambda i: (layer, 0, 0)),
            _mod_spec(layer, row, k_shift, d, zero),
            _mod_spec(layer, row, k_scale, d, zero),
        ],
        out_specs=pl.BlockSpec((tm, d), lambda i: (i, 0)),
        out_shape=jax.ShapeDtypeStruct((m, d), BF16),
        compiler_params=_params("arbitrary"),
        name="norm_mod",
    )(x, g, mods, mods)


def _cast_weights(w_refs, wb_ref):
    @pl.when(pl.program_id(1) == 0)
    def _():
        c0 = 0
        for w_ref in w_refs:
            wn = w_ref.shape[1]
            wb_ref[:, c0:c0 + wn] = w_ref[...].astype(BF16)
            c0 += wn


def _proj_plain_kernel(n_w, blk, h_ref, *refs):
    o_ref, wb_ref = refs[n_w:]
    _cast_weights(refs[:n_w], wb_ref)
    o_ref[...] = _dot(h_ref[...], wb_ref[...]).astype(o_ref.dtype)


def _blocked_dot(lhs_block, rows, blk, epilogue):
    blk = min(rows, blk)
    for r0 in range(0, rows, blk):
        acc = lhs_block(r0, blk)
        for r in range(0, blk, ROW_CHUNK):
            epilogue(acc[r:r + ROW_CHUNK, :], r0 + r)


def _proj_gelu_kernel(n_w, blk, h_ref, *refs):
    o_ref, wb_ref = refs[n_w:]
    _cast_weights(refs[:n_w], wb_ref)

    def epilogue(acc, r):
        o_ref[r:r + ROW_CHUNK, :] = _gelu_tanh(acc).astype(o_ref.dtype)

    _blocked_dot(lambda r0, n: _dot(h_ref[r0:r0 + n, :], wb_ref[...]), h_ref.shape[0], blk, epilogue)


def _swap16(x, even):
    return jnp.where(even, pltpu.roll(x, LANES - 16, axis=1), pltpu.roll(x, 16, axis=1))


def _proj_qkv_kernel(rope, n_w, blk, h_ref, *refs):
    if rope:
        cos_ref, sin_ref, o_ref, wb_ref = refs[n_w:]
    else:
        o_ref, wb_ref = refs[n_w:]
    _cast_weights(refs[:n_w], wb_ref)
    q_scale = HEAD_DIM ** -0.5
    n_rot = (ATT_WIDTH + KV_WIDTH) // LANES
    even = (lax.broadcasted_iota(jnp.int32, (ROW_CHUNK, LANES), 1) & 16) == 0

    def epilogue(acc, r):
        if rope:
            cos = cos_ref[r:r + ROW_CHUNK, :]
            sin = sin_ref[r:r + ROW_CHUNK, :]
        for cidx in range(acc.shape[1] // LANES):
            t = acc[:, cidx * LANES:(cidx + 1) * LANES]
            if cidx < ATT_WIDTH // LANES:
                t = t * q_scale
            if rope and cidx < n_rot:
                t = t * cos + _swap16(t, even) * sin
            o_ref[r:r + ROW_CHUNK, cidx * LANES:(cidx + 1) * LANES] = t.astype(o_ref.dtype)

    _blocked_dot(lambda r0, n: _dot(h_ref[r0:r0 + n, :], wb_ref[...]), h_ref.shape[0], blk, epilogue)


def _proj(kind, h, w, w_index, col0, n, tn, wtn, out_dtype, tm=None, extra=(), blk=PROJ_BLOCK):
    m, k = h.shape
    tm = tm or min(m, 1024)
    assert col0 % wtn == 0 and tn % wtn == 0 and n % tn == 0 and m % tm == 0
    n_w = tn // wtn
    kernels = {
        "plain": _proj_plain_kernel,
        "gelu": _proj_gelu_kernel,
        "qkv": functools.partial(_proj_qkv_kernel, False),
        "qkv_rope": functools.partial(_proj_qkv_kernel, True),
    }
    in_specs = [pl.BlockSpec((tm, k), lambda j, i: (i, 0))]
    for p in range(n_w):
        in_specs.append(pl.BlockSpec((None, k, wtn),
                                     lambda j, i, p=p: (w_index, 0, col0 // wtn + j * n_w + p)))
    in_specs += [pl.BlockSpec((tm, LANES), lambda j, i: (i, 0)) for _ in extra]
    return pl.pallas_call(
        functools.partial(kernels[kind], n_w, blk),
        grid=(n // tn, m // tm),
        in_specs=in_specs,
        out_specs=pl.BlockSpec((tm, tn), lambda j, i: (i, j)),
        out_shape=jax.ShapeDtypeStruct((m, n), out_dtype),
        scratch_shapes=[pltpu.VMEM((k, tn), BF16)],
        compiler_params=_params("arbitrary", "arbitrary"),
        name="proj_" + kind,
    )(h, *([w] * n_w), *extra)


def _proj_res_kernel(n_parts, blk, *refs):
    a_refs = refs[:n_parts]
    w_ref, x_ref, gt_ref, o_ref, wb_ref = refs[n_parts:]
    _cast_weights([w_ref], wb_ref)

    def lhs_block(r0, n):
        acc = None
        k0 = 0
        for a_ref in a_refs:
            kp = a_ref.shape[1]
            part = _dot(a_ref[r0:r0 + n, :], wb_ref[k0:k0 + kp, :])
            acc = part if acc is None else acc + part
            k0 += kp
        return acc

    def epilogue(acc, r):
        o_ref[r:r + ROW_CHUNK, :] = x_ref[r:r + ROW_CHUNK, :] + gt_ref[...] * acc

    _blocked_dot(lhs_block, x_ref.shape[0], blk, epilogue)


def _proj_residual(parts, w, w_index, x, layer, mods, row, k_gate, tm, tn, blk=PROJ_BLOCK):
    m, n = x.shape
    k = sum(p.shape[1] for p in parts)
    tm = min(m, tm)
    in_specs = [pl.BlockSpec((tm, p.shape[1]), lambda j, i: (i, 0)) for p in parts]
    in_specs += [
        pl.BlockSpec((None, k, tn), lambda j, i: (w_index, 0, j)),
        pl.BlockSpec((tm, tn), lambda j, i: (i, j)),
        _mod_spec(layer, row, k_gate, tn, lambda j, i: j),
    ]
    return pl.pallas_call(
        functools.partial(_proj_res_kernel, len(parts), blk),
        grid=(n // tn, m // tm),
        in_specs=in_specs,
        out_specs=pl.BlockSpec((tm, tn), lambda j, i: (i, j)),
        out_shape=jax.ShapeDtypeStruct((m, n), F32),
        scratch_shapes=[pltpu.VMEM((k, tn), BF16)],
        compiler_params=_params("arbitrary", "arbitrary"),
        name="proj_residual",
    )(*parts, w, x, mods)


def _conv_proj_kernel(n_w, kw, left, gated, h_ref, hp_ref, hn_ref, *refs):
    w_refs = refs[:n_w]
    cw_refs = refs[n_w:2 * n_w]
    cb_refs = refs[2 * n_w:3 * n_w]
    o_ref, wb_ref, ext_ref, z_ref = refs[3 * n_w:]
    i = pl.program_id(1)
    tm = h_ref.shape[0]
    tn = w_refs[0].shape[1]

    @pl.when(i == 0)
    def _():
        for p, w_ref in enumerate(w_refs):
            wb_ref[:, p * tn:(p + 1) * tn] = w_ref[...].astype(BF16)

    ext_ref[0:HALO, :] = jnp.where(i > 0, hp_ref[...], jnp.zeros_like(hp_ref))
    ext_ref[HALO:HALO + tm, :] = h_ref[...]
    ext_ref[HALO + tm:, :] = jnp.where(i < pl.num_programs(1) - 1, hn_ref[...], jnp.zeros_like(hn_ref))

    z = _dot(ext_ref[...], wb_ref[...])
    out_slabs = tn // LANES
    for t in range(n_w * out_slabs):
        z_ref[t] = z[:, t * LANES:(t + 1) * LANES]
    for so in range(out_slabs):
        c0 = so * LANES
        for r in range(0, tm, ROW_CHUNK):
            outs = []
            for p in range(n_w):
                u = cb_refs[p][:, c0:c0 + LANES]
                for tap in range(kw):
                    r0 = HALO - left + tap + r
                    u = u + z_ref[p * out_slabs + so, r0:r0 + ROW_CHUNK, :] * cw_refs[p][tap:tap + 1, c0:c0 + LANES]
                outs.append(u)
            if gated:
                g, v = outs
                res = g * _sigmoid(g) * v
            else:
                res = outs[0]
            o_ref[r:r + ROW_CHUNK, c0:c0 + LANES] = res.astype(o_ref.dtype)


def _conv_proj(h, w, w_index, col_blocks, conv_blocks, n_out, cw, cb, kw, left, gated, tm, tn, out_dtype):
    m, k = h.shape
    tm = min(m, tm)
    n_w = len(col_blocks)
    hb = m // HALO
    tb = tm // HALO
    in_specs = [
        pl.BlockSpec((tm, k), lambda j, i: (i, 0)),
        pl.BlockSpec((HALO, k), lambda j, i: (jnp.maximum(i * tb - 1, 0), 0)),
        pl.BlockSpec((HALO, k), lambda j, i: (jnp.minimum((i + 1) * tb, hb - 1), 0)),
    ]
    for c0 in col_blocks:
        in_specs.append(pl.BlockSpec((None, k, tn), lambda j, i, c0=c0: (w_index, 0, c0 + j)))
    for c0 in conv_blocks:
        in_specs.append(pl.BlockSpec((None, kw, tn), lambda j, i, c0=c0: (w_index, 0, c0 + j)))
    for c0 in conv_blocks:
        in_specs.append(pl.BlockSpec((None, 1, tn), lambda j, i, c0=c0: (w_index, 0, c0 + j)))
    args = [h, h, h] + [w] * n_w + [cw] * n_w + [cb] * n_w
    ext_rows = tm + 2 * HALO
    return pl.pallas_call(
        functools.partial(_conv_proj_kernel, n_w, kw, left, gated),
        grid=(n_out // tn, m // tm),
        in_specs=in_specs,
        out_specs=pl.BlockSpec((tm, tn), lambda j, i: (i, j)),
        out_shape=jax.ShapeDtypeStruct((m, n_out), out_dtype),
        scratch_shapes=[pltpu.VMEM((k, n_w * tn), BF16), pltpu.VMEM((ext_rows, k), BF16),
                        pltpu.VMEM((n_w * tn // LANES, ext_rows, LANES), F32)],
        compiler_params=_params("arbitrary", "arbitrary"),
        name="conv_proj",
    )(*args)


def _pair_operand(band, kv_head):
    b = band.astype(F32)
    rolled = pltpu.roll(b, HEAD_DIM, axis=1)
    low = lax.broadcasted_iota(jnp.int32, b.shape, 1) < HEAD_DIM
    zero = jnp.zeros_like(b)
    if kv_head == 0:
        top = jnp.where(low, b, zero)
        bot = jnp.where(low, zero, rolled)
    else:
        top = jnp.where(low, rolled, zero)
        bot = jnp.where(low, zero, b)
    return jnp.concatenate([top, bot], axis=0).astype(BF16)


def _attention_core(q_ref, k_band, v_band, biases, sink_ref, fa_index, o_ref):
    tq = q_ref.shape[0]
    nk = k_band.shape[0]
    low = lax.broadcasted_iota(jnp.int32, (tq, LANES), 1) < HEAD_DIM
    pairs = N_HEADS // N_KV_HEADS // 2
    for kv_head in range(N_KV_HEADS):
        k2 = _pair_operand(k_band, kv_head)
        v2 = _pair_operand(v_band, kv_head)
        col0 = [(kv_head * pairs + pair) * LANES for pair in range(pairs)]
        q_rows = jnp.concatenate([q_ref[:, c0:c0 + LANES] for c0 in col0], axis=0)
        s_all = _dot_nt(q_rows, k2)
        p_rows, inv_rows = [], []
        for pair, c0 in enumerate(col0):
            s = s_all[pair * tq:(pair + 1) * tq, :]
            ps, inv = [], []
            for half in range(2):
                sink = sink_ref[fa_index, c0 // HEAD_DIM + half]
                cols = []
                for j in range(nk // LANES):
                    blk = s[:, half * nk + j * LANES:half * nk + (j + 1) * LANES]
                    cols.append(blk + biases[j] if j in biases else blk)
                top = cols[0]
                for blk in cols[1:]:
                    top = jnp.maximum(top, blk)
                mx = jnp.maximum(jnp.max(top, axis=1, keepdims=True), sink)
                tot = None
                for blk in cols:
                    p = jnp.exp(blk - mx)
                    tot = p if tot is None else tot + p
                    ps.append(p.astype(BF16))
                den = jnp.sum(tot, axis=1, keepdims=True) + jnp.exp(sink - mx)
                inv.append(1.0 / den)
            p_rows.append(jnp.concatenate(ps, axis=1))
            inv_rows.append(jnp.where(low, inv[0], inv[1]))
        o_all = _dot(jnp.concatenate(p_rows, axis=0), v2)
        for pair, c0 in enumerate(col0):
            o = o_all[pair * tq:(pair + 1) * tq, :] * inv_rows[pair]
            o_ref[:, c0:c0 + LANES] = o.astype(o_ref.dtype)


def _attn_lat_kernel(fa_index, sink_ref, q_ref, kp_ref, kc_ref, kn_ref, vp_ref, vc_ref, vn_ref,
                     kx_ref, vx_ref, o_ref):
    n = pl.program_id(0)
    nb = pl.num_programs(0)
    k_band = jnp.concatenate([kp_ref[...], kc_ref[...], kn_ref[...], kx_ref[...]], axis=0)
    v_band = jnp.concatenate([vp_ref[...], vc_ref[...], vn_ref[...], vx_ref[...]], axis=0)
    qi = lax.broadcasted_iota(jnp.int32, (BLOCK, BLOCK), 0)
    kj = lax.broadcasted_iota(jnp.int32, (BLOCK, BLOCK), 1)
    bias_prev = jnp.where((kj >= qi) & (n > 0), 0.0, NEG_INF).astype(F32)
    bias_next = jnp.where((kj <= qi) & (n < nb - 1), 0.0, NEG_INF).astype(F32)
    _attention_core(q_ref, k_band, v_band, {0: bias_prev, 2: bias_next}, sink_ref, fa_index, o_ref)


def _attn_ctx_kernel(fa_index, sink_ref, q_ref, kx_ref, vx_ref, o_ref):
    _attention_core(q_ref, kx_ref[...], vx_ref[...], {}, sink_ref, fa_index, o_ref)


def _attention_lat(qkv, qkv_ctx, sink, fa_index):
    s = qkv.shape[0]
    n_ctx = qkv_ctx.shape[0]
    nb = s // BLOCK
    kcol = ATT_WIDTH // LANES
    vcol = kcol + 1
    prev = lambda n: jnp.maximum(n - 1, 0)
    nxt = lambda n: jnp.minimum(n + 1, nb - 1)
    in_specs = [
        pl.BlockSpec(memory_space=pltpu.SMEM),
        pl.BlockSpec((BLOCK, ATT_WIDTH), lambda n: (n, 0)),
        pl.BlockSpec((BLOCK, LANES), lambda n: (prev(n), kcol)),
        pl.BlockSpec((BLOCK, LANES), lambda n: (n, kcol)),
        pl.BlockSpec((BLOCK, LANES), lambda n: (nxt(n), kcol)),
        pl.BlockSpec((BLOCK, LANES), lambda n: (prev(n), vcol)),
        pl.BlockSpec((BLOCK, LANES), lambda n: (n, vcol)),
        pl.BlockSpec((BLOCK, LANES), lambda n: (nxt(n), vcol)),
        pl.BlockSpec((n_ctx, LANES), lambda n: (0, kcol)),
        pl.BlockSpec((n_ctx, LANES), lambda n: (0, vcol)),
    ]
    return pl.pallas_call(
        functools.partial(_attn_lat_kernel, fa_index),
        grid=(nb,),
        in_specs=in_specs,
        out_specs=pl.BlockSpec((BLOCK, ATT_WIDTH), lambda n: (n, 0)),
        out_shape=jax.ShapeDtypeStruct((s, ATT_WIDTH), BF16),
        compiler_params=_params("arbitrary"),
        name="attention_latent",
    )(sink, qkv, qkv, qkv, qkv, qkv, qkv, qkv, qkv_ctx, qkv_ctx)


def _attention_ctx(qkv_ctx, sink, fa_index):
    n_ctx = qkv_ctx.shape[0]
    kcol = ATT_WIDTH // LANES
    return pl.pallas_call(
        functools.partial(_attn_ctx_kernel, fa_index),
        grid=(1,),
        in_specs=[
            pl.BlockSpec(memory_space=pltpu.SMEM),
            pl.BlockSpec((n_ctx, ATT_WIDTH), lambda n: (0, 0)),
            pl.BlockSpec((n_ctx, LANES), lambda n: (0, kcol)),
            pl.BlockSpec((n_ctx, LANES), lambda n: (0, kcol + 1)),
        ],
        out_specs=pl.BlockSpec((n_ctx, ATT_WIDTH), lambda n: (0, 0)),
        out_shape=jax.ShapeDtypeStruct((n_ctx, ATT_WIDTH), BF16),
        compiler_params=_params("arbitrary"),
        name="attention_context",
    )(sink, qkv_ctx, qkv_ctx, qkv_ctx)


FFT_N1 = 64
FFT_N2 = 128


def _dft_cos_sin(n):
    idx = np.arange(n)
    ang = 2.0 * np.pi * ((idx[:, None] * idx[None, :]) % n) / n
    return np.cos(ang), np.sin(ang)


def _fourier_constants():
    c1, s1 = _dft_cos_sin(FFT_N1)
    stage1 = np.concatenate([c1, -s1], axis=0)
    c2, s2 = _dft_cos_sin(FFT_N2)
    stage2 = np.block([[c2, s2], [-s2, c2]])
    cc, sc = _dft_cos_sin(FG_W)
    chan = np.concatenate([cc, sc], axis=0)
    return (jnp.asarray(stage1, F32), jnp.asarray(stage2, F32), jnp.asarray(chan, F32))


def _twiddle_tables():
    k1 = jnp.arange(FFT_N1, dtype=jnp.int32)[:, None]
    n2 = jnp.arange(FFT_N2, dtype=jnp.int32)[None, :]
    ang = ((k1 * n2) % (FFT_N1 * FFT_N2)).astype(F32) * (2.0 * math.pi / (FFT_N1 * FFT_N2))
    wr = jnp.repeat(jnp.cos(ang), LANES, axis=1)
    wi = jnp.repeat(-jnp.sin(ang), LANES, axis=1)
    return wr, wi


def _fourier_stage1_kernel(x_ref, m_ref, wr_ref, wi_ref, tr_ref, ti_ref):
    y = _dot(m_ref[...].astype(BF16), x_ref[...].astype(BF16))
    reps = F_WIDTH // LANES
    for b in range(x_ref.shape[1] // F_WIDTH):
        cols = slice(b * F_WIDTH, (b + 1) * F_WIDTH)
        yr = y[:FFT_N1, cols]
        yi = y[FFT_N1:, cols]
        wr = jnp.tile(wr_ref[:, b * LANES:(b + 1) * LANES], (1, reps))
        wi = jnp.tile(wi_ref[:, b * LANES:(b + 1) * LANES], (1, reps))
        tr_ref[:, cols] = (yr * wr - yi * wi).astype(tr_ref.dtype)
        ti_ref[:, cols] = (yr * wi + yi * wr).astype(ti_ref.dtype)


def _channel_stage(p, chan_ref, scale, o_ref):
    r = p.shape[0] // 2
    pr = p[:r].astype(BF16)
    pi = p[r:].astype(BF16)
    chan_c = chan_ref[:FG_W, :].astype(BF16)
    chan_s = chan_ref[FG_W:, :].astype(BF16)
    for g in range(N_FG):
        cols = slice(g * FG_W, (g + 1) * FG_W)
        y = _dot(pr[:, cols], chan_c) + _dot(pi[:, cols], chan_s)
        o_ref[:, cols] = (y * scale).astype(o_ref.dtype)


def _fourier_stage2_kernel(scale, tr_ref, ti_ref, m_ref, chan_ref, o_ref):
    m = m_ref[...].astype(BF16)
    for kk in range(tr_ref.shape[0]):
        t = jnp.concatenate([tr_ref[kk], ti_ref[kk]], axis=0)
        _channel_stage(_dot(m, t), chan_ref, scale, o_ref.at[:, kk * F_WIDTH:(kk + 1) * F_WIDTH])


def _fourier_ctx_kernel(scale, x_ref, m_ref, chan_ref, o_ref):
    _channel_stage(_dot(m_ref[...].astype(BF16), x_ref[...].astype(BF16)), chan_ref, scale, o_ref)


def _fourier_lat(f, consts, twiddles):
    n = f.shape[0]
    assert n == FFT_N1 * FFT_N2
    stage1, stage2, chan = consts
    wr, wi = twiddles
    n2_blk = 8
    cols = n2_blk * F_WIDTH
    full = lambda a: pl.BlockSpec(a.shape, lambda j: (0,) * a.ndim)
    tr, ti = pl.pallas_call(
        _fourier_stage1_kernel,
        grid=(FFT_N2 // n2_blk,),
        in_specs=[
            pl.BlockSpec((FFT_N1, cols), lambda j: (0, j)),
            full(stage1),
            pl.BlockSpec((FFT_N1, n2_blk * LANES), lambda j: (0, j)),
            pl.BlockSpec((FFT_N1, n2_blk * LANES), lambda j: (0, j)),
        ],
        out_specs=[pl.BlockSpec((FFT_N1, cols), lambda j: (0, j))] * 2,
        out_shape=[jax.ShapeDtypeStruct((FFT_N1, FFT_N2 * F_WIDTH), BF16)] * 2,
        compiler_params=_params("arbitrary"),
        name="fourier_stage1",
    )(f.reshape(FFT_N1, FFT_N2 * F_WIDTH), stage1, wr, wi)
    tr = tr.reshape(FFT_N1, FFT_N2, F_WIDTH)
    ti = ti.reshape(FFT_N1, FFT_N2, F_WIDTH)
    scale = 1.0 / math.sqrt(n * FG_W)
    k1_blk = 4
    out = pl.pallas_call(
        functools.partial(_fourier_stage2_kernel, scale),
        grid=(FFT_N1 // k1_blk,),
        in_specs=[
            pl.BlockSpec((k1_blk, FFT_N2, F_WIDTH), lambda k1: (k1, 0, 0)),
            pl.BlockSpec((k1_blk, FFT_N2, F_WIDTH), lambda k1: (k1, 0, 0)),
            full(stage2),
            full(chan),
        ],
        out_specs=pl.BlockSpec((FFT_N2, k1_blk * F_WIDTH), lambda k1: (0, k1)),
        out_shape=jax.ShapeDtypeStruct((FFT_N2, FFT_N1 * F_WIDTH), BF16),
        compiler_params=_params("arbitrary"),
        name="fourier_stage2",
    )(tr, ti, stage2, chan)
    return out.reshape(n, F_WIDTH)


def _fourier_ctx(f, chan):
    n = f.shape[0]
    c, s = _dft_cos_sin(n)
    m = jnp.asarray(np.concatenate([c, -s], axis=0), F32)
    full = lambda a: pl.BlockSpec(a.shape, lambda j: (0,) * a.ndim)
    return pl.pallas_call(
        functools.partial(_fourier_ctx_kernel, 1.0 / math.sqrt(n * FG_W)),
        grid=(1,),
        in_specs=[full(f), full(m), full(chan)],
        out_specs=pl.BlockSpec((n, F_WIDTH), lambda j: (0, 0)),
        out_shape=jax.ShapeDtypeStruct((n, F_WIDTH), BF16),
        compiler_params=_params("arbitrary"),
        name="fourier_context",
    )(f, m, chan)


def _scan8(a, b, row, reverse):
    for d in (1, 2, 4):
        shift = (SUBLANES - d) if reverse else d
        keep = (row < SUBLANES - d) if reverse else (row >= d)
        b = jnp.where(keep, b + a * pltpu.roll(b, shift, axis=0), b)
        a = jnp.where(keep, a * pltpu.roll(a, shift, axis=0), a)
    return a, b


def _rows(x, n):
    return jnp.broadcast_to(x, (n, x.shape[1]))


def _rglru_kernel(reverse, combine, xs_ref, wa_ref, wi_ref, ba_ref, bi_ref, lam_ref, h0_ref, *refs):
    if combine:
        hb_ref, gg_ref, o_ref, last_ref, w_scr, a_scr, b_scr, c_scr, carry_scr = refs
    else:
        o_ref, last_ref, w_scr, a_scr, b_scr, c_scr, carry_scr = refs
    t = pl.program_id(1)
    tt, c = xs_ref.shape
    n_slab = c // LANES
    groups = tt // SUBLANES
    blocks = groups // SUBLANES
    row = lax.broadcasted_iota(jnp.int32, (SUBLANES, LANES), 0)
    edge = 0 if reverse else SUBLANES - 1

    @pl.when(t == 0)
    def _():
        carry_scr[...] = jnp.broadcast_to(h0_ref[...], carry_scr.shape)
        w_scr[:, :c] = (0.5 * wa_ref[...]).astype(BF16)
        w_scr[:, c:] = (0.5 * wi_ref[...]).astype(BF16)

    pre = _dot(xs_ref[...].astype(BF16), w_scr[...])
    neg_lam = -lam_ref[...]
    softplus = jnp.maximum(neg_lam, 0.0) + jnp.log1p(jnp.exp(-jnp.abs(neg_lam)))
    k_all = softplus * (-0.5 * LRU_C * math.log2(math.e))
    ba_all = 0.5 * ba_ref[...]
    bi_all = 0.5 * bi_ref[...]

    for s in range(n_slab):
        lanes = slice(s * LANES, (s + 1) * LANES)
        k = _rows(k_all[:, lanes], SUBLANES)
        ba = _rows(ba_all[:, lanes], SUBLANES)
        bi = _rows(bi_all[:, lanes], SUBLANES)
        for g in range(groups):
            r0 = g * SUBLANES
            tr = jnp.tanh(pre[r0:r0 + SUBLANES, s * LANES:(s + 1) * LANES] + ba)
            ti = jnp.tanh(pre[r0:r0 + SUBLANES, c + s * LANES:c + (s + 1) * LANES] + bi)
            a = jnp.exp2((1.0 + tr) * k)
            y = 1.0 - a * a
            root = jnp.where(y > 0.0, y * lax.rsqrt(y), 0.0)
            b = root * ((1.0 + ti) * (0.5 * xs_ref[r0:r0 + SUBLANES, lanes]))
            a_cum, b_loc = _scan8(a, b, row, reverse)
            a_scr[s, r0:r0 + SUBLANES, :] = a_cum
            b_scr[s, r0:r0 + SUBLANES, :] = b_loc

    for s in range(n_slab):
        a2 = a_scr[s, pl.ds(edge, groups, stride=SUBLANES), :]
        b2 = b_scr[s, pl.ds(edge, groups, stride=SUBLANES), :]
        carry = carry_scr[:, s * LANES:(s + 1) * LANES]
        enter_row = groups if reverse else SUBLANES - 1
        base = 0 if reverse else SUBLANES
        c_scr[s, enter_row:enter_row + 1, :] = carry[0:1, :]
        for j in (range(blocks - 1, -1, -1) if reverse else range(blocks)):
            r0 = j * SUBLANES
            a_cum, b_loc = _scan8(a2[r0:r0 + SUBLANES, :], b2[r0:r0 + SUBLANES, :], row, reverse)
            st = b_loc + a_cum * carry
            c_scr[s, base + r0:base + r0 + SUBLANES, :] = st
            carry = _rows(st[edge:edge + 1, :], SUBLANES)
        carry_scr[:, s * LANES:(s + 1) * LANES] = carry
        last_ref[:, s * LANES:(s + 1) * LANES] = carry[0:1, :]

    pair = 2 * SUBLANES
    for s in range(n_slab):
        lanes = slice(s * LANES, (s + 1) * LANES)
        for g in range(0, groups, 2):
            r0 = g * SUBLANES
            src = g + 1 if reverse else g + SUBLANES - 1
            enter = jnp.concatenate([_rows(c_scr[s, src:src + 1, :], SUBLANES),
                                     _rows(c_scr[s, src + 1:src + 2, :], SUBLANES)], axis=0)
            h = b_scr[s, r0:r0 + pair, :] + a_scr[s, r0:r0 + pair, :] * enter
            if combine:
                h = (h + hb_ref[r0:r0 + pair, lanes]) * gg_ref[r0:r0 + pair, lanes].astype(F32)
            o_ref[r0:r0 + pair, lanes] = h.astype(o_ref.dtype)


def _rglru_scan(xs, w_a, w_i, b_a, b_i, lam, rg_index, direction, h0, h_other=None, gelu_gate=None):
    m = xs.shape[0]
    tt = min(m, 1024)
    nt = m // tt
    c = RNN_BLOCK
    reverse = direction == 1
    combine = h_other is not None
    tix = (lambda t: nt - 1 - t) if reverse else (lambda t: t)
    wspec = pl.BlockSpec((None, None, None, c, c), lambda cb, t: (rg_index, direction, cb, 0, 0))
    vspec = pl.BlockSpec((None, None, 1, c), lambda cb, t: (rg_index, direction, 0, cb))
    tile = pl.BlockSpec((tt, c), lambda cb, t: (tix(t), cb))
    in_specs = [tile, wspec, wspec, vspec, vspec, vspec, pl.BlockSpec((1, c), lambda cb, t: (0, cb))]
    args = [xs, w_a, w_i, b_a, b_i, lam, h0]
    if combine:
        in_specs += [tile, tile]
        args += [h_other, gelu_gate]
    return pl.pallas_call(
        functools.partial(_rglru_kernel, reverse, combine),
        grid=(D_RNN // c, nt),
        in_specs=in_specs,
        out_specs=[tile, pl.BlockSpec((1, c), lambda cb, t: (0, cb))],
        out_shape=[jax.ShapeDtypeStruct((m, D_RNN), BF16 if combine else F32),
                   jax.ShapeDtypeStruct((1, D_RNN), F32)],
        scratch_shapes=[pltpu.VMEM((c, 2 * c), BF16),
                        pltpu.VMEM((c // LANES, tt, LANES), F32),
                        pltpu.VMEM((c // LANES, tt, LANES), F32),
                        pltpu.VMEM((c // LANES, tt // SUBLANES + 2 * SUBLANES, LANES), F32),
                        pltpu.VMEM((SUBLANES, c), F32)],
        compiler_params=_params("arbitrary", "arbitrary"),
        name="rglru_scan",
    )(*args)


def _final_norm_kernel(x_ref, g_ref, o_ref):
    g = g_ref[...]
    for r0 in range(0, x_ref.shape[0], NORM_ROWS):
        x = x_ref[r0:r0 + NORM_ROWS, :]
        r = lax.rsqrt(jnp.mean(x * x, axis=-1, keepdims=True) + EPS)
        o_ref[r0:r0 + NORM_ROWS, :] = (x * r) * g


def _final_norm(x, g):
    m, d = x.shape
    tm = 512
    return pl.pallas_call(
        _final_norm_kernel,
        grid=(m // tm,),
        in_specs=[pl.BlockSpec((tm, d), lambda i: (i, 0)), pl.BlockSpec((1, d), lambda i: (0, 0))],
        out_specs=pl.BlockSpec((tm, d), lambda i: (i, 0)),
        out_shape=jax.ShapeDtypeStruct((m, d), F32),
        compiler_params=_params("arbitrary"),
        name="final_norm",
    )(x, g.reshape(1, d))


def _rope_tables(n):
    f = HEAD_DIM // 4
    inv = ROPE_BASE ** (-jnp.arange(f, dtype=F32) / f)
    pos = jnp.arange(n, dtype=jnp.int32)
    ang_r = (pos // GRID_W).astype(F32)[:, None] * inv[None, :]
    ang_c = (pos % GRID_W).astype(F32)[:, None] * inv[None, :]
    cr, sr, cc, sc = jnp.cos(ang_r), jnp.sin(ang_r), jnp.cos(ang_c), jnp.sin(ang_c)
    cos = jnp.concatenate([cr, cr, cc, cc], axis=1)
    sin = jnp.concatenate([-sr, sr, -sc, sc], axis=1)
    reps = LANES // HEAD_DIM
    return jnp.tile(cos, (1, reps)), jnp.tile(sin, (1, reps))


def _fourier_attn_layer(layer, i, x_lat, x_ctx, h_lat, h_ctx, mods, fa_w_in, fa_w_out, attn_sink,
                        tables, ctx_out):
    rope, consts, twiddles = tables
    qkv_w = ATT_WIDTH + 2 * KV_WIDTH
    f_lat = _proj("plain", h_lat, fa_w_in, i, 0, F_WIDTH, F_WIDTH, F_WIDTH, BF16)
    qkv_lat = _proj("qkv_rope", h_lat, fa_w_in, i, F_WIDTH, qkv_w, qkv_w, 256, BF16, tm=512, extra=rope)
    qkv_ctx = _proj("qkv", h_ctx, fa_w_in, i, F_WIDTH, qkv_w, qkv_w, 256, BF16)
    fo_lat = _fourier_lat(f_lat, consts, twiddles)
    ao_lat = _attention_lat(qkv_lat, qkv_ctx, attn_sink, i)
    x_lat = _proj_residual([fo_lat, ao_lat], fa_w_out, i, x_lat, layer, mods, 0, 2, 1024, 1024)
    if ctx_out:
        f_ctx = _proj("plain", h_ctx, fa_w_in, i, 0, F_WIDTH, F_WIDTH, F_WIDTH, BF16)
        fo_ctx = _fourier_ctx(f_ctx, consts[2])
        ao_ctx = _attention_ctx(qkv_ctx, attn_sink, i)
        x_ctx = _proj_residual([fo_ctx, ao_ctx], fa_w_out, i, x_ctx, layer, mods, 1, 2, 1024, 1024)
    return x_lat, x_ctx


def _rglru_layer(layer, i, x_lat, x_ctx, h_lat, h_ctx, mods, rg_w_in, rg_conv_w, rg_conv_b, w_a, b_a,
                 w_i, b_i, lam, rg_w_out, ctx_out):
    tn = 1024
    xcol = D_RNN // tn
    zero_state = jnp.zeros((1, D_RNN), F32)
    gate_lat = _proj("gelu", h_lat, rg_w_in, i, 0, D_RNN, tn, tn, BF16)
    xs_lat = _conv_proj(h_lat, rg_w_in, i, [xcol], [0], D_RNN, rg_conv_w, rg_conv_b, CONV_W, CONV_LEFT,
                        False, 1024, tn, F32)
    xs_ctx = _conv_proj(h_ctx, rg_w_in, i, [xcol], [0], D_RNN, rg_conv_w, rg_conv_b, CONV_W, CONV_LEFT,
                        False, 1024, tn, F32)
    scan = functools.partial(_rglru_scan, w_a=w_a, w_i=w_i, b_a=b_a, b_i=b_i, lam=lam, rg_index=i)
    hb_ctx, s_bwd = scan(xs_ctx, direction=1, h0=zero_state)
    if ctx_out:
        gate_ctx = _proj("gelu", h_ctx, rg_w_in, i, 0, D_RNN, tn, tn, BF16)
        y_ctx, s_fwd = scan(xs_ctx, direction=0, h0=zero_state, h_other=hb_ctx, gelu_gate=gate_ctx)
    else:
        _, s_fwd = scan(xs_ctx, direction=0, h0=zero_state)
    hb_lat, _ = scan(xs_lat, direction=1, h0=s_bwd)
    y_lat, _ = scan(xs_lat, direction=0, h0=s_fwd, h_other=hb_lat, gelu_gate=gate_lat)
    x_lat = _proj_residual([y_lat], rg_w_out, i, x_lat, layer, mods, 0, 2, 1024, 1024)
    if ctx_out:
        x_ctx = _proj_residual([y_ctx], rg_w_out, i, x_ctx, layer, mods, 1, 2, 1024, 1024)
    return x_lat, x_ctx


def _conv_ffn(layer, x, g_ffn, mods, row, w_up, conv_w, conv_b, w_down):
    h = _norm_mod(x, g_ffn, layer, mods, row, 3, 4)
    tf = 512
    blocks = [0, D_FF // tf]
    act = _conv_proj(h, w_up, layer, blocks, blocks, D_FF, conv_w, conv_b, FFN_CONV_W, FFN_CONV_LEFT, True,
                     1024, tf, BF16)
    return _proj_residual([act], w_down, layer, x, layer, mods, row, 5, 512, 512)


def kernel(x, c, ctx, c_ctx, w_mod, b_mod, g_mix, g_ffn, fa_w_in, fa_w_out, attn_sink, rg_w_in, rg_conv_w,
           rg_conv_b, rg_w_a, rg_b_a, rg_w_i, rg_b_i, rg_lambda, rg_w_out, ffn_w_up, ffn_conv_w, ffn_conv_b,
           ffn_w_down, g_final):
    assert x.shape[0] == 1 and ctx.shape[0] == 1
    n = x.shape[1]
    x_lat = x[0]
    x_ctx = ctx[0]
    mods = _modulation(c, c_ctx, w_mod, b_mod)
    tables = (_rope_tables(n), _fourier_constants(), _twiddle_tables())
    g_mix3 = g_mix.reshape(DEPTH, 1, D_MODEL)
    g_ffn3 = g_ffn.reshape(DEPTH, 1, D_MODEL)
    n_rg = rg_conv_b.shape[0]
    rg_conv_b3 = rg_conv_b.reshape(n_rg, 1, D_RNN)
    rg_b_a4 = rg_b_a.reshape(n_rg, 2, 1, D_RNN)
    rg_b_i4 = rg_b_i.reshape(n_rg, 2, 1, D_RNN)
    rg_lam4 = rg_lambda.reshape(n_rg, 2, 1, D_RNN)
    ffn_conv_b3 = ffn_conv_b.reshape(DEPTH, 1, 2 * D_FF)
    for layer in range(DEPTH):
        ctx_out = layer < DEPTH - 1
        i = layer // 2
        h_lat = _norm_mod(x_lat, g_mix3, layer, mods, 0, 0, 1)
        h_ctx = _norm_mod(x_ctx, g_mix3, layer, mods, 1, 0, 1)
        if layer % 2 == 0:
            x_lat, x_ctx = _fourier_attn_layer(layer, i, x_lat, x_ctx, h_lat, h_ctx, mods, fa_w_in, fa_w_out,
                                               attn_sink, tables, ctx_out)
        else:
            x_lat, x_ctx = _rglru_layer(layer, i, x_lat, x_ctx, h_lat, h_ctx, mods, rg_w_in, rg_conv_w,
                                        rg_conv_b3, rg_w_a, rg_b_a4, rg_w_i, rg_b_i4, rg_lam4, rg_w_out,
                                        ctx_out)
        x_lat = _conv_ffn(layer, x_lat, g_ffn3, mods, 0, ffn_w_up, ffn_conv_w, ffn_conv_b3, ffn_w_down)
        if ctx_out:
            x_ctx = _conv_ffn(layer, x_ctx, g_ffn3, mods, 1, ffn_w_up, ffn_conv_w, ffn_conv_b3, ffn_w_down)
    return _final_norm(x_lat, g_final)[None]
```

```python
import functools
import math

import numpy as np
import jax
import jax.numpy as jnp
from jax import lax
from jax.experimental import pallas as pl
from jax.experimental.pallas import tpu as pltpu

D_MODEL = 2048
DEPTH = 4
GRID_W = 64
N_FG = 4
FG_W = 256
F_WIDTH = N_FG * FG_W
N_HEADS = 16
N_KV_HEADS = 2
HEAD_DIM = 64
ATT_WIDTH = N_HEADS * HEAD_DIM
KV_WIDTH = N_KV_HEADS * HEAD_DIM
WINDOW = 128
BLOCK = 128
ROPE_BASE = 10000.0
D_RNN = D_MODEL
N_RNN_BLOCKS = 8
RNN_BLOCK = D_RNN // N_RNN_BLOCKS
CONV_W = 4
CONV_LEFT = 2
LRU_C = 8.0
D_FF = 5632
FFN_CONV_W = 3
FFN_CONV_LEFT = 1
N_MOD = 6
EPS = 1e-6
NEG_INF = -1e30

LANES = 128
SUBLANES = 8
HALO = 16
VMEM_LIMIT = 56 * 1024 * 1024
ROW_CHUNK = 64
PROJ_BLOCK = 1024

BF16 = jnp.bfloat16
F32 = jnp.float32


def _params(*sem):
    return pltpu.CompilerParams(dimension_semantics=sem, vmem_limit_bytes=VMEM_LIMIT)


def _dot(a, b):
    return jnp.dot(a, b, preferred_element_type=F32)


def _dot_nt(a, b):
    return lax.dot_general(a, b, (((1,), (1,)), ((), ())), preferred_element_type=F32)


def _gelu_tanh(x):
    return 0.5 * x * (1.0 + jnp.tanh(math.sqrt(2.0 / math.pi) * (x + 0.044715 * (x * x * x))))


def _sigmoid(x):
    return 0.5 * (1.0 + jnp.tanh(0.5 * x))


def _mod_kernel(cl_ref, cc_ref, w_ref, b_ref, o_ref, sl_ref, sc_ref):
    @pl.when((pl.program_id(0) == 0) & (pl.program_id(1) == 0))
    def _():
        cl = cl_ref[...]
        cc = cc_ref[...]
        sl_ref[...] = cl * _sigmoid(cl)
        sc_ref[...] = cc * _sigmoid(cc)

    tn = w_ref.shape[1]
    reps = tn // LANES

    def body(kg, carry):
        al, ac = carry
        r0 = pl.multiple_of(kg * SUBLANES, SUBLANES)
        w8 = w_ref[pl.ds(r0, SUBLANES), :]
        s8l = jnp.tile(sl_ref[pl.ds(r0, SUBLANES), :], (1, reps))
        s8c = jnp.tile(sc_ref[pl.ds(r0, SUBLANES), :], (1, reps))
        return al + w8 * s8l, ac + w8 * s8c

    zero = jnp.zeros((SUBLANES, tn), F32)
    al, ac = lax.fori_loop(0, w_ref.shape[0] // SUBLANES, body, (zero, zero), unroll=4)
    b = b_ref[...]
    o_ref[0] = jnp.sum(al, axis=0, keepdims=True) + b
    o_ref[1] = jnp.sum(ac, axis=0, keepdims=True) + b


def _modulation(c, c_ctx, w_mod, b_mod):
    d = D_MODEL
    n = N_MOD * d
    tn = 1024
    cl = jnp.broadcast_to(c.reshape(d, 1), (d, LANES))
    cc = jnp.broadcast_to(c_ctx.reshape(d, 1), (d, LANES))
    return pl.pallas_call(
        _mod_kernel,
        grid=(DEPTH, n // tn),
        in_specs=[
            pl.BlockSpec((d, LANES), lambda l, j: (0, 0)),
            pl.BlockSpec((d, LANES), lambda l, j: (0, 0)),
            pl.BlockSpec((None, d, tn), lambda l, j: (l, 0, j)),
            pl.BlockSpec((None, 1, tn), lambda l, j: (l, 0, j)),
        ],
        out_specs=pl.BlockSpec((None, 2, 1, tn), lambda l, j: (l, 0, 0, j)),
        out_shape=jax.ShapeDtypeStruct((DEPTH, 2, 1, n), F32),
        scratch_shapes=[pltpu.VMEM((d, LANES), F32), pltpu.VMEM((d, LANES), F32)],
        compiler_params=_params("arbitrary", "arbitrary"),
        name="modulation",
    )(cl, cc, w_mod, b_mod.reshape(DEPTH, 1, n))


def _mod_spec(layer, row, k, tn, col_of):
    per = D_MODEL // tn
    return pl.BlockSpec((None, None, 1, tn), lambda *g: (layer, row, 0, k * per + col_of(*g)))


NORM_ROWS = 16


def _norm_mod_kernel(x_ref, g_ref, sh_ref, sc_ref, o_ref):
    g = g_ref[...]
    gain = g + g * sc_ref[...]
    shift = sh_ref[...]
    for r0 in range(0, x_ref.shape[0], NORM_ROWS):
        x = x_ref[r0:r0 + NORM_ROWS, :]
        r = lax.rsqrt(jnp.mean(x * x, axis=-1, keepdims=True) + EPS)
        o_ref[r0:r0 + NORM_ROWS, :] = ((x * r) * gain + shift).astype(o_ref.dtype)


def _norm_mod(x, g, layer, mods, row, k_shift, k_scale):
    m, d = x.shape
    tm = min(m, 512)
    zero = lambda i: 0
    return pl.pallas_call(
        _norm_mod_kernel,
        grid=(m // tm,),
        in_specs=[
            pl.BlockSpec((tm, d), lambda i: (i, 0)),
            pl.BlockSpec((None, 1, d), lambda i: (layer, 0, 0)),
            _mod_spec(layer, row, k_shift, d, zero),
            _mod_spec(layer, row, k_scale, d, zero),
        ],
        out_specs=pl.BlockSpec((tm, d), lambda i: (i, 0)),
        out_shape=jax.ShapeDtypeStruct((m, d), BF16),
        compiler_params=_params("arbitrary"),
        name="norm_mod",
    )(x, g, mods, mods)


def _cast_weights(w_refs, wb_ref):
    @pl.when(pl.program_id(1) == 0)
    def _():
        c0 = 0
        for w_ref in w_refs:
            wn = w_ref.shape[1]
            wb_ref[:, c0:c0 + wn] = w_ref[...].astype(BF16)
            c0 += wn


def _proj_plain_kernel(n_w, blk, h_ref, *refs):
    o_ref, wb_ref = refs[n_w:]
    _cast_weights(refs[:n_w], wb_ref)
    o_ref[...] = _dot(h_ref[...], wb_ref[...]).astype(o_ref.dtype)


def _blocked_dot(lhs_block, rows, blk, epilogue):
    blk = min(rows, blk)
    for r0 in range(0, rows, blk):
        acc = lhs_block(r0, blk)
        for r in range(0, blk, ROW_CHUNK):
            epilogue(acc[r:r + ROW_CHUNK, :], r0 + r)


def _proj_gelu_kernel(n_w, blk, h_ref, *refs):
    o_ref, wb_ref = refs[n_w:]
    _cast_weights(refs[:n_w], wb_ref)

    def epilogue(acc, r):
        o_ref[r:r + ROW_CHUNK, :] = _gelu_tanh(acc).astype(o_ref.dtype)

    _blocked_dot(lambda r0, n: _dot(h_ref[r0:r0 + n, :], wb_ref[...]), h_ref.shape[0], blk, epilogue)


def _swap16(x, even):
    return jnp.where(even, pltpu.roll(x, LANES - 16, axis=1), pltpu.roll(x, 16, axis=1))


def _proj_qkv_kernel(rope, n_w, blk, h_ref, *refs):
    if rope:
        cos_ref, sin_ref, o_ref, wb_ref = refs[n_w:]
    else:
        o_ref, wb_ref = refs[n_w:]
    _cast_weights(refs[:n_w], wb_ref)
    q_scale = HEAD_DIM ** -0.5
    n_rot = (ATT_WIDTH + KV_WIDTH) // LANES
    even = (lax.broadcasted_iota(jnp.int32, (ROW_CHUNK, LANES), 1) & 16) == 0

    def epilogue(acc, r):
        if rope:
            cos = cos_ref[r:r + ROW_CHUNK, :]
            sin = sin_ref[r:r + ROW_CHUNK, :]
        for cidx in range(acc.shape[1] // LANES):
            t = acc[:, cidx * LANES:(cidx + 1) * LANES]
            if cidx < ATT_WIDTH // LANES:
                t = t * q_scale
            if rope and cidx < n_rot:
                t = t * cos + _swap16(t, even) * sin
            o_ref[r:r + ROW_CHUNK, cidx * LANES:(cidx + 1) * LANES] = t.astype(o_ref.dtype)

    _blocked_dot(lambda r0, n: _dot(h_ref[r0:r0 + n, :], wb_ref[...]), h_ref.shape[0], blk, epilogue)


def _proj(kind, h, w, w_index, col0, n, tn, wtn, out_dtype, tm=None, extra=(), blk=PROJ_BLOCK):
    m, k = h.shape
    tm = tm or min(m, 1024)
    assert col0 % wtn == 0 and tn % wtn == 0 and n % tn == 0 and m % tm == 0
    n_w = tn // wtn
    kernels = {
        "plain": _proj_plain_kernel,
        "gelu": _proj_gelu_kernel,
        "qkv": functools.partial(_proj_qkv_kernel, False),
        "qkv_rope": functools.partial(_proj_qkv_kernel, True),
    }
    in_specs = [pl.BlockSpec((tm, k), lambda j, i: (i, 0))]
    for p in range(n_w):
        in_specs.append(pl.BlockSpec((None, k, wtn),
                                     lambda j, i, p=p: (w_index, 0, col0 // wtn + j * n_w + p)))
    in_specs += [pl.BlockSpec((tm, LANES), lambda j, i: (i, 0)) for _ in extra]
    return pl.pallas_call(
        functools.partial(kernels[kind], n_w, blk),
        grid=(n // tn, m // tm),
        in_specs=in_specs,
        out_specs=pl.BlockSpec((tm, tn), lambda j, i: (i, j)),
        out_shape=jax.ShapeDtypeStruct((m, n), out_dtype),
        scratch_shapes=[pltpu.VMEM((k, tn), BF16)],
        compiler_params=_params("arbitrary", "arbitrary"),
        name="proj_" + kind,
    )(h, *([w] * n_w), *extra)


def _proj_res_kernel(n_parts, n_groups, n_lat, *refs):
    per = n_parts + 2
    groups = [refs[g * per:(g + 1) * per] for g in range(n_groups)]
    w_ref = refs[n_groups * per]
    o_refs = refs[n_groups * per + 1:n_groups * per + 1 + n_groups]
    wb_ref = refs[-1]
    _cast_weights([w_ref], wb_ref)

    def tile(group, o_ref):
        a_refs, (x_ref, gt_ref) = group[:n_parts], group[n_parts:]

        def lhs_block(r0, n):
            acc = None
            k0 = 0
            for a_ref in a_refs:
                kp = a_ref.shape[1]
                part = _dot(a_ref[r0:r0 + n, :], wb_ref[k0:k0 + kp, :])
                acc = part if acc is None else acc + part
                k0 += kp
            return acc

        def epilogue(acc, r):
            o_ref[r:r + ROW_CHUNK, :] = x_ref[r:r + ROW_CHUNK, :] + gt_ref[...] * acc

        _blocked_dot(lhs_block, x_ref.shape[0], PROJ_BLOCK, epilogue)

    i = pl.program_id(1)
    if n_groups == 1:
        tile(groups[0], o_refs[0])
    else:
        @pl.when(i < n_lat)
        def _():
            tile(groups[0], o_refs[0])

        @pl.when(i == n_lat)
        def _():
            tile(groups[1], o_refs[1])


def _proj_residual(parts, w, w_index, x, layer, mods, k_gate, tm, tn, parts_ctx=None, x_ctx=None):
    m, n = x.shape
    k = sum(p.shape[1] for p in parts)
    tm = min(m, tm)
    n_lat = m // tm
    lat_row = lambda j, i: jnp.minimum(i, n_lat - 1)
    in_specs = [pl.BlockSpec((tm, p.shape[1]), lambda j, i: (lat_row(j, i), 0)) for p in parts]
    in_specs += [pl.BlockSpec((tm, tn), lambda j, i: (lat_row(j, i), j)),
                 _mod_spec(layer, 0, k_gate, tn, lambda j, i: j)]
    args = [*parts, x, mods]
    out_specs = [pl.BlockSpec((tm, tn), lambda j, i: (lat_row(j, i), j))]
    out_shape = [jax.ShapeDtypeStruct((m, n), F32)]
    n_groups = 1
    if parts_ctx is not None:
        mc = x_ctx.shape[0]
        in_specs += [pl.BlockSpec((mc, p.shape[1]), lambda j, i: (0, 0)) for p in parts_ctx]
        in_specs += [pl.BlockSpec((mc, tn), lambda j, i: (0, j)),
                     _mod_spec(layer, 1, k_gate, tn, lambda j, i: j)]
        args += [*parts_ctx, x_ctx, mods]
        out_specs.append(pl.BlockSpec((mc, tn), lambda j, i: (0, j)))
        out_shape.append(jax.ShapeDtypeStruct((mc, n), F32))
        n_groups = 2
    in_specs.append(pl.BlockSpec((None, k, tn), lambda j, i: (w_index, 0, j)))
    args.append(w)
    outs = pl.pallas_call(
        functools.partial(_proj_res_kernel, len(parts), n_groups, n_lat),
        grid=(n // tn, n_lat + n_groups - 1),
        in_specs=in_specs,
        out_specs=out_specs,
        out_shape=out_shape,
        scratch_shapes=[pltpu.VMEM((k, tn), BF16)],
        compiler_params=_params("arbitrary", "arbitrary"),
        name="proj_residual",
    )(*args)
    return outs if n_groups == 2 else (outs[0], None)


def _conv_proj_kernel(n_w, kw, left, gated, n_lat, with_ctx, h_ref, hp_ref, hn_ref, *refs):
    if with_ctx:
        hc_ref, refs = refs[0], refs[1:]
    w_refs = refs[:n_w]
    cw_refs = refs[n_w:2 * n_w]
    cb_refs = refs[2 * n_w:3 * n_w]
    rest = refs[3 * n_w:]
    o_ref = rest[0]
    oc_ref = rest[1] if with_ctx else None
    wb_ref, ext_ref, z_ref = rest[-3:]
    i = pl.program_id(1)
    tn = w_refs[0].shape[1]
    out_slabs = tn // LANES

    @pl.when(i == 0)
    def _():
        for p, w_ref in enumerate(w_refs):
            wb_ref[:, p * tn:(p + 1) * tn] = w_ref[...].astype(BF16)

    def tile(src_ref, halo_prev, halo_next, dst_ref):
        tm = src_ref.shape[0]
        rows = tm + 2 * HALO
        ext_ref[0:HALO, :] = halo_prev
        ext_ref[HALO:HALO + tm, :] = src_ref[...]
        ext_ref[HALO + tm:rows, :] = halo_next
        z = _dot(ext_ref[0:rows, :], wb_ref[...])
        for t in range(n_w * out_slabs):
            z_ref[t, 0:rows, :] = z[:, t * LANES:(t + 1) * LANES]
        chunk = min(tm, ROW_CHUNK)
        for so in range(out_slabs):
            c0 = so * LANES
            for r in range(0, tm, chunk):
                outs = []
                for p in range(n_w):
                    u = cb_refs[p][:, c0:c0 + LANES]
                    for tap in range(kw):
                        r0 = HALO - left + tap + r
                        u = u + z_ref[p * out_slabs + so, r0:r0 + chunk, :] * cw_refs[p][tap:tap + 1, c0:c0 + LANES]
                    outs.append(u)
                if gated:
                    g, v = outs
                    res = g * _sigmoid(g) * v
                else:
                    res = outs[0]
                dst_ref[r:r + chunk, c0:c0 + LANES] = res.astype(dst_ref.dtype)

    zero_halo = jnp.zeros(hp_ref.shape, BF16)

    def latent_tile():
        tile(h_ref, jnp.where(i > 0, hp_ref[...], zero_halo),
             jnp.where(i < n_lat - 1, hn_ref[...], zero_halo), o_ref)

    if with_ctx:
        pl.when(i < n_lat)(latent_tile)

        @pl.when(i == n_lat)
        def _():
            tile(hc_ref, zero_halo, zero_halo, oc_ref)
    else:
        latent_tile()


def _conv_proj(h, w, w_index, col_blocks, conv_blocks, n_out, cw, cb, kw, left, gated, tm, tn, out_dtype,
               h_ctx=None):
    m, k = h.shape
    tm = min(m, tm)
    n_w = len(col_blocks)
    hb = m // HALO
    tb = tm // HALO
    n_lat = m // tm
    with_ctx = h_ctx is not None
    row = lambda j, i: jnp.minimum(i, n_lat - 1)
    in_specs = [
        pl.BlockSpec((tm, k), lambda j, i: (row(j, i), 0)),
        pl.BlockSpec((HALO, k), lambda j, i: (jnp.maximum(row(j, i) * tb - 1, 0), 0)),
        pl.BlockSpec((HALO, k), lambda j, i: (jnp.minimum((row(j, i) + 1) * tb, hb - 1), 0)),
    ]
    args = [h, h, h]
    out_specs = [pl.BlockSpec((tm, tn), lambda j, i: (row(j, i), j))]
    out_shape = [jax.ShapeDtypeStruct((m, n_out), out_dtype)]
    if with_ctx:
        mc = h_ctx.shape[0]
        in_specs.append(pl.BlockSpec((mc, k), lambda j, i: (0, 0)))
        args.append(h_ctx)
        out_specs.append(pl.BlockSpec((mc, tn), lambda j, i: (0, j)))
        out_shape.append(jax.ShapeDtypeStruct((mc, n_out), out_dtype))
    for c0 in col_blocks:
        in_specs.append(pl.BlockSpec((None, k, tn), lambda j, i, c0=c0: (w_index, 0, c0 + j)))
    for c0 in conv_blocks:
        in_specs.append(pl.BlockSpec((None, kw, tn), lambda j, i, c0=c0: (w_index, 0, c0 + j)))
    for c0 in conv_blocks:
        in_specs.append(pl.BlockSpec((None, 1, tn), lambda j, i, c0=c0: (w_index, 0, c0 + j)))
    args += [w] * n_w + [cw] * n_w + [cb] * n_w
    ext_rows = tm + 2 * HALO
    outs = pl.pallas_call(
        functools.partial(_conv_proj_kernel, n_w, kw, left, gated, n_lat, with_ctx),
        grid=(n_out // tn, n_lat + (1 if with_ctx else 0)),
        in_specs=in_specs,
        out_specs=out_specs,
        out_shape=out_shape,
        scratch_shapes=[pltpu.VMEM((k, n_w * tn), BF16), pltpu.VMEM((ext_rows, k), BF16),
                        pltpu.VMEM((n_w * tn // LANES, ext_rows, LANES), F32)],
        compiler_params=_params("arbitrary", "arbitrary"),
        name="conv_proj",
    )(*args)
    return outs if with_ctx else (outs[0], None)


def _pair_operand(band, kv_head):
    b = band.astype(F32)
    rolled = pltpu.roll(b, HEAD_DIM, axis=1)
    low = lax.broadcasted_iota(jnp.int32, b.shape, 1) < HEAD_DIM
    zero = jnp.zeros_like(b)
    if kv_head == 0:
        top = jnp.where(low, b, zero)
        bot = jnp.where(low, zero, rolled)
    else:
        top = jnp.where(low, rolled, zero)
        bot = jnp.where(low, zero, b)
    return jnp.concatenate([top, bot], axis=0).astype(BF16)


def _attention_core(q_ref, k_band, v_band, biases, sink_ref, fa_index, o_ref):
    tq = q_ref.shape[0]
    nk = k_band.shape[0]
    low = lax.broadcasted_iota(jnp.int32, (tq, LANES), 1) < HEAD_DIM
    pairs = N_HEADS // N_KV_HEADS // 2
    for kv_head in range(N_KV_HEADS):
        k2 = _pair_operand(k_band, kv_head)
        v2 = _pair_operand(v_band, kv_head)
        col0 = [(kv_head * pairs + pair) * LANES for pair in range(pairs)]
        q_rows = jnp.concatenate([q_ref[:, c0:c0 + LANES] for c0 in col0], axis=0)
        s_all = _dot_nt(q_rows, k2)
        p_rows, inv_rows = [], []
        for pair, c0 in enumerate(col0):
            s = s_all[pair * tq:(pair + 1) * tq, :]
            ps, inv = [], []
            for half in range(2):
                sink = sink_ref[fa_index, c0 // HEAD_DIM + half]
                cols = []
                for j in range(nk // LANES):
                    blk = s[:, half * nk + j * LANES:half * nk + (j + 1) * LANES]
                    cols.append(blk + biases[j] if j in biases else blk)
                top = cols[0]
                for blk in cols[1:]:
                    top = jnp.maximum(top, blk)
                mx = jnp.maximum(jnp.max(top, axis=1, keepdims=True), sink)
                tot = None
                for blk in cols:
                    p = jnp.exp(blk - mx)
                    tot = p if tot is None else tot + p
                    ps.append(p.astype(BF16))
                den = jnp.sum(tot, axis=1, keepdims=True) + jnp.exp(sink - mx)
                inv.append(1.0 / den)
            p_rows.append(jnp.concatenate(ps, axis=1))
            inv_rows.append(jnp.where(low, inv[0], inv[1]))
        o_all = _dot(jnp.concatenate(p_rows, axis=0), v2)
        for pair, c0 in enumerate(col0):
            o = o_all[pair * tq:(pair + 1) * tq, :] * inv_rows[pair]
            o_ref[:, c0:c0 + LANES] = o.astype(o_ref.dtype)


def _attn_lat_kernel(fa_index, sink_ref, q_ref, kp_ref, kc_ref, kn_ref, vp_ref, vc_ref, vn_ref,
                     kx_ref, vx_ref, o_ref):
    n = pl.program_id(0)
    nb = pl.num_programs(0)
    k_band = jnp.concatenate([kp_ref[...], kc_ref[...], kn_ref[...], kx_ref[...]], axis=0)
    v_band = jnp.concatenate([vp_ref[...], vc_ref[...], vn_ref[...], vx_ref[...]], axis=0)
    qi = lax.broadcasted_iota(jnp.int32, (BLOCK, BLOCK), 0)
    kj = lax.broadcasted_iota(jnp.int32, (BLOCK, BLOCK), 1)
    bias_prev = jnp.where((kj >= qi) & (n > 0), 0.0, NEG_INF).astype(F32)
    bias_next = jnp.where((kj <= qi) & (n < nb - 1), 0.0, NEG_INF).astype(F32)
    _attention_core(q_ref, k_band, v_band, {0: bias_prev, 2: bias_next}, sink_ref, fa_index, o_ref)


def _attn_ctx_kernel(fa_index, sink_ref, q_ref, kx_ref, vx_ref, o_ref):
    _attention_core(q_ref, kx_ref[...], vx_ref[...], {}, sink_ref, fa_index, o_ref)


def _attention_lat(qkv, qkv_ctx, sink, fa_index):
    s = qkv.shape[0]
    n_ctx = qkv_ctx.shape[0]
    nb = s // BLOCK
    kcol = ATT_WIDTH // LANES
    vcol = kcol + 1
    prev = lambda n: jnp.maximum(n - 1, 0)
    nxt = lambda n: jnp.minimum(n + 1, nb - 1)
    in_specs = [
        pl.BlockSpec(memory_space=pltpu.SMEM),
        pl.BlockSpec((BLOCK, ATT_WIDTH), lambda n: (n, 0)),
        pl.BlockSpec((BLOCK, LANES), lambda n: (prev(n), kcol)),
        pl.BlockSpec((BLOCK, LANES), lambda n: (n, kcol)),
        pl.BlockSpec((BLOCK, LANES), lambda n: (nxt(n), kcol)),
        pl.BlockSpec((BLOCK, LANES), lambda n: (prev(n), vcol)),
        pl.BlockSpec((BLOCK, LANES), lambda n: (n, vcol)),
        pl.BlockSpec((BLOCK, LANES), lambda n: (nxt(n), vcol)),
        pl.BlockSpec((n_ctx, LANES), lambda n: (0, kcol)),
        pl.BlockSpec((n_ctx, LANES), lambda n: (0, vcol)),
    ]
    return pl.pallas_call(
        functools.partial(_attn_lat_kernel, fa_index),
        grid=(nb,),
        in_specs=in_specs,
        out_specs=pl.BlockSpec((BLOCK, ATT_WIDTH), lambda n: (n, 0)),
        out_shape=jax.ShapeDtypeStruct((s, ATT_WIDTH), BF16),
        compiler_params=_params("arbitrary"),
        name="attention_latent",
    )(sink, qkv, qkv, qkv, qkv, qkv, qkv, qkv, qkv_ctx, qkv_ctx)


def _attention_ctx(qkv_ctx, sink, fa_index):
    n_ctx = qkv_ctx.shape[0]
    kcol = ATT_WIDTH // LANES
    return pl.pallas_call(
        functools.partial(_attn_ctx_kernel, fa_index),
        grid=(1,),
        in_specs=[
            pl.BlockSpec(memory_space=pltpu.SMEM),
            pl.BlockSpec((n_ctx, ATT_WIDTH), lambda n: (0, 0)),
            pl.BlockSpec((n_ctx, LANES), lambda n: (0, kcol)),
            pl.BlockSpec((n_ctx, LANES), lambda n: (0, kcol + 1)),
        ],
        out_specs=pl.BlockSpec((n_ctx, ATT_WIDTH), lambda n: (0, 0)),
        out_shape=jax.ShapeDtypeStruct((n_ctx, ATT_WIDTH), BF16),
        compiler_params=_params("arbitrary"),
        name="attention_context",
    )(sink, qkv_ctx, qkv_ctx, qkv_ctx)


FFT_N1 = 64
FFT_N2 = 128


def _dft_cos_sin(n):
    idx = np.arange(n)
    ang = 2.0 * np.pi * ((idx[:, None] * idx[None, :]) % n) / n
    return np.cos(ang), np.sin(ang)


def _fourier_constants():
    c1, s1 = _dft_cos_sin(FFT_N1)
    stage1 = np.concatenate([c1, -s1], axis=0)
    c2, s2 = _dft_cos_sin(FFT_N2)
    stage2 = np.block([[c2, s2], [-s2, c2]])
    cc, sc = _dft_cos_sin(FG_W)
    chan = np.concatenate([cc, sc], axis=0)
    return (jnp.asarray(stage1, F32), jnp.asarray(stage2, F32), jnp.asarray(chan, F32))


def _twiddle_tables():
    k1 = jnp.arange(FFT_N1, dtype=jnp.int32)[:, None]
    n2 = jnp.arange(FFT_N2, dtype=jnp.int32)[None, :]
    ang = ((k1 * n2) % (FFT_N1 * FFT_N2)).astype(F32) * (2.0 * math.pi / (FFT_N1 * FFT_N2))
    wr = jnp.repeat(jnp.cos(ang), LANES, axis=1)
    wi = jnp.repeat(-jnp.sin(ang), LANES, axis=1)
    return wr, wi


def _fourier_stage1_kernel(x_ref, m_ref, wr_ref, wi_ref, t_ref):
    y = _dot(m_ref[...].astype(BF16), x_ref[...].astype(BF16))
    reps = F_WIDTH // LANES
    for b in range(x_ref.shape[1] // F_WIDTH):
        cols = slice(b * F_WIDTH, (b + 1) * F_WIDTH)
        yr = y[:FFT_N1, cols]
        yi = y[FFT_N1:, cols]
        wr = jnp.tile(wr_ref[:, b * LANES:(b + 1) * LANES], (1, reps))
        wi = jnp.tile(wi_ref[:, b * LANES:(b + 1) * LANES], (1, reps))
        t_ref[:, 2 * b * F_WIDTH:(2 * b + 1) * F_WIDTH] = (yr * wr - yi * wi).astype(t_ref.dtype)
        t_ref[:, (2 * b + 1) * F_WIDTH:(2 * b + 2) * F_WIDTH] = (yr * wi + yi * wr).astype(t_ref.dtype)


def _channel_stage(p, chan_ref, scale, o_ref):
    r = p.shape[0] // 2
    pr = p[:r].astype(BF16)
    pi = p[r:].astype(BF16)
    chan_c = chan_ref[:FG_W, :].astype(BF16)
    chan_s = chan_ref[FG_W:, :].astype(BF16)
    for g in range(N_FG):
        cols = slice(g * FG_W, (g + 1) * FG_W)
        y = _dot(pr[:, cols], chan_c) + _dot(pi[:, cols], chan_s)
        o_ref[:, cols] = (y * scale).astype(o_ref.dtype)


def _fourier_stage2_kernel(scale, t_ref, m_ref, chan_ref, o_ref):
    m = m_ref[...].astype(BF16)
    for kk in range(t_ref.shape[0]):
        t = jnp.concatenate([t_ref[kk, :, :F_WIDTH], t_ref[kk, :, F_WIDTH:]], axis=0)
        _channel_stage(_dot(m, t), chan_ref, scale, o_ref.at[:, kk * F_WIDTH:(kk + 1) * F_WIDTH])


def _fourier_ctx_kernel(scale, x_ref, m_ref, chan_ref, o_ref):
    _channel_stage(_dot(m_ref[...].astype(BF16), x_ref[...].astype(BF16)), chan_ref, scale, o_ref)


def _fourier_lat(f, consts, twiddles):
    n = f.shape[0]
    assert n == FFT_N1 * FFT_N2
    stage1, stage2, chan = consts
    wr, wi = twiddles
    n2_blk = 8
    cols = n2_blk * F_WIDTH
    full = lambda a: pl.BlockSpec(a.shape, lambda j: (0,) * a.ndim)
    t = pl.pallas_call(
        _fourier_stage1_kernel,
        grid=(FFT_N2 // n2_blk,),
        in_specs=[
            pl.BlockSpec((FFT_N1, cols), lambda j: (0, j)),
            full(stage1),
            pl.BlockSpec((FFT_N1, n2_blk * LANES), lambda j: (0, j)),
            pl.BlockSpec((FFT_N1, n2_blk * LANES), lambda j: (0, j)),
        ],
        out_specs=pl.BlockSpec((FFT_N1, 2 * cols), lambda j: (0, j)),
        out_shape=jax.ShapeDtypeStruct((FFT_N1, FFT_N2 * 2 * F_WIDTH), BF16),
        compiler_params=_params("arbitrary"),
        name="fourier_stage1",
    )(f.reshape(FFT_N1, FFT_N2 * F_WIDTH), stage1, wr, wi)
    t = t.reshape(FFT_N1, FFT_N2, 2 * F_WIDTH)
    scale = 1.0 / math.sqrt(n * FG_W)
    k1_blk = 4
    out = pl.pallas_call(
        functools.partial(_fourier_stage2_kernel, scale),
        grid=(FFT_N1 // k1_blk,),
        in_specs=[
            pl.BlockSpec((k1_blk, FFT_N2, 2 * F_WIDTH), lambda k1: (k1, 0, 0)),
            full(stage2),
            full(chan),
        ],
        out_specs=pl.BlockSpec((FFT_N2, k1_blk * F_WIDTH), lambda k1: (0, k1)),
        out_shape=jax.ShapeDtypeStruct((FFT_N2, FFT_N1 * F_WIDTH), BF16),
        compiler_params=_params("arbitrary"),
        name="fourier_stage2",
    )(t, stage2, chan)
    return out.reshape(n, F_WIDTH)


def _fourier_ctx(f, chan):
    n = f.shape[0]
    c, s = _dft_cos_sin(n)
    m = jnp.asarray(np.concatenate([c, -s], axis=0), F32)
    full = lambda a: pl.BlockSpec(a.shape, lambda j: (0,) * a.ndim)
    return pl.pallas_call(
        functools.partial(_fourier_ctx_kernel, 1.0 / math.sqrt(n * FG_W)),
        grid=(1,),
        in_specs=[full(f), full(m), full(chan)],
        out_specs=pl.BlockSpec((n, F_WIDTH), lambda j: (0, 0)),
        out_shape=jax.ShapeDtypeStruct((n, F_WIDTH), BF16),
        compiler_params=_params("arbitrary"),
        name="fourier_context",
    )(f, m, chan)


def _scan8(a, b, row, reverse):
    for d in (1, 2, 4):
        shift = (SUBLANES - d) if reverse else d
        keep = (row < SUBLANES - d) if reverse else (row >= d)
        b = jnp.where(keep, b + a * pltpu.roll(b, shift, axis=0), b)
        a = jnp.where(keep, a * pltpu.roll(a, shift, axis=0), a)
    return a, b


def _rows(x, n):
    return jnp.broadcast_to(x, (n, x.shape[1]))


def _rglru_kernel(reverse, combine, xs_ref, wa_ref, wi_ref, ba_ref, bi_ref, lam_ref, h0_ref, *refs):
    if combine:
        hb_ref, gg_ref, o_ref, last_ref, w_scr, a_scr, b_scr, c_scr, carry_scr = refs
    else:
        o_ref, last_ref, w_scr, a_scr, b_scr, c_scr, carry_scr = refs
    t = pl.program_id(1)
    tt, c = xs_ref.shape
    n_slab = c // LANES
    groups = tt // SUBLANES
    blocks = groups // SUBLANES
    row = lax.broadcasted_iota(jnp.int32, (SUBLANES, LANES), 0)
    edge = 0 if reverse else SUBLANES - 1

    @pl.when(t == 0)
    def _():
        carry_scr[...] = jnp.broadcast_to(h0_ref[...], carry_scr.shape)
        w_scr[:, :c] = (0.5 * wa_ref[...]).astype(BF16)
        w_scr[:, c:] = (0.5 * wi_ref[...]).astype(BF16)

    pre = _dot(xs_ref[...].astype(BF16), w_scr[...])
    neg_lam = -lam_ref[...]
    softplus = jnp.maximum(neg_lam, 0.0) + jnp.log1p(jnp.exp(-jnp.abs(neg_lam)))
    k_all = softplus * (-0.5 * LRU_C * math.log2(math.e))
    ba_all = 0.5 * ba_ref[...]
    bi_all = 0.5 * bi_ref[...]

    for s in range(n_slab):
        lanes = slice(s * LANES, (s + 1) * LANES)
        k = _rows(k_all[:, lanes], SUBLANES)
        ba = _rows(ba_all[:, lanes], SUBLANES)
        bi = _rows(bi_all[:, lanes], SUBLANES)
        for g in range(groups):
            r0 = g * SUBLANES
            tr = jnp.tanh(pre[r0:r0 + SUBLANES, s * LANES:(s + 1) * LANES] + ba)
            ti = jnp.tanh(pre[r0:r0 + SUBLANES, c + s * LANES:c + (s + 1) * LANES] + bi)
            a = jnp.exp2((1.0 + tr) * k)
            y = 1.0 - a * a
            root = jnp.where(y > 0.0, y * lax.rsqrt(y), 0.0)
            b = root * ((1.0 + ti) * (0.5 * xs_ref[r0:r0 + SUBLANES, lanes]))
            a_cum, b_loc = _scan8(a, b, row, reverse)
            a_scr[s, r0:r0 + SUBLANES, :] = a_cum
            b_scr[s, r0:r0 + SUBLANES, :] = b_loc

    for s in range(n_slab):
        a2 = a_scr[s, pl.ds(edge, groups, stride=SUBLANES), :]
        b2 = b_scr[s, pl.ds(edge, groups, stride=SUBLANES), :]
        carry = carry_scr[:, s * LANES:(s + 1) * LANES]
        enter_row = groups if reverse else SUBLANES - 1
        base = 0 if reverse else SUBLANES
        c_scr[s, enter_row:enter_row + 1, :] = carry[0:1, :]
        for j in (range(blocks - 1, -1, -1) if reverse else range(blocks)):
            r0 = j * SUBLANES
            a_cum, b_loc = _scan8(a2[r0:r0 + SUBLANES, :], b2[r0:r0 + SUBLANES, :], row, reverse)
            st = b_loc + a_cum * carry
            c_scr[s, base + r0:base + r0 + SUBLANES, :] = st
            carry = _rows(st[edge:edge + 1, :], SUBLANES)
        carry_scr[:, s * LANES:(s + 1) * LANES] = carry
        last_ref[:, s * LANES:(s + 1) * LANES] = carry[0:1, :]

    pair = 2 * SUBLANES
    for s in range(n_slab):
        lanes = slice(s * LANES, (s + 1) * LANES)
        for g in range(0, groups, 2):
            r0 = g * SUBLANES
            src = g + 1 if reverse else g + SUBLANES - 1
            enter = jnp.concatenate([_rows(c_scr[s, src:src + 1, :], SUBLANES),
                                     _rows(c_scr[s, src + 1:src + 2, :], SUBLANES)], axis=0)
            h = b_scr[s, r0:r0 + pair, :] + a_scr[s, r0:r0 + pair, :] * enter
            if combine:
                h = (h + hb_ref[r0:r0 + pair, lanes]) * gg_ref[r0:r0 + pair, lanes].astype(F32)
            o_ref[r0:r0 + pair, lanes] = h.astype(o_ref.dtype)


def _rglru_scan(xs, w_a, w_i, b_a, b_i, lam, rg_index, direction, h0, h_other=None, gelu_gate=None):
    m = xs.shape[0]
    tt = min(m, 1024)
    nt = m // tt
    c = RNN_BLOCK
    reverse = direction == 1
    combine = h_other is not None
    tix = (lambda t: nt - 1 - t) if reverse else (lambda t: t)
    wspec = pl.BlockSpec((None, None, None, c, c), lambda cb, t: (rg_index, direction, cb, 0, 0))
    vspec = pl.BlockSpec((None, None, 1, c), lambda cb, t: (rg_index, direction, 0, cb))
    tile = pl.BlockSpec((tt, c), lambda cb, t: (tix(t), cb))
    in_specs = [tile, wspec, wspec, vspec, vspec, vspec, pl.BlockSpec((1, c), lambda cb, t: (0, cb))]
    args = [xs, w_a, w_i, b_a, b_i, lam, h0]
    if combine:
        in_specs += [tile, tile]
        args += [h_other, gelu_gate]
    return pl.pallas_call(
        functools.partial(_rglru_kernel, reverse, combine),
        grid=(D_RNN // c, nt),
        in_specs=in_specs,
        out_specs=[tile, pl.BlockSpec((1, c), lambda cb, t: (0, cb))],
        out_shape=[jax.ShapeDtypeStruct((m, D_RNN), BF16 if combine else F32),
                   jax.ShapeDtypeStruct((1, D_RNN), F32)],
        scratch_shapes=[pltpu.VMEM((c, 2 * c), BF16),
                        pltpu.VMEM((c // LANES, tt, LANES), F32),
                        pltpu.VMEM((c // LANES, tt, LANES), F32),
                        pltpu.VMEM((c // LANES, tt // SUBLANES + 2 * SUBLANES, LANES), F32),
                        pltpu.VMEM((SUBLANES, c), F32)],
        compiler_params=_params("arbitrary", "arbitrary"),
        name="rglru_scan",
    )(*args)


def _final_norm_kernel(x_ref, g_ref, o_ref):
    g = g_ref[...]
    for r0 in range(0, x_ref.shape[0], NORM_ROWS):
        x = x_ref[r0:r0 + NORM_ROWS, :]
        r = lax.rsqrt(jnp.mean(x * x, axis=-1, keepdims=True) + EPS)
        o_ref[r0:r0 + NORM_ROWS, :] = (x * r) * g


def _final_norm(x, g):
    m, d = x.shape
    tm = 512
    return pl.pallas_call(
        _final_norm_kernel,
        grid=(m // tm,),
        in_specs=[pl.BlockSpec((tm, d), lambda i: (i, 0)), pl.BlockSpec((1, d), lambda i: (0, 0))],
        out_specs=pl.BlockSpec((tm, d), lambda i: (i, 0)),
        out_shape=jax.ShapeDtypeStruct((m, d), F32),
        compiler_params=_params("arbitrary"),
        name="final_norm",
    )(x, g.reshape(1, d))


def _rope_tables(n):
    f = HEAD_DIM // 4
    inv = ROPE_BASE ** (-jnp.arange(f, dtype=F32) / f)
    pos = jnp.arange(n, dtype=jnp.int32)
    ang_r = (pos // GRID_W).astype(F32)[:, None] * inv[None, :]
    ang_c = (pos % GRID_W).astype(F32)[:, None] * inv[None, :]
    cr, sr, cc, sc = jnp.cos(ang_r), jnp.sin(ang_r), jnp.cos(ang_c), jnp.sin(ang_c)
    cos = jnp.concatenate([cr, cr, cc, cc], axis=1)
    sin = jnp.concatenate([-sr, sr, -sc, sc], axis=1)
    reps = LANES // HEAD_DIM
    return jnp.tile(cos, (1, reps)), jnp.tile(sin, (1, reps))


def _fourier_attn_layer(layer, i, x_lat, x_ctx, h_lat, h_ctx, mods, fa_w_in, fa_w_out, attn_sink,
                        tables, ctx_out):
    rope, consts, twiddles = tables
    qkv_w = ATT_WIDTH + 2 * KV_WIDTH
    f_lat = _proj("plain", h_lat, fa_w_in, i, 0, F_WIDTH, F_WIDTH, F_WIDTH, BF16)
    qkv_lat = _proj("qkv_rope", h_lat, fa_w_in, i, F_WIDTH, qkv_w, qkv_w, 256, BF16, tm=512, extra=rope)
    qkv_ctx = _proj("qkv", h_ctx, fa_w_in, i, F_WIDTH, qkv_w, qkv_w, 256, BF16)
    fo_lat = _fourier_lat(f_lat, consts, twiddles)
    ao_lat = _attention_lat(qkv_lat, qkv_ctx, attn_sink, i)
    parts_ctx = None
    if ctx_out:
        f_ctx = _proj("plain", h_ctx, fa_w_in, i, 0, F_WIDTH, F_WIDTH, F_WIDTH, BF16)
        parts_ctx = [_fourier_ctx(f_ctx, consts[2]), _attention_ctx(qkv_ctx, attn_sink, i)]
    return _proj_residual([fo_lat, ao_lat], fa_w_out, i, x_lat, layer, mods, 2, 1024, 1024, parts_ctx, x_ctx)


def _rglru_layer(layer, i, x_lat, x_ctx, h_lat, h_ctx, mods, rg_w_in, rg_conv_w, rg_conv_b, w_a, b_a,
                 w_i, b_i, lam, rg_w_out, ctx_out):
    tn = 1024
    xcol = D_RNN // tn
    zero_state = jnp.zeros((1, D_RNN), F32)
    gate_lat = _proj("gelu", h_lat, rg_w_in, i, 0, D_RNN, tn, tn, BF16)
    xs_lat, xs_ctx = _conv_proj(h_lat, rg_w_in, i, [xcol], [0], D_RNN, rg_conv_w, rg_conv_b, CONV_W, CONV_LEFT,
                                False, 1024, tn, F32, h_ctx)
    scan = functools.partial(_rglru_scan, w_a=w_a, w_i=w_i, b_a=b_a, b_i=b_i, lam=lam, rg_index=i)
    hb_ctx, s_bwd = scan(xs_ctx, direction=1, h0=zero_state)
    parts_ctx = None
    if ctx_out:
        gate_ctx = _proj("gelu", h_ctx, rg_w_in, i, 0, D_RNN, tn, tn, BF16)
        y_ctx, s_fwd = scan(xs_ctx, direction=0, h0=zero_state, h_other=hb_ctx, gelu_gate=gate_ctx)
        parts_ctx = [y_ctx]
    else:
        _, s_fwd = scan(xs_ctx, direction=0, h0=zero_state)
    hb_lat, _ = scan(xs_lat, direction=1, h0=s_bwd)
    y_lat, _ = scan(xs_lat, direction=0, h0=s_fwd, h_other=hb_lat, gelu_gate=gate_lat)
    return _proj_residual([y_lat], rg_w_out, i, x_lat, layer, mods, 2, 1024, 1024, parts_ctx, x_ctx)


def _conv_ffn(layer, x_lat, x_ctx, g_ffn, mods, w_up, conv_w, conv_b, w_down):
    h_lat = _norm_mod(x_lat, g_ffn, layer, mods, 0, 3, 4)
    h_ctx = None if x_ctx is None else _norm_mod(x_ctx, g_ffn, layer, mods, 1, 3, 4)
    tf = 512
    blocks = [0, D_FF // tf]
    act_lat, act_ctx = _conv_proj(h_lat, w_up, layer, blocks, blocks, D_FF, conv_w, conv_b, FFN_CONV_W,
                                  FFN_CONV_LEFT, True, 1024, tf, BF16, h_ctx)
    parts_ctx = None if x_ctx is None else [act_ctx]
    return _proj_residual([act_lat], w_down, layer, x_lat, layer, mods, 5, 512, 512, parts_ctx, x_ctx)


def kernel(x, c, ctx, c_ctx, w_mod, b_mod, g_mix, g_ffn, fa_w_in, fa_w_out, attn_sink, rg_w_in, rg_conv_w,
           rg_conv_b, rg_w_a, rg_b_a, rg_w_i, rg_b_i, rg_lambda, rg_w_out, ffn_w_up, ffn_conv_w, ffn_conv_b,
           ffn_w_down, g_final):
    assert x.shape[0] == 1 and ctx.shape[0] == 1
    n = x.shape[1]
    x_lat = x[0]
    x_ctx = ctx[0]
    mods = _modulation(c, c_ctx, w_mod, b_mod)
    tables = (_rope_tables(n), _fourier_constants(), _twiddle_tables())
    g_mix3 = g_mix.reshape(DEPTH, 1, D_MODEL)
    g_ffn3 = g_ffn.reshape(DEPTH, 1, D_MODEL)
    n_rg = rg_conv_b.shape[0]
    rg_conv_b3 = rg_conv_b.reshape(n_rg, 1, D_RNN)
    rg_b_a4 = rg_b_a.reshape(n_rg, 2, 1, D_RNN)
    rg_b_i4 = rg_b_i.reshape(n_rg, 2, 1, D_RNN)
    rg_lam4 = rg_lambda.reshape(n_rg, 2, 1, D_RNN)
    ffn_conv_b3 = ffn_conv_b.reshape(DEPTH, 1, 2 * D_FF)
    for layer in range(DEPTH):
        ctx_out = layer < DEPTH - 1
        i = layer // 2
        h_lat = _norm_mod(x_lat, g_mix3, layer, mods, 0, 0, 1)
        h_ctx = _norm_mod(x_ctx, g_mix3, layer, mods, 1, 0, 1)
        if layer % 2 == 0:
            x_lat, x_ctx = _fourier_attn_layer(layer, i, x_lat, x_ctx, h_lat, h_ctx, mods, fa_w_in, fa_w_out,
                                               attn_sink, tables, ctx_out)
        else:
            x_lat, x_ctx = _rglru_layer(layer, i, x_lat, x_ctx, h_lat, h_ctx, mods, rg_w_in, rg_conv_w,
                                        rg_conv_b3, rg_w_a, rg_b_a4, rg_w_i, rg_b_i4, rg_lam4, rg_w_out,
                                        ctx_out)
        x_lat, x_ctx = _conv_ffn(layer, x_lat, x_ctx if ctx_out else None, g_ffn3, mods, ffn_w_up, ffn_conv_w,
                                 ffn_conv_b3, ffn_w_down)
    return _final_norm(x_lat, g_final)[None]
```

```python
import functools
import math

import numpy as np
import jax
import jax.numpy as jnp
from jax import lax
from jax.experimental import pallas as pl
from jax.experimental.pallas import tpu as pltpu

D_MODEL = 2048
DEPTH = 4
GRID_W = 64
N_FG = 4
FG_W = 256
F_WIDTH = N_FG * FG_W
N_HEADS = 16
N_KV_HEADS = 2
HEAD_DIM = 64
ATT_WIDTH = N_HEADS * HEAD_DIM
KV_WIDTH = N_KV_HEADS * HEAD_DIM
WINDOW = 128
BLOCK = 128
ROPE_BASE = 10000.0
D_RNN = D_MODEL
N_RNN_BLOCKS = 8
RNN_BLOCK = D_RNN // N_RNN_BLOCKS
CONV_W = 4
CONV_LEFT = 2
LRU_C = 8.0
D_FF = 5632
FFN_CONV_W = 3
FFN_CONV_LEFT = 1
N_MOD = 6
EPS = 1e-6
NEG_INF = -1e30

LANES = 128
SUBLANES = 8
HALO = 16
VMEM_LIMIT = 56 * 1024 * 1024
ROW_CHUNK = 64
PROJ_BLOCK = 1024
SCAN_BLOCKS = 2

BF16 = jnp.bfloat16
F32 = jnp.float32


def _params(*sem):
    return pltpu.CompilerParams(dimension_semantics=sem, vmem_limit_bytes=VMEM_LIMIT)


def _dot(a, b):
    return jnp.dot(a, b, preferred_element_type=F32)


def _dot_nt(a, b):
    return lax.dot_general(a, b, (((1,), (1,)), ((), ())), preferred_element_type=F32)


def _gelu_tanh(x):
    return 0.5 * x * (1.0 + jnp.tanh(math.sqrt(2.0 / math.pi) * (x + 0.044715 * (x * x * x))))


def _sigmoid(x):
    return 0.5 * (1.0 + jnp.tanh(0.5 * x))


def _mod_kernel(cl_ref, cc_ref, w_ref, b_ref, o_ref, sl_ref, sc_ref):
    @pl.when((pl.program_id(0) == 0) & (pl.program_id(1) == 0))
    def _():
        cl = cl_ref[...]
        cc = cc_ref[...]
        sl_ref[...] = cl * _sigmoid(cl)
        sc_ref[...] = cc * _sigmoid(cc)

    tn = w_ref.shape[1]
    reps = tn // LANES

    def body(kg, carry):
        al, ac = carry
        r0 = pl.multiple_of(kg * SUBLANES, SUBLANES)
        w8 = w_ref[pl.ds(r0, SUBLANES), :]
        s8l = jnp.tile(sl_ref[pl.ds(r0, SUBLANES), :], (1, reps))
        s8c = jnp.tile(sc_ref[pl.ds(r0, SUBLANES), :], (1, reps))
        return al + w8 * s8l, ac + w8 * s8c

    zero = jnp.zeros((SUBLANES, tn), F32)
    al, ac = lax.fori_loop(0, w_ref.shape[0] // SUBLANES, body, (zero, zero), unroll=4)
    b = b_ref[...]
    o_ref[0] = jnp.sum(al, axis=0, keepdims=True) + b
    o_ref[1] = jnp.sum(ac, axis=0, keepdims=True) + b


def _modulation(c, c_ctx, w_mod, b_mod):
    d = D_MODEL
    n = N_MOD * d
    tn = 1024
    cl = jnp.broadcast_to(c.reshape(d, 1), (d, LANES))
    cc = jnp.broadcast_to(c_ctx.reshape(d, 1), (d, LANES))
    return pl.pallas_call(
        _mod_kernel,
        grid=(DEPTH, n // tn),
        in_specs=[
            pl.BlockSpec((d, LANES), lambda l, j: (0, 0)),
            pl.BlockSpec((d, LANES), lambda l, j: (0, 0)),
            pl.BlockSpec((None, d, tn), lambda l, j: (l, 0, j)),
            pl.BlockSpec((None, 1, tn), lambda l, j: (l, 0, j)),
        ],
        out_specs=pl.BlockSpec((None, 2, 1, tn), lambda l, j: (l, 0, 0, j)),
        out_shape=jax.ShapeDtypeStruct((DEPTH, 2, 1, n), F32),
        scratch_shapes=[pltpu.VMEM((d, LANES), F32), pltpu.VMEM((d, LANES), F32)],
        compiler_params=_params("arbitrary", "arbitrary"),
        name="modulation",
    )(cl, cc, w_mod, b_mod.reshape(DEPTH, 1, n))


def _mod_spec(layer, row, k, tn, col_of):
    per = D_MODEL // tn
    return pl.BlockSpec((None, None, 1, tn), lambda *g: (layer, row, 0, k * per + col_of(*g)))


NORM_ROWS = 16


def _norm_mod_kernel(x_ref, g_ref, sh_ref, sc_ref, o_ref):
    g = g_ref[...]
    gain = g + g * sc_ref[...]
    shift = sh_ref[...]
    for r0 in range(0, x_ref.shape[0], NORM_ROWS):
        x = x_ref[r0:r0 + NORM_ROWS, :]
        r = lax.rsqrt(jnp.mean(x * x, axis=-1, keepdims=True) + EPS)
        o_ref[r0:r0 + NORM_ROWS, :] = ((x * r) * gain + shift).astype(o_ref.dtype)


def _norm_mod(x, g, layer, mods, row, k_shift, k_scale):
    m, d = x.shape
    tm = min(m, 1024)
    zero = lambda i: 0
    return pl.pallas_call(
        _norm_mod_kernel,
        grid=(m // tm,),
        in_specs=[
            pl.BlockSpec((tm, d), lambda i: (i, 0)),
            pl.BlockSpec((None, 1, d), lambda i: (layer, 0, 0)),
            _mod_spec(layer, row, k_shift, d, zero),
            _mod_spec(layer, row, k_scale, d, zero),
        ],
        out_specs=pl.BlockSpec((tm, d), lambda i: (i, 0)),
        out_shape=jax.ShapeDtypeStruct((m, d), BF16),
        compiler_params=_params("arbitrary"),
        name="norm_mod",
    )(x, g, mods, mods)


def _cast_weights(w_refs, wb_ref):
    @pl.when(pl.program_id(1) == 0)
    def _():
        c0 = 0
        for w_ref in w_refs:
            wn = w_ref.shape[1]
            wb_ref[:, c0:c0 + wn] = w_ref[...].astype(BF16)
            c0 += wn


def _proj_plain_kernel(n_w, blk, h_ref, *refs):
    o_ref, wb_ref = refs[n_w:]
    _cast_weights(refs[:n_w], wb_ref)
    o_ref[...] = _dot(h_ref[...], wb_ref[...]).astype(o_ref.dtype)


def _blocked_dot(lhs_block, rows, blk, epilogue):
    blk = min(rows, blk)
    for r0 in range(0, rows, blk):
        acc = lhs_block(r0, blk)
        for r in range(0, blk, ROW_CHUNK):
            epilogue(acc[r:r + ROW_CHUNK, :], r0 + r)


def _proj_gelu_kernel(n_w, blk, h_ref, *refs):
    o_ref, wb_ref = refs[n_w:]
    _cast_weights(refs[:n_w], wb_ref)

    def epilogue(acc, r):
        o_ref[r:r + ROW_CHUNK, :] = _gelu_tanh(acc).astype(o_ref.dtype)

    _blocked_dot(lambda r0, n: _dot(h_ref[r0:r0 + n, :], wb_ref[...]), h_ref.shape[0], blk, epilogue)


def _swap16(x, even):
    return jnp.where(even, pltpu.roll(x, LANES - 16, axis=1), pltpu.roll(x, 16, axis=1))


def _proj_qkv_kernel(rope, n_w, blk, h_ref, *refs):
    if rope:
        cos_ref, sin_ref, o_ref, wb_ref = refs[n_w:]
    else:
        o_ref, wb_ref = refs[n_w:]
    _cast_weights(refs[:n_w], wb_ref)
    q_scale = HEAD_DIM ** -0.5
    n_rot = (ATT_WIDTH + KV_WIDTH) // LANES
    even = (lax.broadcasted_iota(jnp.int32, (ROW_CHUNK, LANES), 1) & 16) == 0

    def epilogue(acc, r):
        if rope:
            cos = cos_ref[r:r + ROW_CHUNK, :]
            sin = sin_ref[r:r + ROW_CHUNK, :]
        for cidx in range(acc.shape[1] // LANES):
            t = acc[:, cidx * LANES:(cidx + 1) * LANES]
            if cidx < ATT_WIDTH // LANES:
                t = t * q_scale
            if rope and cidx < n_rot:
                t = t * cos + _swap16(t, even) * sin
            o_ref[r:r + ROW_CHUNK, cidx * LANES:(cidx + 1) * LANES] = t.astype(o_ref.dtype)

    _blocked_dot(lambda r0, n: _dot(h_ref[r0:r0 + n, :], wb_ref[...]), h_ref.shape[0], blk, epilogue)


def _proj(kind, h, w, w_index, col0, n, tn, wtn, out_dtype, tm=None, extra=(), blk=PROJ_BLOCK):
    m, k = h.shape
    tm = tm or min(m, 1024)
    assert col0 % wtn == 0 and tn % wtn == 0 and n % tn == 0 and m % tm == 0
    n_w = tn // wtn
    kernels = {
        "plain": _proj_plain_kernel,
        "gelu": _proj_gelu_kernel,
        "qkv": functools.partial(_proj_qkv_kernel, False),
        "qkv_rope": functools.partial(_proj_qkv_kernel, True),
    }
    in_specs = [pl.BlockSpec((tm, k), lambda j, i: (i, 0))]
    for p in range(n_w):
        in_specs.append(pl.BlockSpec((None, k, wtn),
                                     lambda j, i, p=p: (w_index, 0, col0 // wtn + j * n_w + p)))
    in_specs += [pl.BlockSpec((tm, LANES), lambda j, i: (i, 0)) for _ in extra]
    return pl.pallas_call(
        functools.partial(kernels[kind], n_w, blk),
        grid=(n // tn, m // tm),
        in_specs=in_specs,
        out_specs=pl.BlockSpec((tm, tn), lambda j, i: (i, j)),
        out_shape=jax.ShapeDtypeStruct((m, n), out_dtype),
        scratch_shapes=[pltpu.VMEM((k, tn), BF16)],
        compiler_params=_params("arbitrary", "arbitrary"),
        name="proj_" + kind,
    )(h, *([w] * n_w), *extra)


def _proj_res_kernel(n_parts, n_groups, n_lat, *refs):
    per = n_parts + 2
    groups = [refs[g * per:(g + 1) * per] for g in range(n_groups)]
    w_ref = refs[n_groups * per]
    o_refs = refs[n_groups * per + 1:n_groups * per + 1 + n_groups]
    wb_ref = refs[-1]
    _cast_weights([w_ref], wb_ref)

    def tile(group, o_ref):
        a_refs, (x_ref, gt_ref) = group[:n_parts], group[n_parts:]

        def lhs_block(r0, n):
            acc = None
            k0 = 0
            for a_ref in a_refs:
                kp = a_ref.shape[1]
                part = _dot(a_ref[r0:r0 + n, :], wb_ref[k0:k0 + kp, :])
                acc = part if acc is None else acc + part
                k0 += kp
            return acc

        def epilogue(acc, r):
            o_ref[r:r + ROW_CHUNK, :] = x_ref[r:r + ROW_CHUNK, :] + gt_ref[...] * acc

        _blocked_dot(lhs_block, x_ref.shape[0], PROJ_BLOCK, epilogue)

    i = pl.program_id(1)
    if n_groups == 1:
        tile(groups[0], o_refs[0])
    else:
        @pl.when(i < n_lat)
        def _():
            tile(groups[0], o_refs[0])

        @pl.when(i == n_lat)
        def _():
            tile(groups[1], o_refs[1])


def _proj_residual(parts, w, w_index, x, layer, mods, k_gate, tm, tn, parts_ctx=None, x_ctx=None):
    m, n = x.shape
    k = sum(p.shape[1] for p in parts)
    tm = min(m, tm)
    n_lat = m // tm
    lat_row = lambda j, i: jnp.minimum(i, n_lat - 1)
    in_specs = [pl.BlockSpec((tm, p.shape[1]), lambda j, i: (lat_row(j, i), 0)) for p in parts]
    in_specs += [pl.BlockSpec((tm, tn), lambda j, i: (lat_row(j, i), j)),
                 _mod_spec(layer, 0, k_gate, tn, lambda j, i: j)]
    args = [*parts, x, mods]
    out_specs = [pl.BlockSpec((tm, tn), lambda j, i: (lat_row(j, i), j))]
    out_shape = [jax.ShapeDtypeStruct((m, n), F32)]
    n_groups = 1
    if parts_ctx is not None:
        mc = x_ctx.shape[0]
        in_specs += [pl.BlockSpec((mc, p.shape[1]), lambda j, i: (0, 0)) for p in parts_ctx]
        in_specs += [pl.BlockSpec((mc, tn), lambda j, i: (0, j)),
                     _mod_spec(layer, 1, k_gate, tn, lambda j, i: j)]
        args += [*parts_ctx, x_ctx, mods]
        out_specs.append(pl.BlockSpec((mc, tn), lambda j, i: (0, j)))
        out_shape.append(jax.ShapeDtypeStruct((mc, n), F32))
        n_groups = 2
    in_specs.append(pl.BlockSpec((None, k, tn), lambda j, i: (w_index, 0, j)))
    args.append(w)
    outs = pl.pallas_call(
        functools.partial(_proj_res_kernel, len(parts), n_groups, n_lat),
        grid=(n // tn, n_lat + n_groups - 1),
        in_specs=in_specs,
        out_specs=out_specs,
        out_shape=out_shape,
        scratch_shapes=[pltpu.VMEM((k, tn), BF16)],
        compiler_params=_params("arbitrary", "arbitrary"),
        name="proj_residual",
    )(*args)
    return outs if n_groups == 2 else (outs[0], None)


def _conv_proj_kernel(n_w, kw, left, gated, h_ref, hp_ref, hn_ref, *refs):
    w_refs = refs[:n_w]
    cw_refs = refs[n_w:2 * n_w]
    cb_refs = refs[2 * n_w:3 * n_w]
    o_ref, wb_ref, ext_ref, z_ref = refs[3 * n_w:]
    i = pl.program_id(1)
    tm = h_ref.shape[0]
    tn = w_refs[0].shape[1]
    out_slabs = tn // LANES

    @pl.when(i == 0)
    def _():
        for p, w_ref in enumerate(w_refs):
            wb_ref[:, p * tn:(p + 1) * tn] = w_ref[...].astype(BF16)

    ext_ref[0:HALO, :] = jnp.where(i > 0, hp_ref[...], jnp.zeros_like(hp_ref))
    ext_ref[HALO:HALO + tm, :] = h_ref[...]
    ext_ref[HALO + tm:, :] = jnp.where(i < pl.num_programs(1) - 1, hn_ref[...], jnp.zeros_like(hn_ref))

    z = _dot(ext_ref[...], wb_ref[...])
    for t in range(n_w * out_slabs):
        z_ref[t] = z[:, t * LANES:(t + 1) * LANES]
    for so in range(out_slabs):
        c0 = so * LANES
        for r in range(0, tm, ROW_CHUNK):
            outs = []
            for p in range(n_w):
                u = cb_refs[p][:, c0:c0 + LANES]
                for tap in range(kw):
                    r0 = HALO - left + tap + r
                    u = u + z_ref[p * out_slabs + so, r0:r0 + ROW_CHUNK, :] * cw_refs[p][tap:tap + 1, c0:c0 + LANES]
                outs.append(u)
            if gated:
                g, v = outs
                res = g * _sigmoid(g) * v
            else:
                res = outs[0]
            o_ref[r:r + ROW_CHUNK, c0:c0 + LANES] = res.astype(o_ref.dtype)


def _conv_proj(h, w, w_index, col_blocks, conv_blocks, n_out, cw, cb, kw, left, gated, tm, tn, out_dtype):
    m, k = h.shape
    tm = min(m, tm)
    n_w = len(col_blocks)
    hb = m // HALO
    tb = tm // HALO
    in_specs = [
        pl.BlockSpec((tm, k), lambda j, i: (i, 0)),
        pl.BlockSpec((HALO, k), lambda j, i: (jnp.maximum(i * tb - 1, 0), 0)),
        pl.BlockSpec((HALO, k), lambda j, i: (jnp.minimum((i + 1) * tb, hb - 1), 0)),
    ]
    for c0 in col_blocks:
        in_specs.append(pl.BlockSpec((None, k, tn), lambda j, i, c0=c0: (w_index, 0, c0 + j)))
    for c0 in conv_blocks:
        in_specs.append(pl.BlockSpec((None, kw, tn), lambda j, i, c0=c0: (w_index, 0, c0 + j)))
    for c0 in conv_blocks:
        in_specs.append(pl.BlockSpec((None, 1, tn), lambda j, i, c0=c0: (w_index, 0, c0 + j)))
    args = [h, h, h] + [w] * n_w + [cw] * n_w + [cb] * n_w
    ext_rows = tm + 2 * HALO
    return pl.pallas_call(
        functools.partial(_conv_proj_kernel, n_w, kw, left, gated),
        grid=(n_out // tn, m // tm),
        in_specs=in_specs,
        out_specs=pl.BlockSpec((tm, tn), lambda j, i: (i, j)),
        out_shape=jax.ShapeDtypeStruct((m, n_out), out_dtype),
        scratch_shapes=[pltpu.VMEM((k, n_w * tn), BF16), pltpu.VMEM((ext_rows, k), BF16),
                        pltpu.VMEM((n_w * tn // LANES, ext_rows, LANES), F32)],
        compiler_params=_params("arbitrary", "arbitrary"),
        name="conv_proj",
    )(*args)


def _pair_operand(band, kv_head):
    b = band.astype(F32)
    rolled = pltpu.roll(b, HEAD_DIM, axis=1)
    low = lax.broadcasted_iota(jnp.int32, b.shape, 1) < HEAD_DIM
    zero = jnp.zeros_like(b)
    if kv_head == 0:
        top = jnp.where(low, b, zero)
        bot = jnp.where(low, zero, rolled)
    else:
        top = jnp.where(low, rolled, zero)
        bot = jnp.where(low, zero, b)
    return jnp.concatenate([top, bot], axis=0).astype(BF16)


def _attention_core(q_ref, k_band, v_band, biases, sink_ref, fa_index, o_ref):
    tq = q_ref.shape[0]
    nk = k_band.shape[0]
    low = lax.broadcasted_iota(jnp.int32, (tq, LANES), 1) < HEAD_DIM
    pairs = N_HEADS // N_KV_HEADS // 2
    for kv_head in range(N_KV_HEADS):
        k2 = _pair_operand(k_band, kv_head)
        v2 = _pair_operand(v_band, kv_head)
        col0 = [(kv_head * pairs + pair) * LANES for pair in range(pairs)]
        q_rows = jnp.concatenate([q_ref[:, c0:c0 + LANES] for c0 in col0], axis=0)
        s_all = _dot_nt(q_rows, k2)
        p_rows, inv_rows = [], []
        for pair, c0 in enumerate(col0):
            s = s_all[pair * tq:(pair + 1) * tq, :]
            ps, inv = [], []
            for half in range(2):
                sink = sink_ref[fa_index, c0 // HEAD_DIM + half]
                cols = []
                for j in range(nk // LANES):
                    blk = s[:, half * nk + j * LANES:half * nk + (j + 1) * LANES]
                    cols.append(blk + biases[j] if j in biases else blk)
                top = cols[0]
                for blk in cols[1:]:
                    top = jnp.maximum(top, blk)
                mx = jnp.maximum(jnp.max(top, axis=1, keepdims=True), sink)
                tot = None
                for blk in cols:
                    p = jnp.exp(blk - mx)
                    tot = p if tot is None else tot + p
                    ps.append(p.astype(BF16))
                den = jnp.sum(tot, axis=1, keepdims=True) + jnp.exp(sink - mx)
                inv.append(1.0 / den)
            p_rows.append(jnp.concatenate(ps, axis=1))
            inv_rows.append(jnp.where(low, inv[0], inv[1]))
        o_all = _dot(jnp.concatenate(p_rows, axis=0), v2)
        for pair, c0 in enumerate(col0):
            o = o_all[pair * tq:(pair + 1) * tq, :] * inv_rows[pair]
            o_ref[:, c0:c0 + LANES] = o.astype(o_ref.dtype)


def _attn_lat_kernel(fa_index, sink_ref, q_ref, kp_ref, kc_ref, kn_ref, vp_ref, vc_ref, vn_ref,
                     kx_ref, vx_ref, o_ref):
    n = pl.program_id(0)
    nb = pl.num_programs(0)
    k_band = jnp.concatenate([kp_ref[...], kc_ref[...], kn_ref[...], kx_ref[...]], axis=0)
    v_band = jnp.concatenate([vp_ref[...], vc_ref[...], vn_ref[...], vx_ref[...]], axis=0)
    qi = lax.broadcasted_iota(jnp.int32, (BLOCK, BLOCK), 0)
    kj = lax.broadcasted_iota(jnp.int32, (BLOCK, BLOCK), 1)
    bias_prev = jnp.where((kj >= qi) & (n > 0), 0.0, NEG_INF).astype(F32)
    bias_next = jnp.where((kj <= qi) & (n < nb - 1), 0.0, NEG_INF).astype(F32)
    _attention_core(q_ref, k_band, v_band, {0: bias_prev, 2: bias_next}, sink_ref, fa_index, o_ref)


def _attn_ctx_kernel(fa_index, sink_ref, q_ref, kx_ref, vx_ref, o_ref):
    _attention_core(q_ref, kx_ref[...], vx_ref[...], {}, sink_ref, fa_index, o_ref)


def _attention_lat(qkv, qkv_ctx, sink, fa_index):
    s = qkv.shape[0]
    n_ctx = qkv_ctx.shape[0]
    nb = s // BLOCK
    kcol = ATT_WIDTH // LANES
    vcol = kcol + 1
    prev = lambda n: jnp.maximum(n - 1, 0)
    nxt = lambda n: jnp.minimum(n + 1, nb - 1)
    in_specs = [
        pl.BlockSpec(memory_space=pltpu.SMEM),
        pl.BlockSpec((BLOCK, ATT_WIDTH), lambda n: (n, 0)),
        pl.BlockSpec((BLOCK, LANES), lambda n: (prev(n), kcol)),
        pl.BlockSpec((BLOCK, LANES), lambda n: (n, kcol)),
        pl.BlockSpec((BLOCK, LANES), lambda n: (nxt(n), kcol)),
        pl.BlockSpec((BLOCK, LANES), lambda n: (prev(n), vcol)),
        pl.BlockSpec((BLOCK, LANES), lambda n: (n, vcol)),
        pl.BlockSpec((BLOCK, LANES), lambda n: (nxt(n), vcol)),
        pl.BlockSpec((n_ctx, LANES), lambda n: (0, kcol)),
        pl.BlockSpec((n_ctx, LANES), lambda n: (0, vcol)),
    ]
    return pl.pallas_call(
        functools.partial(_attn_lat_kernel, fa_index),
        grid=(nb,),
        in_specs=in_specs,
        out_specs=pl.BlockSpec((BLOCK, ATT_WIDTH), lambda n: (n, 0)),
        out_shape=jax.ShapeDtypeStruct((s, ATT_WIDTH), BF16),
        compiler_params=_params("arbitrary"),
        name="attention_latent",
    )(sink, qkv, qkv, qkv, qkv, qkv, qkv, qkv, qkv_ctx, qkv_ctx)


def _attention_ctx(qkv_ctx, sink, fa_index):
    n_ctx = qkv_ctx.shape[0]
    kcol = ATT_WIDTH // LANES
    return pl.pallas_call(
        functools.partial(_attn_ctx_kernel, fa_index),
        grid=(1,),
        in_specs=[
            pl.BlockSpec(memory_space=pltpu.SMEM),
            pl.BlockSpec((n_ctx, ATT_WIDTH), lambda n: (0, 0)),
            pl.BlockSpec((n_ctx, LANES), lambda n: (0, kcol)),
            pl.BlockSpec((n_ctx, LANES), lambda n: (0, kcol + 1)),
        ],
        out_specs=pl.BlockSpec((n_ctx, ATT_WIDTH), lambda n: (0, 0)),
        out_shape=jax.ShapeDtypeStruct((n_ctx, ATT_WIDTH), BF16),
        compiler_params=_params("arbitrary"),
        name="attention_context",
    )(sink, qkv_ctx, qkv_ctx, qkv_ctx)


FFT_N1 = 64
FFT_N2 = 128


def _dft_cos_sin(n):
    idx = np.arange(n)
    ang = 2.0 * np.pi * ((idx[:, None] * idx[None, :]) % n) / n
    return np.cos(ang), np.sin(ang)


def _fourier_constants():
    c1, s1 = _dft_cos_sin(FFT_N1)
    stage1 = np.concatenate([c1, -s1], axis=0)
    c2, s2 = _dft_cos_sin(FFT_N2)
    stage2 = np.block([[c2, s2], [-s2, c2]])
    cc, sc = _dft_cos_sin(FG_W)
    chan = np.concatenate([cc, sc], axis=0)
    return (jnp.asarray(stage1, F32), jnp.asarray(stage2, F32), jnp.asarray(chan, F32))


def _twiddle_tables():
    k1 = jnp.arange(FFT_N1, dtype=jnp.int32)[:, None]
    n2 = jnp.arange(FFT_N2, dtype=jnp.int32)[None, :]
    ang = ((k1 * n2) % (FFT_N1 * FFT_N2)).astype(F32) * (2.0 * math.pi / (FFT_N1 * FFT_N2))
    wr = jnp.repeat(jnp.cos(ang), LANES, axis=1)
    wi = jnp.repeat(-jnp.sin(ang), LANES, axis=1)
    return wr, wi


def _fourier_stage1_kernel(x_ref, m_ref, wr_ref, wi_ref, t_ref):
    y = _dot(m_ref[...].astype(BF16), x_ref[...].astype(BF16))
    reps = F_WIDTH // LANES
    for b in range(x_ref.shape[1] // F_WIDTH):
        cols = slice(b * F_WIDTH, (b + 1) * F_WIDTH)
        yr = y[:FFT_N1, cols]
        yi = y[FFT_N1:, cols]
        wr = jnp.tile(wr_ref[:, b * LANES:(b + 1) * LANES], (1, reps))
        wi = jnp.tile(wi_ref[:, b * LANES:(b + 1) * LANES], (1, reps))
        t_ref[:, 2 * b * F_WIDTH:(2 * b + 1) * F_WIDTH] = (yr * wr - yi * wi).astype(t_ref.dtype)
        t_ref[:, (2 * b + 1) * F_WIDTH:(2 * b + 2) * F_WIDTH] = (yr * wi + yi * wr).astype(t_ref.dtype)


def _channel_stage(p, chan_ref, scale, o_ref):
    r = p.shape[0] // 2
    pr = p[:r].astype(BF16)
    pi = p[r:].astype(BF16)
    chan_c = chan_ref[:FG_W, :].astype(BF16)
    chan_s = chan_ref[FG_W:, :].astype(BF16)
    for g in range(N_FG):
        cols = slice(g * FG_W, (g + 1) * FG_W)
        y = _dot(pr[:, cols], chan_c) + _dot(pi[:, cols], chan_s)
        o_ref[:, cols] = (y * scale).astype(o_ref.dtype)


def _fourier_stage2_kernel(scale, t_ref, m_ref, chan_ref, o_ref):
    m = m_ref[...].astype(BF16)
    for kk in range(t_ref.shape[0]):
        t = jnp.concatenate([t_ref[kk, :, :F_WIDTH], t_ref[kk, :, F_WIDTH:]], axis=0)
        _channel_stage(_dot(m, t), chan_ref, scale, o_ref.at[:, kk * F_WIDTH:(kk + 1) * F_WIDTH])


def _fourier_ctx_kernel(scale, x_ref, m_ref, chan_ref, o_ref):
    _channel_stage(_dot(m_ref[...].astype(BF16), x_ref[...].astype(BF16)), chan_ref, scale, o_ref)


def _fourier_lat(f, consts, twiddles):
    n = f.shape[0]
    assert n == FFT_N1 * FFT_N2
    stage1, stage2, chan = consts
    wr, wi = twiddles
    n2_blk = 8
    cols = n2_blk * F_WIDTH
    full = lambda a: pl.BlockSpec(a.shape, lambda j: (0,) * a.ndim)
    t = pl.pallas_call(
        _fourier_stage1_kernel,
        grid=(FFT_N2 // n2_blk,),
        in_specs=[
            pl.BlockSpec((FFT_N1, cols), lambda j: (0, j)),
            full(stage1),
            pl.BlockSpec((FFT_N1, n2_blk * LANES), lambda j: (0, j)),
            pl.BlockSpec((FFT_N1, n2_blk * LANES), lambda j: (0, j)),
        ],
        out_specs=pl.BlockSpec((FFT_N1, 2 * cols), lambda j: (0, j)),
        out_shape=jax.ShapeDtypeStruct((FFT_N1, FFT_N2 * 2 * F_WIDTH), BF16),
        compiler_params=_params("arbitrary"),
        name="fourier_stage1",
    )(f.reshape(FFT_N1, FFT_N2 * F_WIDTH), stage1, wr, wi)
    t = t.reshape(FFT_N1, FFT_N2, 2 * F_WIDTH)
    scale = 1.0 / math.sqrt(n * FG_W)
    k1_blk = 4
    out = pl.pallas_call(
        functools.partial(_fourier_stage2_kernel, scale),
        grid=(FFT_N1 // k1_blk,),
        in_specs=[
            pl.BlockSpec((k1_blk, FFT_N2, 2 * F_WIDTH), lambda k1: (k1, 0, 0)),
            full(stage2),
            full(chan),
        ],
        out_specs=pl.BlockSpec((FFT_N2, k1_blk * F_WIDTH), lambda k1: (0, k1)),
        out_shape=jax.ShapeDtypeStruct((FFT_N2, FFT_N1 * F_WIDTH), BF16),
        compiler_params=_params("arbitrary"),
        name="fourier_stage2",
    )(t, stage2, chan)
    return out.reshape(n, F_WIDTH)


def _fourier_ctx(f, chan):
    n = f.shape[0]
    c, s = _dft_cos_sin(n)
    m = jnp.asarray(np.concatenate([c, -s], axis=0), F32)
    full = lambda a: pl.BlockSpec(a.shape, lambda j: (0,) * a.ndim)
    return pl.pallas_call(
        functools.partial(_fourier_ctx_kernel, 1.0 / math.sqrt(n * FG_W)),
        grid=(1,),
        in_specs=[full(f), full(m), full(chan)],
        out_specs=pl.BlockSpec((n, F_WIDTH), lambda j: (0, 0)),
        out_shape=jax.ShapeDtypeStruct((n, F_WIDTH), BF16),
        compiler_params=_params("arbitrary"),
        name="fourier_context",
    )(f, m, chan)


def _scan8(a, b, row, reverse):
    for d in (1, 2, 4):
        shift = (SUBLANES - d) if reverse else d
        keep = (row < SUBLANES - d) if reverse else (row >= d)
        b = jnp.where(keep, b + a * pltpu.roll(b, shift, axis=0), b)
        a = jnp.where(keep, a * pltpu.roll(a, shift, axis=0), a)
    return a, b


def _rows(x, n):
    return jnp.broadcast_to(x, (n, x.shape[1]))


def _rglru_kernel(reverse, combine, xs_ref, wa_ref, wi_ref, ba_ref, bi_ref, lam_ref, h0_ref, *refs):
    if combine:
        hb_ref, gg_ref, o_ref, last_ref, w_scr, a_scr, b_scr, c_scr, carry_scr = refs
    else:
        o_ref, last_ref, w_scr, a_scr, b_scr, c_scr, carry_scr = refs
    t = pl.program_id(1)
    tt, c = xs_ref.shape
    n_slab = c // LANES
    slabs_per_block = RNN_BLOCK // LANES
    groups = tt // SUBLANES
    blocks = groups // SUBLANES
    row = lax.broadcasted_iota(jnp.int32, (SUBLANES, LANES), 0)
    edge = 0 if reverse else SUBLANES - 1

    @pl.when(t == 0)
    def _():
        carry_scr[...] = jnp.broadcast_to(h0_ref[...], carry_scr.shape)
        w_scr[:, :, :RNN_BLOCK] = (0.5 * wa_ref[...]).astype(BF16)
        w_scr[:, :, RNN_BLOCK:] = (0.5 * wi_ref[...]).astype(BF16)

    pre = [_dot(xs_ref[:, nb * RNN_BLOCK:(nb + 1) * RNN_BLOCK].astype(BF16), w_scr[nb])
           for nb in range(c // RNN_BLOCK)]
    neg_lam = -lam_ref[...]
    softplus = jnp.maximum(neg_lam, 0.0) + jnp.log1p(jnp.exp(-jnp.abs(neg_lam)))
    k_all = softplus * (-0.5 * LRU_C * math.log2(math.e))
    ba_all = 0.5 * ba_ref[...]
    bi_all = 0.5 * bi_ref[...]

    for s in range(n_slab):
        lanes = slice(s * LANES, (s + 1) * LANES)
        k = _rows(k_all[:, lanes], SUBLANES)
        ba = _rows(ba_all[:, lanes], SUBLANES)
        bi = _rows(bi_all[:, lanes], SUBLANES)
        pre_b = pre[s // slabs_per_block]
        l0 = (s % slabs_per_block) * LANES
        for g in range(groups):
            r0 = g * SUBLANES
            tr = jnp.tanh(pre_b[r0:r0 + SUBLANES, l0:l0 + LANES] + ba)
            ti = jnp.tanh(pre_b[r0:r0 + SUBLANES, RNN_BLOCK + l0:RNN_BLOCK + l0 + LANES] + bi)
            a = jnp.exp2((1.0 + tr) * k)
            y = 1.0 - a * a
            root = jnp.where(y > 0.0, y * lax.rsqrt(y), 0.0)
            b = root * ((1.0 + ti) * (0.5 * xs_ref[r0:r0 + SUBLANES, lanes]))
            a_cum, b_loc = _scan8(a, b, row, reverse)
            a_scr[s, r0:r0 + SUBLANES, :] = a_cum
            b_scr[s, r0:r0 + SUBLANES, :] = b_loc

    for s in range(n_slab):
        a2 = a_scr[s, pl.ds(edge, groups, stride=SUBLANES), :]
        b2 = b_scr[s, pl.ds(edge, groups, stride=SUBLANES), :]
        carry = carry_scr[:, s * LANES:(s + 1) * LANES]
        enter_row = groups if reverse else SUBLANES - 1
        base = 0 if reverse else SUBLANES
        c_scr[s, enter_row:enter_row + 1, :] = carry[0:1, :]
        for j in (range(blocks - 1, -1, -1) if reverse else range(blocks)):
            r0 = j * SUBLANES
            a_cum, b_loc = _scan8(a2[r0:r0 + SUBLANES, :], b2[r0:r0 + SUBLANES, :], row, reverse)
            st = b_loc + a_cum * carry
            c_scr[s, base + r0:base + r0 + SUBLANES, :] = st
            carry = _rows(st[edge:edge + 1, :], SUBLANES)
        carry_scr[:, s * LANES:(s + 1) * LANES] = carry
        last_ref[:, s * LANES:(s + 1) * LANES] = carry[0:1, :]

    pair = 2 * SUBLANES
    for s in range(n_slab):
        lanes = slice(s * LANES, (s + 1) * LANES)
        for g in range(0, groups, 2):
            r0 = g * SUBLANES
            src = g + 1 if reverse else g + SUBLANES - 1
            enter = jnp.concatenate([_rows(c_scr[s, src:src + 1, :], SUBLANES),
                                     _rows(c_scr[s, src + 1:src + 2, :], SUBLANES)], axis=0)
            h = b_scr[s, r0:r0 + pair, :] + a_scr[s, r0:r0 + pair, :] * enter
            if combine:
                h = (h + hb_ref[r0:r0 + pair, lanes]) * gg_ref[r0:r0 + pair, lanes].astype(F32)
            o_ref[r0:r0 + pair, lanes] = h.astype(o_ref.dtype)


def _rglru_scan(xs, w_a, w_i, b_a, b_i, lam, rg_index, direction, h0, h_other=None, gelu_gate=None):
    m = xs.shape[0]
    tt = min(m, 1024)
    nt = m // tt
    c = SCAN_BLOCKS * RNN_BLOCK
    reverse = direction == 1
    combine = h_other is not None
    tix = (lambda t: nt - 1 - t) if reverse else (lambda t: t)
    wspec = pl.BlockSpec((None, None, SCAN_BLOCKS, RNN_BLOCK, RNN_BLOCK),
                         lambda cb, t: (rg_index, direction, cb, 0, 0))
    vspec = pl.BlockSpec((None, None, 1, c), lambda cb, t: (rg_index, direction, 0, cb))
    tile = pl.BlockSpec((tt, c), lambda cb, t: (tix(t), cb))
    in_specs = [tile, wspec, wspec, vspec, vspec, vspec, pl.BlockSpec((1, c), lambda cb, t: (0, cb))]
    args = [xs, w_a, w_i, b_a, b_i, lam, h0]
    if combine:
        in_specs += [tile, tile]
        args += [h_other, gelu_gate]
    return pl.pallas_call(
        functools.partial(_rglru_kernel, reverse, combine),
        grid=(D_RNN // c, nt),
        in_specs=in_specs,
        out_specs=[tile, pl.BlockSpec((1, c), lambda cb, t: (0, cb))],
        out_shape=[jax.ShapeDtypeStruct((m, D_RNN), BF16 if combine else F32),
                   jax.ShapeDtypeStruct((1, D_RNN), F32)],
        scratch_shapes=[pltpu.VMEM((SCAN_BLOCKS, RNN_BLOCK, 2 * RNN_BLOCK), BF16),
                        pltpu.VMEM((c // LANES, tt, LANES), F32),
                        pltpu.VMEM((c // LANES, tt, LANES), F32),
                        pltpu.VMEM((c // LANES, tt // SUBLANES + 2 * SUBLANES, LANES), F32),
                        pltpu.VMEM((SUBLANES, c), F32)],
        compiler_params=_params("arbitrary", "arbitrary"),
        name="rglru_scan",
    )(*args)


def _final_norm_kernel(x_ref, g_ref, o_ref):
    g = g_ref[...]
    for r0 in range(0, x_ref.shape[0], NORM_ROWS):
        x = x_ref[r0:r0 + NORM_ROWS, :]
        r = lax.rsqrt(jnp.mean(x * x, axis=-1, keepdims=True) + EPS)
        o_ref[r0:r0 + NORM_ROWS, :] = (x * r) * g


def _final_norm(x, g):
    m, d = x.shape
    tm = 512
    return pl.pallas_call(
        _final_norm_kernel,
        grid=(m // tm,),
        in_specs=[pl.BlockSpec((tm, d), lambda i: (i, 0)), pl.BlockSpec((1, d), lambda i: (0, 0))],
        out_specs=pl.BlockSpec((tm, d), lambda i: (i, 0)),
        out_shape=jax.ShapeDtypeStruct((m, d), F32),
        compiler_params=_params("arbitrary"),
        name="final_norm",
    )(x, g.reshape(1, d))


def _rope_tables(n):
    f = HEAD_DIM // 4
    inv = ROPE_BASE ** (-jnp.arange(f, dtype=F32) / f)
    pos = jnp.arange(n, dtype=jnp.int32)
    ang_r = (pos // GRID_W).astype(F32)[:, None] * inv[None, :]
    ang_c = (pos % GRID_W).astype(F32)[:, None] * inv[None, :]
    cr, sr, cc, sc = jnp.cos(ang_r), jnp.sin(ang_r), jnp.cos(ang_c), jnp.sin(ang_c)
    cos = jnp.concatenate([cr, cr, cc, cc], axis=1)
    sin = jnp.concatenate([-sr, sr, -sc, sc], axis=1)
    reps = LANES // HEAD_DIM
    return jnp.tile(cos, (1, reps)), jnp.tile(sin, (1, reps))


def _fourier_attn_layer(layer, i, x_lat, x_ctx, h_lat, h_ctx, mods, fa_w_in, fa_w_out, attn_sink,
                        tables, ctx_out):
    rope, consts, twiddles = tables
    qkv_w = ATT_WIDTH + 2 * KV_WIDTH
    f_lat = _proj("plain", h_lat, fa_w_in, i, 0, F_WIDTH, F_WIDTH, F_WIDTH, BF16)
    qkv_lat = _proj("qkv_rope", h_lat, fa_w_in, i, F_WIDTH, qkv_w, qkv_w, 256, BF16, tm=512, extra=rope)
    qkv_ctx = _proj("qkv", h_ctx, fa_w_in, i, F_WIDTH, qkv_w, qkv_w, 256, BF16)
    fo_lat = _fourier_lat(f_lat, consts, twiddles)
    ao_lat = _attention_lat(qkv_lat, qkv_ctx, attn_sink, i)
    parts_ctx = None
    if ctx_out:
        f_ctx = _proj("plain", h_ctx, fa_w_in, i, 0, F_WIDTH, F_WIDTH, F_WIDTH, BF16)
        parts_ctx = [_fourier_ctx(f_ctx, consts[2]), _attention_ctx(qkv_ctx, attn_sink, i)]
    return _proj_residual([fo_lat, ao_lat], fa_w_out, i, x_lat, layer, mods, 2, 1024, 1024, parts_ctx, x_ctx)


def _rglru_layer(layer, i, x_lat, x_ctx, h_lat, h_ctx, mods, rg_w_in, rg_conv_w, rg_conv_b, w_a, b_a,
                 w_i, b_i, lam, rg_w_out, ctx_out):
    tn = 1024
    xcol = D_RNN // tn
    zero_state = jnp.zeros((1, D_RNN), F32)
    gate_lat = _proj("gelu", h_lat, rg_w_in, i, 0, D_RNN, tn, tn, BF16)
    conv = functools.partial(_conv_proj, w=rg_w_in, w_index=i, col_blocks=[xcol], conv_blocks=[0], n_out=D_RNN,
                             cw=rg_conv_w, cb=rg_conv_b, kw=CONV_W, left=CONV_LEFT, gated=False, tm=1024, tn=tn,
                             out_dtype=F32)
    xs_lat = conv(h_lat)
    xs_ctx = conv(h_ctx)
    scan = functools.partial(_rglru_scan, w_a=w_a, w_i=w_i, b_a=b_a, b_i=b_i, lam=lam, rg_index=i)
    hb_ctx, s_bwd = scan(xs_ctx, direction=1, h0=zero_state)
    parts_ctx = None
    if ctx_out:
        gate_ctx = _proj("gelu", h_ctx, rg_w_in, i, 0, D_RNN, tn, tn, BF16)
        y_ctx, s_fwd = scan(xs_ctx, direction=0, h0=zero_state, h_other=hb_ctx, gelu_gate=gate_ctx)
        parts_ctx = [y_ctx]
    else:
        _, s_fwd = scan(xs_ctx, direction=0, h0=zero_state)
    hb_lat, _ = scan(xs_lat, direction=1, h0=s_bwd)
    y_lat, _ = scan(xs_lat, direction=0, h0=s_fwd, h_other=hb_lat, gelu_gate=gate_lat)
    return _proj_residual([y_lat], rg_w_out, i, x_lat, layer, mods, 2, 1024, 1024, parts_ctx, x_ctx)


def _conv_ffn(layer, x_lat, x_ctx, g_ffn, mods, w_up, conv_w, conv_b, w_down):
    h_lat = _norm_mod(x_lat, g_ffn, layer, mods, 0, 3, 4)
    h_ctx = None if x_ctx is None else _norm_mod(x_ctx, g_ffn, layer, mods, 1, 3, 4)
    tf = 512
    blocks = [0, D_FF // tf]
    conv = functools.partial(_conv_proj, w=w_up, w_index=layer, col_blocks=blocks, conv_blocks=blocks, n_out=D_FF,
                             cw=conv_w, cb=conv_b, kw=FFN_CONV_W, left=FFN_CONV_LEFT, gated=True, tm=1024, tn=tf,
                             out_dtype=BF16)
    act_lat = conv(h_lat)
    parts_ctx = None if x_ctx is None else [conv(h_ctx)]
    return _proj_residual([act_lat], w_down, layer, x_lat, layer, mods, 5, 512, 512, parts_ctx, x_ctx)


def kernel(x, c, ctx, c_ctx, w_mod, b_mod, g_mix, g_ffn, fa_w_in, fa_w_out, attn_sink, rg_w_in, rg_conv_w,
           rg_conv_b, rg_w_a, rg_b_a, rg_w_i, rg_b_i, rg_lambda, rg_w_out, ffn_w_up, ffn_conv_w, ffn_conv_b,
           ffn_w_down, g_final):
    assert x.shape[0] == 1 and ctx.shape[0] == 1
    n = x.shape[1]
    x_lat = x[0]
    x_ctx = ctx[0]
    mods = _modulation(c, c_ctx, w_mod, b_mod)
    tables = (_rope_tables(n), _fourier_constants(), _twiddle_tables())
    g_mix3 = g_mix.reshape(DEPTH, 1, D_MODEL)
    g_ffn3 = g_ffn.reshape(DEPTH, 1, D_MODEL)
    n_rg = rg_conv_b.shape[0]
    rg_conv_b3 = rg_conv_b.reshape(n_rg, 1, D_RNN)
    rg_b_a4 = rg_b_a.reshape(n_rg, 2, 1, D_RNN)
    rg_b_i4 = rg_b_i.reshape(n_rg, 2, 1, D_RNN)
    rg_lam4 = rg_lambda.reshape(n_rg, 2, 1, D_RNN)
    ffn_conv_b3 = ffn_conv_b.reshape(DEPTH, 1, 2 * D_FF)
    for layer in range(DEPTH):
        ctx_out = layer < DEPTH - 1
        i = layer // 2
        h_lat = _norm_mod(x_lat, g_mix3, layer, mods, 0, 0, 1)
        h_ctx = _norm_mod(x_ctx, g_mix3, layer, mods, 1, 0, 1)
        if layer % 2 == 0:
            x_lat, x_ctx = _fourier_attn_layer(layer, i, x_lat, x_ctx, h_lat, h_ctx, mods, fa_w_in, fa_w_out,
                                               attn_sink, tables, ctx_out)
        else:
            x_lat, x_ctx = _rglru_layer(layer, i, x_lat, x_ctx, h_lat, h_ctx, mods, rg_w_in, rg_conv_w,
                                        rg_conv_b3, rg_w_a, rg_b_a4, rg_w_i, rg_b_i4, rg_lam4, rg_w_out,
                                        ctx_out)
        x_lat, x_ctx = _conv_ffn(layer, x_lat, x_ctx if ctx_out else None, g_ffn3, mods, ffn_w_up, ffn_conv_w,
                                 ffn_conv_b3, ffn_w_down)
    return _final_norm(x_lat, g_final)[None]
```

```python
import functools
import math

import numpy as np
import jax
import jax.numpy as jnp
from jax import lax
from jax.experimental import pallas as pl
from jax.experimental.pallas import tpu as pltpu

D_MODEL = 2048
DEPTH = 4
GRID_W = 64
N_FG = 4
FG_W = 256
F_WIDTH = N_FG * FG_W
N_HEADS = 16
N_KV_HEADS = 2
HEAD_DIM = 64
ATT_WIDTH = N_HEADS * HEAD_DIM
KV_WIDTH = N_KV_HEADS * HEAD_DIM
WINDOW = 128
BLOCK = 128
ROPE_BASE = 10000.0
D_RNN = D_MODEL
N_RNN_BLOCKS = 8
RNN_BLOCK = D_RNN // N_RNN_BLOCKS
CONV_W = 4
CONV_LEFT = 2
LRU_C = 8.0
D_FF = 5632
FFN_CONV_W = 3
FFN_CONV_LEFT = 1
N_MOD = 6
EPS = 1e-6
NEG_INF = -1e30

LANES = 128
SUBLANES = 8
HALO = 16
VMEM_LIMIT = 56 * 1024 * 1024
ROW_CHUNK = 64
PROJ_BLOCK = 1024
PAIR_FORMS = 4
QKV_OUT = ATT_WIDTH + 2 * PAIR_FORMS * LANES
SCAN_BLOCKS = 2

BF16 = jnp.bfloat16
F32 = jnp.float32


def _params(*sem):
    return pltpu.CompilerParams(dimension_semantics=sem, vmem_limit_bytes=VMEM_LIMIT)


def _dot(a, b):
    return jnp.dot(a, b, preferred_element_type=F32)


def _dot_nt(a, b):
    return lax.dot_general(a, b, (((1,), (1,)), ((), ())), preferred_element_type=F32)


def _gelu_tanh(x):
    return 0.5 * x * (1.0 + jnp.tanh(math.sqrt(2.0 / math.pi) * (x + 0.044715 * (x * x * x))))


def _sigmoid(x):
    return 0.5 * (1.0 + jnp.tanh(0.5 * x))


def _mod_kernel(cl_ref, cc_ref, w_ref, b_ref, o_ref, sl_ref, sc_ref):
    @pl.when((pl.program_id(0) == 0) & (pl.program_id(1) == 0))
    def _():
        cl = cl_ref[...]
        cc = cc_ref[...]
        sl_ref[...] = cl * _sigmoid(cl)
        sc_ref[...] = cc * _sigmoid(cc)

    tn = w_ref.shape[1]
    reps = tn // LANES

    def body(kg, carry):
        al, ac = carry
        r0 = pl.multiple_of(kg * SUBLANES, SUBLANES)
        w8 = w_ref[pl.ds(r0, SUBLANES), :]
        s8l = jnp.tile(sl_ref[pl.ds(r0, SUBLANES), :], (1, reps))
        s8c = jnp.tile(sc_ref[pl.ds(r0, SUBLANES), :], (1, reps))
        return al + w8 * s8l, ac + w8 * s8c

    zero = jnp.zeros((SUBLANES, tn), F32)
    al, ac = lax.fori_loop(0, w_ref.shape[0] // SUBLANES, body, (zero, zero), unroll=4)
    b = b_ref[...]
    o_ref[0] = jnp.sum(al, axis=0, keepdims=True) + b
    o_ref[1] = jnp.sum(ac, axis=0, keepdims=True) + b


def _modulation(c, c_ctx, w_mod, b_mod):
    d = D_MODEL
    n = N_MOD * d
    tn = 1024
    cl = jnp.broadcast_to(c.reshape(d, 1), (d, LANES))
    cc = jnp.broadcast_to(c_ctx.reshape(d, 1), (d, LANES))
    return pl.pallas_call(
        _mod_kernel,
        grid=(DEPTH, n // tn),
        in_specs=[
            pl.BlockSpec((d, LANES), lambda l, j: (0, 0)),
            pl.BlockSpec((d, LANES), lambda l, j: (0, 0)),
            pl.BlockSpec((None, d, tn), lambda l, j: (l, 0, j)),
            pl.BlockSpec((None, 1, tn), lambda l, j: (l, 0, j)),
        ],
        out_specs=pl.BlockSpec((None, 2, 1, tn), lambda l, j: (l, 0, 0, j)),
        out_shape=jax.ShapeDtypeStruct((DEPTH, 2, 1, n), F32),
        scratch_shapes=[pltpu.VMEM((d, LANES), F32), pltpu.VMEM((d, LANES), F32)],
        compiler_params=_params("arbitrary", "arbitrary"),
        name="modulation",
    )(cl, cc, w_mod, b_mod.reshape(DEPTH, 1, n))


def _mod_spec(layer, row, k, tn, col_of):
    per = D_MODEL // tn
    return pl.BlockSpec((None, None, 1, tn), lambda *g: (layer, row, 0, k * per + col_of(*g)))


NORM_ROWS = 16


def _norm_mod_kernel(x_ref, g_ref, sh_ref, sc_ref, o_ref):
    g = g_ref[...]
    gain = g + g * sc_ref[...]
    shift = sh_ref[...]
    for r0 in range(0, x_ref.shape[0], NORM_ROWS):
        x = x_ref[r0:r0 + NORM_ROWS, :]
        r = lax.rsqrt(jnp.mean(x * x, axis=-1, keepdims=True) + EPS)
        o_ref[r0:r0 + NORM_ROWS, :] = ((x * r) * gain + shift).astype(o_ref.dtype)


def _norm_mod(x, g, layer, mods, row, k_shift, k_scale):
    m, d = x.shape
    tm = min(m, 1024)
    zero = lambda i: 0
    return pl.pallas_call(
        _norm_mod_kernel,
        grid=(m // tm,),
        in_specs=[
            pl.BlockSpec((tm, d), lambda i: (i, 0)),
            pl.BlockSpec((None, 1, d), lambda i: (layer, 0, 0)),
            _mod_spec(layer, row, k_shift, d, zero),
            _mod_spec(layer, row, k_scale, d, zero),
        ],
        out_specs=pl.BlockSpec((tm, d), lambda i: (i, 0)),
        out_shape=jax.ShapeDtypeStruct((m, d), BF16),
        compiler_params=_params("arbitrary"),
        name="norm_mod",
    )(x, g, mods, mods)


def _cast_weights(w_refs, wb_ref):
    @pl.when(pl.program_id(1) == 0)
    def _():
        c0 = 0
        for w_ref in w_refs:
            wn = w_ref.shape[1]
            wb_ref[:, c0:c0 + wn] = w_ref[...].astype(BF16)
            c0 += wn


def _proj_plain_kernel(n_w, blk, h_ref, *refs):
    o_ref, wb_ref = refs[n_w:]
    _cast_weights(refs[:n_w], wb_ref)
    o_ref[...] = _dot(h_ref[...], wb_ref[...]).astype(o_ref.dtype)


def _blocked_dot(lhs_block, rows, blk, epilogue):
    blk = min(rows, blk)
    for r0 in range(0, rows, blk):
        acc = lhs_block(r0, blk)
        for r in range(0, blk, ROW_CHUNK):
            epilogue(acc[r:r + ROW_CHUNK, :], r0 + r)


def _proj_gelu_kernel(n_w, blk, h_ref, *refs):
    o_ref, wb_ref = refs[n_w:]
    _cast_weights(refs[:n_w], wb_ref)

    def epilogue(acc, r):
        o_ref[r:r + ROW_CHUNK, :] = _gelu_tanh(acc).astype(o_ref.dtype)

    _blocked_dot(lambda r0, n: _dot(h_ref[r0:r0 + n, :], wb_ref[...]), h_ref.shape[0], blk, epilogue)


def _swap16(x, even):
    return jnp.where(even, pltpu.roll(x, LANES - 16, axis=1), pltpu.roll(x, 16, axis=1))


def _proj_qkv_kernel(rope, n_w, blk, h_ref, *refs):
    if rope:
        cos_ref, sin_ref, o_ref, wb_ref = refs[n_w:]
    else:
        o_ref, wb_ref = refs[n_w:]
    _cast_weights(refs[:n_w], wb_ref)
    q_scale = HEAD_DIM ** -0.5
    q_chunks = ATT_WIDTH // LANES
    n_rot = (ATT_WIDTH + KV_WIDTH) // LANES
    lane = lax.broadcasted_iota(jnp.int32, (ROW_CHUNK, LANES), 1)
    even = (lane & 16) == 0
    low = lane < HEAD_DIM

    def store(r, chunk, t):
        o_ref[r:r + ROW_CHUNK, chunk * LANES:(chunk + 1) * LANES] = t.astype(o_ref.dtype)

    def epilogue(acc, r):
        if rope:
            cos = cos_ref[r:r + ROW_CHUNK, :]
            sin = sin_ref[r:r + ROW_CHUNK, :]
        for cidx in range(acc.shape[1] // LANES):
            t = acc[:, cidx * LANES:(cidx + 1) * LANES]
            if cidx < q_chunks:
                t = t * q_scale
            if rope and cidx < n_rot:
                t = t * cos + _swap16(t, even) * sin
            if cidx < q_chunks:
                store(r, cidx, t)
            else:
                swapped = pltpu.roll(t, HEAD_DIM, axis=1)
                zero = jnp.zeros_like(t)
                base = q_chunks + (cidx - q_chunks) * PAIR_FORMS
                store(r, base, jnp.where(low, t, zero))
                store(r, base + 1, jnp.where(low, zero, swapped))
                store(r, base + 2, jnp.where(low, swapped, zero))
                store(r, base + 3, jnp.where(low, zero, t))

    _blocked_dot(lambda r0, n: _dot(h_ref[r0:r0 + n, :], wb_ref[...]), h_ref.shape[0], blk, epilogue)


def _proj(kind, h, w, w_index, col0, n, tn, wtn, out_dtype, tm=None, extra=(), blk=PROJ_BLOCK, out_tn=None):
    m, k = h.shape
    tm = tm or min(m, 1024)
    out_tn = out_tn or tn
    assert col0 % wtn == 0 and tn % wtn == 0 and n % tn == 0 and m % tm == 0
    n_w = tn // wtn
    kernels = {
        "plain": _proj_plain_kernel,
        "gelu": _proj_gelu_kernel,
        "qkv": functools.partial(_proj_qkv_kernel, False),
        "qkv_rope": functools.partial(_proj_qkv_kernel, True),
    }
    in_specs = [pl.BlockSpec((tm, k), lambda j, i: (i, 0))]
    for p in range(n_w):
        in_specs.append(pl.BlockSpec((None, k, wtn),
                                     lambda j, i, p=p: (w_index, 0, col0 // wtn + j * n_w + p)))
    in_specs += [pl.BlockSpec((tm, LANES), lambda j, i: (i, 0)) for _ in extra]
    return pl.pallas_call(
        functools.partial(kernels[kind], n_w, blk),
        grid=(n // tn, m // tm),
        in_specs=in_specs,
        out_specs=pl.BlockSpec((tm, out_tn), lambda j, i: (i, j)),
        out_shape=jax.ShapeDtypeStruct((m, n // tn * out_tn), out_dtype),
        scratch_shapes=[pltpu.VMEM((k, tn), BF16)],
        compiler_params=_params("arbitrary", "arbitrary"),
        name="proj_" + kind,
    )(h, *([w] * n_w), *extra)


def _proj_res_kernel(n_parts, n_groups, n_lat, *refs):
    per = n_parts + 2
    groups = [refs[g * per:(g + 1) * per] for g in range(n_groups)]
    w_ref = refs[n_groups * per]
    o_refs = refs[n_groups * per + 1:n_groups * per + 1 + n_groups]
    wb_ref = refs[-1]
    _cast_weights([w_ref], wb_ref)

    def tile(group, o_ref):
        a_refs, (x_ref, gt_ref) = group[:n_parts], group[n_parts:]

        def lhs_block(r0, n):
            acc = None
            k0 = 0
            for a_ref in a_refs:
                kp = a_ref.shape[1]
                part = _dot(a_ref[r0:r0 + n, :], wb_ref[k0:k0 + kp, :])
                acc = part if acc is None else acc + part
                k0 += kp
            return acc

        def epilogue(acc, r):
            o_ref[r:r + ROW_CHUNK, :] = x_ref[r:r + ROW_CHUNK, :] + gt_ref[...] * acc

        _blocked_dot(lhs_block, x_ref.shape[0], PROJ_BLOCK, epilogue)

    i = pl.program_id(1)
    if n_groups == 1:
        tile(groups[0], o_refs[0])
    else:
        @pl.when(i < n_lat)
        def _():
            tile(groups[0], o_refs[0])

        @pl.when(i == n_lat)
        def _():
            tile(groups[1], o_refs[1])


def _proj_residual(parts, w, w_index, x, layer, mods, k_gate, tm, tn, parts_ctx=None, x_ctx=None):
    m, n = x.shape
    k = sum(p.shape[1] for p in parts)
    tm = min(m, tm)
    n_lat = m // tm
    lat_row = lambda j, i: jnp.minimum(i, n_lat - 1)
    in_specs = [pl.BlockSpec((tm, p.shape[1]), lambda j, i: (lat_row(j, i), 0)) for p in parts]
    in_specs += [pl.BlockSpec((tm, tn), lambda j, i: (lat_row(j, i), j)),
                 _mod_spec(layer, 0, k_gate, tn, lambda j, i: j)]
    args = [*parts, x, mods]
    out_specs = [pl.BlockSpec((tm, tn), lambda j, i: (lat_row(j, i), j))]
    out_shape = [jax.ShapeDtypeStruct((m, n), F32)]
    n_groups = 1
    if parts_ctx is not None:
        mc = x_ctx.shape[0]
        in_specs += [pl.BlockSpec((mc, p.shape[1]), lambda j, i: (0, 0)) for p in parts_ctx]
        in_specs += [pl.BlockSpec((mc, tn), lambda j, i: (0, j)),
                     _mod_spec(layer, 1, k_gate, tn, lambda j, i: j)]
        args += [*parts_ctx, x_ctx, mods]
        out_specs.append(pl.BlockSpec((mc, tn), lambda j, i: (0, j)))
        out_shape.append(jax.ShapeDtypeStruct((mc, n), F32))
        n_groups = 2
    in_specs.append(pl.BlockSpec((None, k, tn), lambda j, i: (w_index, 0, j)))
    args.append(w)
    outs = pl.pallas_call(
        functools.partial(_proj_res_kernel, len(parts), n_groups, n_lat),
        grid=(n // tn, n_lat + n_groups - 1),
        in_specs=in_specs,
        out_specs=out_specs,
        out_shape=out_shape,
        scratch_shapes=[pltpu.VMEM((k, tn), BF16)],
        compiler_params=_params("arbitrary", "arbitrary"),
        name="proj_residual",
    )(*args)
    return outs if n_groups == 2 else (outs[0], None)


def _proj_res_norm_kernel(n_parts, *refs):
    a_refs = refs[:n_parts]
    w_ref, x_ref, gt_ref, g_ref, sh_ref, sc_ref, o_ref, h_ref, wb_ref = refs[n_parts:]

    @pl.when(pl.program_id(0) == 0)
    def _():
        wb_ref[...] = w_ref[...].astype(BF16)

    acc = None
    k0 = 0
    for a_ref in a_refs:
        kp = a_ref.shape[1]
        part = _dot(a_ref[...], wb_ref[k0:k0 + kp, :])
        acc = part if acc is None else acc + part
        k0 += kp
    gate = gt_ref[...]
    g = g_ref[...]
    gain = g + g * sc_ref[...]
    shift = sh_ref[...]
    for r0 in range(0, x_ref.shape[0], NORM_ROWS):
        x = x_ref[r0:r0 + NORM_ROWS, :] + gate * acc[r0:r0 + NORM_ROWS, :]
        o_ref[r0:r0 + NORM_ROWS, :] = x
        r = lax.rsqrt(jnp.mean(x * x, axis=-1, keepdims=True) + EPS)
        h_ref[r0:r0 + NORM_ROWS, :] = ((x * r) * gain + shift).astype(h_ref.dtype)


def _proj_residual_norm(parts, w, w_index, x, layer, mods, row, k_gate, g, k_shift, k_scale):
    m, n = x.shape
    k = sum(p.shape[1] for p in parts)
    tm = min(m, 512)
    zero = lambda i: 0
    in_specs = [pl.BlockSpec((tm, p.shape[1]), lambda i: (i, 0)) for p in parts]
    in_specs += [
        pl.BlockSpec((None, k, n), lambda i: (w_index, 0, 0), pipeline_mode=pl.Buffered(1)),
        pl.BlockSpec((tm, n), lambda i: (i, 0)),
        _mod_spec(layer, row, k_gate, n, zero),
        pl.BlockSpec((None, 1, n), lambda i: (layer, 0, 0)),
        _mod_spec(layer, row, k_shift, n, zero),
        _mod_spec(layer, row, k_scale, n, zero),
    ]
    return pl.pallas_call(
        functools.partial(_proj_res_norm_kernel, len(parts)),
        grid=(m // tm,),
        in_specs=in_specs,
        out_specs=[pl.BlockSpec((tm, n), lambda i: (i, 0))] * 2,
        out_shape=[jax.ShapeDtypeStruct((m, n), F32), jax.ShapeDtypeStruct((m, n), BF16)],
        scratch_shapes=[pltpu.VMEM((k, n), BF16)],
        compiler_params=_params("arbitrary"),
        name="proj_residual_norm",
    )(*parts, w, x, mods, g, mods, mods)


def _conv_proj_kernel(n_w, kw, left, gated, h_ref, hp_ref, hn_ref, *refs):
    w_refs = refs[:n_w]
    cw_refs = refs[n_w:2 * n_w]
    cb_refs = refs[2 * n_w:3 * n_w]
    o_ref, wb_ref, ext_ref, z_ref = refs[3 * n_w:]
    i = pl.program_id(1)
    tm = h_ref.shape[0]
    tn = w_refs[0].shape[1]
    out_slabs = tn // LANES

    @pl.when(i == 0)
    def _():
        for p, w_ref in enumerate(w_refs):
            wb_ref[:, p * tn:(p + 1) * tn] = w_ref[...].astype(BF16)

    ext_ref[0:HALO, :] = jnp.where(i > 0, hp_ref[...], jnp.zeros_like(hp_ref))
    ext_ref[HALO:HALO + tm, :] = h_ref[...]
    ext_ref[HALO + tm:, :] = jnp.where(i < pl.num_programs(1) - 1, hn_ref[...], jnp.zeros_like(hn_ref))

    z = _dot(ext_ref[...], wb_ref[...])
    for t in range(n_w * out_slabs):
        z_ref[t] = z[:, t * LANES:(t + 1) * LANES]
    for so in range(out_slabs):
        c0 = so * LANES
        for r in range(0, tm, ROW_CHUNK):
            outs = []
            for p in range(n_w):
                u = cb_refs[p][:, c0:c0 + LANES]
                for tap in range(kw):
                    r0 = HALO - left + tap + r
                    u = u + z_ref[p * out_slabs + so, r0:r0 + ROW_CHUNK, :] * cw_refs[p][tap:tap + 1, c0:c0 + LANES]
                outs.append(u)
            if gated:
                g, v = outs
                res = g * _sigmoid(g) * v
            else:
                res = outs[0]
            o_ref[r:r + ROW_CHUNK, c0:c0 + LANES] = res.astype(o_ref.dtype)


def _conv_proj(h, w, w_index, col_blocks, conv_blocks, n_out, cw, cb, kw, left, gated, tm, tn, out_dtype):
    m, k = h.shape
    tm = min(m, tm)
    n_w = len(col_blocks)
    hb = m // HALO
    tb = tm // HALO
    in_specs = [
        pl.BlockSpec((tm, k), lambda j, i: (i, 0)),
        pl.BlockSpec((HALO, k), lambda j, i: (jnp.maximum(i * tb - 1, 0), 0)),
        pl.BlockSpec((HALO, k), lambda j, i: (jnp.minimum((i + 1) * tb, hb - 1), 0)),
    ]
    for c0 in col_blocks:
        in_specs.append(pl.BlockSpec((None, k, tn), lambda j, i, c0=c0: (w_index, 0, c0 + j)))
    for c0 in conv_blocks:
        in_specs.append(pl.BlockSpec((None, kw, tn), lambda j, i, c0=c0: (w_index, 0, c0 + j)))
    for c0 in conv_blocks:
        in_specs.append(pl.BlockSpec((None, 1, tn), lambda j, i, c0=c0: (w_index, 0, c0 + j)))
    args = [h, h, h] + [w] * n_w + [cw] * n_w + [cb] * n_w
    ext_rows = tm + 2 * HALO
    return pl.pallas_call(
        functools.partial(_conv_proj_kernel, n_w, kw, left, gated),
        grid=(n_out // tn, m // tm),
        in_specs=in_specs,
        out_specs=pl.BlockSpec((tm, tn), lambda j, i: (i, j)),
        out_shape=jax.ShapeDtypeStruct((m, n_out), out_dtype),
        scratch_shapes=[pltpu.VMEM((k, n_w * tn), BF16), pltpu.VMEM((ext_rows, k), BF16),
                        pltpu.VMEM((n_w * tn // LANES, ext_rows, LANES), F32)],
        compiler_params=_params("arbitrary", "arbitrary"),
        name="conv_proj",
    )(*args)


def _pair_operand(band, kv_head):
    c0 = 2 * kv_head * LANES
    return jnp.concatenate([band[:, c0:c0 + LANES], band[:, c0 + LANES:c0 + 2 * LANES]], axis=0)


def _attention_core(q_ref, k_band, v_band, biases, sink_ref, fa_index, o_ref):
    tq = q_ref.shape[0]
    nk = k_band.shape[0]
    low = lax.broadcasted_iota(jnp.int32, (tq, LANES), 1) < HEAD_DIM
    pairs = N_HEADS // N_KV_HEADS // 2
    for kv_head in range(N_KV_HEADS):
        k2 = _pair_operand(k_band, kv_head)
        v2 = _pair_operand(v_band, kv_head)
        col0 = [(kv_head * pairs + pair) * LANES for pair in range(pairs)]
        q_rows = jnp.concatenate([q_ref[:, c0:c0 + LANES] for c0 in col0], axis=0)
        s_all = _dot_nt(q_rows, k2)
        p_rows, inv_rows = [], []
        for pair, c0 in enumerate(col0):
            s = s_all[pair * tq:(pair + 1) * tq, :]
            ps, inv = [], []
            for half in range(2):
                sink = sink_ref[fa_index, c0 // HEAD_DIM + half]
                cols = []
                for j in range(nk // LANES):
                    blk = s[:, half * nk + j * LANES:half * nk + (j + 1) * LANES]
                    cols.append(blk + biases[j] if j in biases else blk)
                top = cols[0]
                for blk in cols[1:]:
                    top = jnp.maximum(top, blk)
                mx = jnp.maximum(jnp.max(top, axis=1, keepdims=True), sink)
                tot = None
                for blk in cols:
                    p = jnp.exp(blk - mx)
                    tot = p if tot is None else tot + p
                    ps.append(p.astype(BF16))
                den = jnp.sum(tot, axis=1, keepdims=True) + jnp.exp(sink - mx)
                inv.append(1.0 / den)
            p_rows.append(jnp.concatenate(ps, axis=1))
            inv_rows.append(jnp.where(low, inv[0], inv[1]))
        o_all = _dot(jnp.concatenate(p_rows, axis=0), v2)
        for pair, c0 in enumerate(col0):
            o = o_all[pair * tq:(pair + 1) * tq, :] * inv_rows[pair]
            o_ref[:, c0:c0 + LANES] = o.astype(o_ref.dtype)


def _attn_lat_kernel(fa_index, sink_ref, q_ref, kp_ref, kc_ref, kn_ref, vp_ref, vc_ref, vn_ref,
                     kx_ref, vx_ref, o_ref):
    n = pl.program_id(0)
    nb = pl.num_programs(0)
    k_band = jnp.concatenate([kp_ref[...], kc_ref[...], kn_ref[...], kx_ref[...]], axis=0)
    v_band = jnp.concatenate([vp_ref[...], vc_ref[...], vn_ref[...], vx_ref[...]], axis=0)
    qi = lax.broadcasted_iota(jnp.int32, (BLOCK, BLOCK), 0)
    kj = lax.broadcasted_iota(jnp.int32, (BLOCK, BLOCK), 1)
    bias_prev = jnp.where((kj >= qi) & (n > 0), 0.0, NEG_INF).astype(F32)
    bias_next = jnp.where((kj <= qi) & (n < nb - 1), 0.0, NEG_INF).astype(F32)
    _attention_core(q_ref, k_band, v_band, {0: bias_prev, 2: bias_next}, sink_ref, fa_index, o_ref)


def _attn_ctx_kernel(fa_index, sink_ref, q_ref, kx_ref, vx_ref, o_ref):
    _attention_core(q_ref, kx_ref[...], vx_ref[...], {}, sink_ref, fa_index, o_ref)


def _attention_lat(qkv, qkv_ctx, sink, fa_index):
    s = qkv.shape[0]
    n_ctx = qkv_ctx.shape[0]
    nb = s // BLOCK
    fw = PAIR_FORMS * LANES
    kcol = ATT_WIDTH // fw
    vcol = kcol + 1
    prev = lambda n: jnp.maximum(n - 1, 0)
    nxt = lambda n: jnp.minimum(n + 1, nb - 1)
    in_specs = [
        pl.BlockSpec(memory_space=pltpu.SMEM),
        pl.BlockSpec((BLOCK, ATT_WIDTH), lambda n: (n, 0)),
        pl.BlockSpec((BLOCK, fw), lambda n: (prev(n), kcol)),
        pl.BlockSpec((BLOCK, fw), lambda n: (n, kcol)),
        pl.BlockSpec((BLOCK, fw), lambda n: (nxt(n), kcol)),
        pl.BlockSpec((BLOCK, fw), lambda n: (prev(n), vcol)),
        pl.BlockSpec((BLOCK, fw), lambda n: (n, vcol)),
        pl.BlockSpec((BLOCK, fw), lambda n: (nxt(n), vcol)),
        pl.BlockSpec((n_ctx, fw), lambda n: (0, kcol)),
        pl.BlockSpec((n_ctx, fw), lambda n: (0, vcol)),
    ]
    return pl.pallas_call(
        functools.partial(_attn_lat_kernel, fa_index),
        grid=(nb,),
        in_specs=in_specs,
        out_specs=pl.BlockSpec((BLOCK, ATT_WIDTH), lambda n: (n, 0)),
        out_shape=jax.ShapeDtypeStruct((s, ATT_WIDTH), BF16),
        compiler_params=_params("arbitrary"),
        name="attention_latent",
    )(sink, qkv, qkv, qkv, qkv, qkv, qkv, qkv, qkv_ctx, qkv_ctx)


def _attention_ctx(qkv_ctx, sink, fa_index):
    n_ctx = qkv_ctx.shape[0]
    fw = PAIR_FORMS * LANES
    kcol = ATT_WIDTH // fw
    return pl.pallas_call(
        functools.partial(_attn_ctx_kernel, fa_index),
        grid=(1,),
        in_specs=[
            pl.BlockSpec(memory_space=pltpu.SMEM),
            pl.BlockSpec((n_ctx, ATT_WIDTH), lambda n: (0, 0)),
            pl.BlockSpec((n_ctx, fw), lambda n: (0, kcol)),
            pl.BlockSpec((n_ctx, fw), lambda n: (0, kcol + 1)),
        ],
        out_specs=pl.BlockSpec((n_ctx, ATT_WIDTH), lambda n: (0, 0)),
        out_shape=jax.ShapeDtypeStruct((n_ctx, ATT_WIDTH), BF16),
        compiler_params=_params("arbitrary"),
        name="attention_context",
    )(sink, qkv_ctx, qkv_ctx, qkv_ctx)


FFT_N1 = 64
FFT_N2 = 128


def _dft_cos_sin(n):
    idx = np.arange(n)
    ang = 2.0 * np.pi * ((idx[:, None] * idx[None, :]) % n) / n
    return np.cos(ang), np.sin(ang)


def _fourier_constants():
    c1, s1 = _dft_cos_sin(FFT_N1)
    stage1 = np.concatenate([c1, -s1], axis=0)
    c2, s2 = _dft_cos_sin(FFT_N2)
    stage2 = np.block([[c2, s2], [-s2, c2]])
    cc, sc = _dft_cos_sin(FG_W)
    chan = np.concatenate([cc, sc], axis=0)
    return (jnp.asarray(stage1, F32), jnp.asarray(stage2, F32), jnp.asarray(chan, F32))


def _twiddle_tables():
    k1 = jnp.arange(FFT_N1, dtype=jnp.int32)[:, None]
    n2 = jnp.arange(FFT_N2, dtype=jnp.int32)[None, :]
    ang = ((k1 * n2) % (FFT_N1 * FFT_N2)).astype(F32) * (2.0 * math.pi / (FFT_N1 * FFT_N2))
    wr = jnp.repeat(jnp.cos(ang), LANES, axis=1)
    wi = jnp.repeat(-jnp.sin(ang), LANES, axis=1)
    return wr, wi


def _fourier_stage1_kernel(x_ref, m_ref, wr_ref, wi_ref, t_ref):
    y = _dot(m_ref[...].astype(BF16), x_ref[...].astype(BF16))
    reps = F_WIDTH // LANES
    for b in range(x_ref.shape[1] // F_WIDTH):
        cols = slice(b * F_WIDTH, (b + 1) * F_WIDTH)
        yr = y[:FFT_N1, cols]
        yi = y[FFT_N1:, cols]
        wr = jnp.tile(wr_ref[:, b * LANES:(b + 1) * LANES], (1, reps))
        wi = jnp.tile(wi_ref[:, b * LANES:(b + 1) * LANES], (1, reps))
        t_ref[:, 2 * b * F_WIDTH:(2 * b + 1) * F_WIDTH] = (yr * wr - yi * wi).astype(t_ref.dtype)
        t_ref[:, (2 * b + 1) * F_WIDTH:(2 * b + 2) * F_WIDTH] = (yr * wi + yi * wr).astype(t_ref.dtype)


def _channel_stage(p, chan_ref, scale, o_ref):
    r = p.shape[0] // 2
    pr = p[:r].astype(BF16)
    pi = p[r:].astype(BF16)
    chan_c = chan_ref[:FG_W, :].astype(BF16)
    chan_s = chan_ref[FG_W:, :].astype(BF16)
    for g in range(N_FG):
        cols = slice(g * FG_W, (g + 1) * FG_W)
        y = _dot(pr[:, cols], chan_c) + _dot(pi[:, cols], chan_s)
        o_ref[:, cols] = (y * scale).astype(o_ref.dtype)


def _fourier_stage2_kernel(scale, t_ref, m_ref, chan_ref, o_ref):
    m = m_ref[...].astype(BF16)
    for kk in range(t_ref.shape[0]):
        t = jnp.concatenate([t_ref[kk, :, :F_WIDTH], t_ref[kk, :, F_WIDTH:]], axis=0)
        _channel_stage(_dot(m, t), chan_ref, scale, o_ref.at[:, kk * F_WIDTH:(kk + 1) * F_WIDTH])


def _fourier_ctx_kernel(scale, x_ref, m_ref, chan_ref, o_ref):
    _channel_stage(_dot(m_ref[...].astype(BF16), x_ref[...].astype(BF16)), chan_ref, scale, o_ref)


def _fourier_lat(f, consts, twiddles):
    n = f.shape[0]
    assert n == FFT_N1 * FFT_N2
    stage1, stage2, chan = consts
    wr, wi = twiddles
    n2_blk = 8
    cols = n2_blk * F_WIDTH
    full = lambda a: pl.BlockSpec(a.shape, lambda j: (0,) * a.ndim)
    t = pl.pallas_call(
        _fourier_stage1_kernel,
        grid=(FFT_N2 // n2_blk,),
        in_specs=[
            pl.BlockSpec((FFT_N1, cols), lambda j: (0, j)),
            full(stage1),
            pl.BlockSpec((FFT_N1, n2_blk * LANES), lambda j: (0, j)),
            pl.BlockSpec((FFT_N1, n2_blk * LANES), lambda j: (0, j)),
        ],
        out_specs=pl.BlockSpec((FFT_N1, 2 * cols), lambda j: (0, j)),
        out_shape=jax.ShapeDtypeStruct((FFT_N1, FFT_N2 * 2 * F_WIDTH), BF16),
        compiler_params=_params("arbitrary"),
        name="fourier_stage1",
    )(f.reshape(FFT_N1, FFT_N2 * F_WIDTH), stage1, wr, wi)
    t = t.reshape(FFT_N1, FFT_N2, 2 * F_WIDTH)
    scale = 1.0 / math.sqrt(n * FG_W)
    k1_blk = 4
    out = pl.pallas_call(
        functools.partial(_fourier_stage2_kernel, scale),
        grid=(FFT_N1 // k1_blk,),
        in_specs=[
            pl.BlockSpec((k1_blk, FFT_N2, 2 * F_WIDTH), lambda k1: (k1, 0, 0)),
            full(stage2),
            full(chan),
        ],
        out_specs=pl.BlockSpec((FFT_N2, k1_blk * F_WIDTH), lambda k1: (0, k1)),
        out_shape=jax.ShapeDtypeStruct((FFT_N2, FFT_N1 * F_WIDTH), BF16),
        compiler_params=_params("arbitrary"),
        name="fourier_stage2",
    )(t, stage2, chan)
    return out.reshape(n, F_WIDTH)


def _fourier_ctx(f, chan):
    n = f.shape[0]
    c, s = _dft_cos_sin(n)
    m = jnp.asarray(np.concatenate([c, -s], axis=0), F32)
    full = lambda a: pl.BlockSpec(a.shape, lambda j: (0,) * a.ndim)
    return pl.pallas_call(
        functools.partial(_fourier_ctx_kernel, 1.0 / math.sqrt(n * FG_W)),
        grid=(1,),
        in_specs=[full(f), full(m), full(chan)],
        out_specs=pl.BlockSpec((n, F_WIDTH), lambda j: (0, 0)),
        out_shape=jax.ShapeDtypeStruct((n, F_WIDTH), BF16),
        compiler_params=_params("arbitrary"),
        name="fourier_context",
    )(f, m, chan)


def _scan8(a, b, row, reverse):
    for d in (1, 2, 4):
        shift = (SUBLANES - d) if reverse else d
        keep = (row < SUBLANES - d) if reverse else (row >= d)
        b = jnp.where(keep, b + a * pltpu.roll(b, shift, axis=0), b)
        a = jnp.where(keep, a * pltpu.roll(a, shift, axis=0), a)
    return a, b


def _rows(x, n):
    return jnp.broadcast_to(x, (n, x.shape[1]))


def _rglru_kernel(reverse, combine, xs_ref, wa_ref, wi_ref, ba_ref, bi_ref, lam_ref, h0_ref, *refs):
    if combine:
        hb_ref, gg_ref, o_ref, last_ref, w_scr, a_scr, b_scr, c_scr, carry_scr = refs
    else:
        o_ref, last_ref, w_scr, a_scr, b_scr, c_scr, carry_scr = refs
    t = pl.program_id(1)
    tt, c = xs_ref.shape
    n_slab = c // LANES
    slabs_per_block = RNN_BLOCK // LANES
    groups = tt // SUBLANES
    blocks = groups // SUBLANES
    row = lax.broadcasted_iota(jnp.int32, (SUBLANES, LANES), 0)
    edge = 0 if reverse else SUBLANES - 1

    @pl.when(t == 0)
    def _():
        carry_scr[...] = jnp.broadcast_to(h0_ref[...], carry_scr.shape)
        w_scr[:, :, :RNN_BLOCK] = (0.5 * wa_ref[...]).astype(BF16)
        w_scr[:, :, RNN_BLOCK:] = (0.5 * wi_ref[...]).astype(BF16)

    pre = [_dot(xs_ref[:, nb * RNN_BLOCK:(nb + 1) * RNN_BLOCK].astype(BF16), w_scr[nb])
           for nb in range(c // RNN_BLOCK)]
    neg_lam = -lam_ref[...]
    softplus = jnp.maximum(neg_lam, 0.0) + jnp.log1p(jnp.exp(-jnp.abs(neg_lam)))
    k_all = softplus * (-0.5 * LRU_C * math.log2(math.e))
    ba_all = 0.5 * ba_ref[...]
    bi_all = 0.5 * bi_ref[...]

    for s in range(n_slab):
        lanes = slice(s * LANES, (s + 1) * LANES)
        k = _rows(k_all[:, lanes], SUBLANES)
        ba = _rows(ba_all[:, lanes], SUBLANES)
        bi = _rows(bi_all[:, lanes], SUBLANES)
        pre_b = pre[s // slabs_per_block]
        l0 = (s % slabs_per_block) * LANES
        for g in range(groups):
            r0 = g * SUBLANES
            tr = jnp.tanh(pre_b[r0:r0 + SUBLANES, l0:l0 + LANES] + ba)
            ti = jnp.tanh(pre_b[r0:r0 + SUBLANES, RNN_BLOCK + l0:RNN_BLOCK + l0 + LANES] + bi)
            a = jnp.exp2((1.0 + tr) * k)
            y = 1.0 - a * a
            root = jnp.where(y > 0.0, y * lax.rsqrt(y), 0.0)
            b = root * ((1.0 + ti) * (0.5 * xs_ref[r0:r0 + SUBLANES, lanes]))
            a_cum, b_loc = _scan8(a, b, row, reverse)
            a_scr[s, r0:r0 + SUBLANES, :] = a_cum
            b_scr[s, r0:r0 + SUBLANES, :] = b_loc

    for s in range(n_slab):
        a2 = a_scr[s, pl.ds(edge, groups, stride=SUBLANES), :]
        b2 = b_scr[s, pl.ds(edge, groups, stride=SUBLANES), :]
        carry = carry_scr[:, s * LANES:(s + 1) * LANES]
        enter_row = groups if reverse else SUBLANES - 1
        base = 0 if reverse else SUBLANES
        c_scr[s, enter_row:enter_row + 1, :] = carry[0:1, :]
        for j in (range(blocks - 1, -1, -1) if reverse else range(blocks)):
            r0 = j * SUBLANES
            a_cum, b_loc = _scan8(a2[r0:r0 + SUBLANES, :], b2[r0:r0 + SUBLANES, :], row, reverse)
            st = b_loc + a_cum * carry
            c_scr[s, base + r0:base + r0 + SUBLANES, :] = st
            carry = _rows(st[edge:edge + 1, :], SUBLANES)
        carry_scr[:, s * LANES:(s + 1) * LANES] = carry
        last_ref[:, s * LANES:(s + 1) * LANES] = carry[0:1, :]

    pair = 2 * SUBLANES
    for s in range(n_slab):
        lanes = slice(s * LANES, (s + 1) * LANES)
        for g in range(0, groups, 2):
            r0 = g * SUBLANES
            src = g + 1 if reverse else g + SUBLANES - 1
            enter = jnp.concatenate([_rows(c_scr[s, src:src + 1, :], SUBLANES),
                                     _rows(c_scr[s, src + 1:src + 2, :], SUBLANES)], axis=0)
            h = b_scr[s, r0:r0 + pair, :] + a_scr[s, r0:r0 + pair, :] * enter
            if combine:
                h = (h + hb_ref[r0:r0 + pair, lanes]) * gg_ref[r0:r0 + pair, lanes].astype(F32)
            o_ref[r0:r0 + pair, lanes] = h.astype(o_ref.dtype)


def _rglru_scan(xs, w_a, w_i, b_a, b_i, lam, rg_index, direction, h0, h_other=None, gelu_gate=None):
    m = xs.shape[0]
    tt = min(m, 1024)
    nt = m // tt
    c = SCAN_BLOCKS * RNN_BLOCK
    reverse = direction == 1
    combine = h_other is not None
    tix = (lambda t: nt - 1 - t) if reverse else (lambda t: t)
    wspec = pl.BlockSpec((None, None, SCAN_BLOCKS, RNN_BLOCK, RNN_BLOCK),
                         lambda cb, t: (rg_index, direction, cb, 0, 0))
    vspec = pl.BlockSpec((None, None, 1, c), lambda cb, t: (rg_index, direction, 0, cb))
    tile = pl.BlockSpec((tt, c), lambda cb, t: (tix(t), cb))
    in_specs = [tile, wspec, wspec, vspec, vspec, vspec, pl.BlockSpec((1, c), lambda cb, t: (0, cb))]
    args = [xs, w_a, w_i, b_a, b_i, lam, h0]
    if combine:
        in_specs += [tile, tile]
        args += [h_other, gelu_gate]
    return pl.pallas_call(
        functools.partial(_rglru_kernel, reverse, combine),
        grid=(D_RNN // c, nt),
        in_specs=in_specs,
        out_specs=[tile, pl.BlockSpec((1, c), lambda cb, t: (0, cb))],
        out_shape=[jax.ShapeDtypeStruct((m, D_RNN), BF16 if combine else F32),
                   jax.ShapeDtypeStruct((1, D_RNN), F32)],
        scratch_shapes=[pltpu.VMEM((SCAN_BLOCKS, RNN_BLOCK, 2 * RNN_BLOCK), BF16),
                        pltpu.VMEM((c // LANES, tt, LANES), F32),
                        pltpu.VMEM((c // LANES, tt, LANES), F32),
                        pltpu.VMEM((c // LANES, tt // SUBLANES + 2 * SUBLANES, LANES), F32),
                        pltpu.VMEM((SUBLANES, c), F32)],
        compiler_params=_params("arbitrary", "arbitrary"),
        name="rglru_scan",
    )(*args)


def _final_norm_kernel(x_ref, g_ref, o_ref):
    g = g_ref[...]
    for r0 in range(0, x_ref.shape[0], NORM_ROWS):
        x = x_ref[r0:r0 + NORM_ROWS, :]
        r = lax.rsqrt(jnp.mean(x * x, axis=-1, keepdims=True) + EPS)
        o_ref[r0:r0 + NORM_ROWS, :] = (x * r) * g


def _final_norm(x, g):
    m, d = x.shape
    tm = 512
    return pl.pallas_call(
        _final_norm_kernel,
        grid=(m // tm,),
        in_specs=[pl.BlockSpec((tm, d), lambda i: (i, 0)), pl.BlockSpec((1, d), lambda i: (0, 0))],
        out_specs=pl.BlockSpec((tm, d), lambda i: (i, 0)),
        out_shape=jax.ShapeDtypeStruct((m, d), F32),
        compiler_params=_params("arbitrary"),
        name="final_norm",
    )(x, g.reshape(1, d))


def _rope_tables(n):
    f = HEAD_DIM // 4
    inv = ROPE_BASE ** (-jnp.arange(f, dtype=F32) / f)
    pos = jnp.arange(n, dtype=jnp.int32)
    ang_r = (pos // GRID_W).astype(F32)[:, None] * inv[None, :]
    ang_c = (pos % GRID_W).astype(F32)[:, None] * inv[None, :]
    cr, sr, cc, sc = jnp.cos(ang_r), jnp.sin(ang_r), jnp.cos(ang_c), jnp.sin(ang_c)
    cos = jnp.concatenate([cr, cr, cc, cc], axis=1)
    sin = jnp.concatenate([-sr, sr, -sc, sc], axis=1)
    reps = LANES // HEAD_DIM
    return jnp.tile(cos, (1, reps)), jnp.tile(sin, (1, reps))


FUSED_OUT_NORM = {0: True, 1: True, 2: False, 3: False}


def _mixer_out(parts, w, w_index, x_lat, layer, mods, parts_ctx, x_ctx, g_ffn):
    if not FUSED_OUT_NORM[layer]:
        x_lat, x_ctx = _proj_residual(parts, w, w_index, x_lat, layer, mods, 2, 1024, 1024, parts_ctx, x_ctx)
        return x_lat, x_ctx, None, None
    x_lat, h_lat = _proj_residual_norm(parts, w, w_index, x_lat, layer, mods, 0, 2, g_ffn, 3, 4)
    h_ctx = None
    if parts_ctx is None:
        x_ctx = None
    else:
        x_ctx, h_ctx = _proj_residual_norm(parts_ctx, w, w_index, x_ctx, layer, mods, 1, 2, g_ffn, 3, 4)
    return x_lat, x_ctx, h_lat, h_ctx


def _fourier_attn_layer(layer, i, x_lat, x_ctx, h_lat, h_ctx, mods, fa_w_in, fa_w_out, attn_sink,
                        tables, ctx_out, g_ffn):
    rope, consts, twiddles = tables
    qkv_w = ATT_WIDTH + 2 * KV_WIDTH
    f_lat = _proj("plain", h_lat, fa_w_in, i, 0, F_WIDTH, F_WIDTH, F_WIDTH, BF16)
    qkv_lat = _proj("qkv_rope", h_lat, fa_w_in, i, F_WIDTH, qkv_w, qkv_w, 256, BF16, tm=512, extra=rope,
                    out_tn=QKV_OUT)
    qkv_ctx = _proj("qkv", h_ctx, fa_w_in, i, F_WIDTH, qkv_w, qkv_w, 256, BF16, out_tn=QKV_OUT)
    fo_lat = _fourier_lat(f_lat, consts, twiddles)
    ao_lat = _attention_lat(qkv_lat, qkv_ctx, attn_sink, i)
    parts_ctx = None
    if ctx_out:
        f_ctx = _proj("plain", h_ctx, fa_w_in, i, 0, F_WIDTH, F_WIDTH, F_WIDTH, BF16)
        parts_ctx = [_fourier_ctx(f_ctx, consts[2]), _attention_ctx(qkv_ctx, attn_sink, i)]
    return _mixer_out([fo_lat, ao_lat], fa_w_out, i, x_lat, layer, mods, parts_ctx, x_ctx, g_ffn)


def _rglru_layer(layer, i, x_lat, x_ctx, h_lat, h_ctx, mods, rg_w_in, rg_conv_w, rg_conv_b, w_a, b_a,
                 w_i, b_i, lam, rg_w_out, ctx_out, g_ffn):
    tn = 1024
    xcol = D_RNN // tn
    zero_state = jnp.zeros((1, D_RNN), F32)
    gate_lat = _proj("gelu", h_lat, rg_w_in, i, 0, D_RNN, tn, tn, BF16)
    conv = functools.partial(_conv_proj, w=rg_w_in, w_index=i, col_blocks=[xcol], conv_blocks=[0], n_out=D_RNN,
                             cw=rg_conv_w, cb=rg_conv_b, kw=CONV_W, left=CONV_LEFT, gated=False, tm=1024, tn=tn,
                             out_dtype=F32)
    xs_lat = conv(h_lat)
    xs_ctx = conv(h_ctx)
    scan = functools.partial(_rglru_scan, w_a=w_a, w_i=w_i, b_a=b_a, b_i=b_i, lam=lam, rg_index=i)
    hb_ctx, s_bwd = scan(xs_ctx, direction=1, h0=zero_state)
    parts_ctx = None
    if ctx_out:
        gate_ctx = _proj("gelu", h_ctx, rg_w_in, i, 0, D_RNN, tn, tn, BF16)
        y_ctx, s_fwd = scan(xs_ctx, direction=0, h0=zero_state, h_other=hb_ctx, gelu_gate=gate_ctx)
        parts_ctx = [y_ctx]
    else:
        _, s_fwd = scan(xs_ctx, direction=0, h0=zero_state)
    hb_lat, _ = scan(xs_lat, direction=1, h0=s_bwd)
    y_lat, _ = scan(xs_lat, direction=0, h0=s_fwd, h_other=hb_lat, gelu_gate=gate_lat)
    return _mixer_out([y_lat], rg_w_out, i, x_lat, layer, mods, parts_ctx, x_ctx, g_ffn)


def _conv_ffn(layer, x_lat, x_ctx, h_lat, h_ctx, g_ffn, mods, w_up, conv_w, conv_b, w_down):
    if h_lat is None:
        h_lat = _norm_mod(x_lat, g_ffn, layer, mods, 0, 3, 4)
        h_ctx = None if x_ctx is None else _norm_mod(x_ctx, g_ffn, layer, mods, 1, 3, 4)
    tf = 512
    blocks = [0, D_FF // tf]
    conv = functools.partial(_conv_proj, w=w_up, w_index=layer, col_blocks=blocks, conv_blocks=blocks, n_out=D_FF,
                             cw=conv_w, cb=conv_b, kw=FFN_CONV_W, left=FFN_CONV_LEFT, gated=True, tm=1024, tn=tf,
                             out_dtype=BF16)
    act_lat = conv(h_lat)
    parts_ctx = None if x_ctx is None else [conv(h_ctx)]
    return _proj_residual([act_lat], w_down, layer, x_lat, layer, mods, 5, 512, 512, parts_ctx, x_ctx)


def kernel(x, c, ctx, c_ctx, w_mod, b_mod, g_mix, g_ffn, fa_w_in, fa_w_out, attn_sink, rg_w_in, rg_conv_w,
           rg_conv_b, rg_w_a, rg_b_a, rg_w_i, rg_b_i, rg_lambda, rg_w_out, ffn_w_up, ffn_conv_w, ffn_conv_b,
           ffn_w_down, g_final):
    assert x.shape[0] == 1 and ctx.shape[0] == 1
    n = x.shape[1]
    x_lat = x[0]
    x_ctx = ctx[0]
    mods = _modulation(c, c_ctx, w_mod, b_mod)
    tables = (_rope_tables(n), _fourier_constants(), _twiddle_tables())
    g_mix3 = g_mix.reshape(DEPTH, 1, D_MODEL)
    g_ffn3 = g_ffn.reshape(DEPTH, 1, D_MODEL)
    n_rg = rg_conv_b.shape[0]
    rg_conv_b3 = rg_conv_b.reshape(n_rg, 1, D_RNN)
    rg_b_a4 = rg_b_a.reshape(n_rg, 2, 1, D_RNN)
    rg_b_i4 = rg_b_i.reshape(n_rg, 2, 1, D_RNN)
    rg_lam4 = rg_lambda.reshape(n_rg, 2, 1, D_RNN)
    ffn_conv_b3 = ffn_conv_b.reshape(DEPTH, 1, 2 * D_FF)
    for layer in range(DEPTH):
        ctx_out = layer < DEPTH - 1
        i = layer // 2
        h_lat = _norm_mod(x_lat, g_mix3, layer, mods, 0, 0, 1)
        h_ctx = _norm_mod(x_ctx, g_mix3, layer, mods, 1, 0, 1)
        if layer % 2 == 0:
            mixed = _fourier_attn_layer(layer, i, x_lat, x_ctx, h_lat, h_ctx, mods, fa_w_in, fa_w_out,
                                        attn_sink, tables, ctx_out, g_ffn3)
        else:
            mixed = _rglru_layer(layer, i, x_lat, x_ctx, h_lat, h_ctx, mods, rg_w_in, rg_conv_w, rg_conv_b3,
                                 rg_w_a, rg_b_a4, rg_w_i, rg_b_i4, rg_lam4, rg_w_out, ctx_out, g_ffn3)
        x_lat, x_ctx, hf_lat, hf_ctx = mixed
        x_lat, x_ctx = _conv_ffn(layer, x_lat, x_ctx if ctx_out else None, hf_lat, hf_ctx, g_ffn3, mods,
                                 ffn_w_up, ffn_conv_w, ffn_conv_b3, ffn_w_down)
    return _final_norm(x_lat, g_final)[None]
```

```python
import functools
import math

import numpy as np
import jax
import jax.numpy as jnp
from jax import lax
from jax.experimental import pallas as pl
from jax.experimental.pallas import tpu as pltpu

D_MODEL = 2048
DEPTH = 4
GRID_W = 64
N_FG = 4
FG_W = 256
F_WIDTH = N_FG * FG_W
N_HEADS = 16
N_KV_HEADS = 2
HEAD_DIM = 64
ATT_WIDTH = N_HEADS * HEAD_DIM
KV_WIDTH = N_KV_HEADS * HEAD_DIM
WINDOW = 128
BLOCK = 128
ROPE_BASE = 10000.0
D_RNN = D_MODEL
N_RNN_BLOCKS = 8
RNN_BLOCK = D_RNN // N_RNN_BLOCKS
CONV_W = 4
CONV_LEFT = 2
LRU_C = 8.0
D_FF = 5632
FFN_CONV_W = 3
FFN_CONV_LEFT = 1
N_MOD = 6
EPS = 1e-6
NEG_INF = -1e30

LANES = 128
SUBLANES = 8
HALO = 16
VMEM_LIMIT = 56 * 1024 * 1024
ROW_CHUNK = 64
PROJ_BLOCK = 1024
SCAN_BLOCKS = 2

BF16 = jnp.bfloat16
F32 = jnp.float32


def _params(*sem):
    return pltpu.CompilerParams(dimension_semantics=sem, vmem_limit_bytes=VMEM_LIMIT)


def _dot(a, b):
    return jnp.dot(a, b, preferred_element_type=F32)


def _dot_nt(a, b):
    return lax.dot_general(a, b, (((1,), (1,)), ((), ())), preferred_element_type=F32)


def _gelu_tanh(x):
    return 0.5 * x * (1.0 + jnp.tanh(math.sqrt(2.0 / math.pi) * (x + 0.044715 * (x * x * x))))


def _sigmoid(x):
    return 0.5 * (1.0 + jnp.tanh(0.5 * x))


def _mod_kernel(cl_ref, cc_ref, w_ref, b_ref, o_ref, sl_ref, sc_ref):
    @pl.when((pl.program_id(0) == 0) & (pl.program_id(1) == 0))
    def _():
        cl = cl_ref[...]
        cc = cc_ref[...]
        sl_ref[...] = cl * _sigmoid(cl)
        sc_ref[...] = cc * _sigmoid(cc)

    tn = w_ref.shape[1]
    reps = tn // LANES

    def body(kg, carry):
        al, ac = carry
        r0 = pl.multiple_of(kg * SUBLANES, SUBLANES)
        w8 = w_ref[pl.ds(r0, SUBLANES), :]
        s8l = jnp.tile(sl_ref[pl.ds(r0, SUBLANES), :], (1, reps))
        s8c = jnp.tile(sc_ref[pl.ds(r0, SUBLANES), :], (1, reps))
        return al + w8 * s8l, ac + w8 * s8c

    zero = jnp.zeros((SUBLANES, tn), F32)
    al, ac = lax.fori_loop(0, w_ref.shape[0] // SUBLANES, body, (zero, zero), unroll=4)
    b = b_ref[...]
    o_ref[0] = jnp.sum(al, axis=0, keepdims=True) + b
    o_ref[1] = jnp.sum(ac, axis=0, keepdims=True) + b


def _modulation(c, c_ctx, w_mod, b_mod):
    d = D_MODEL
    n = N_MOD * d
    tn = 1024
    cl = jnp.broadcast_to(c.reshape(d, 1), (d, LANES))
    cc = jnp.broadcast_to(c_ctx.reshape(d, 1), (d, LANES))
    return pl.pallas_call(
        _mod_kernel,
        grid=(DEPTH, n // tn),
        in_specs=[
            pl.BlockSpec((d, LANES), lambda l, j: (0, 0)),
            pl.BlockSpec((d, LANES), lambda l, j: (0, 0)),
            pl.BlockSpec((None, d, tn), lambda l, j: (l, 0, j)),
            pl.BlockSpec((None, 1, tn), lambda l, j: (l, 0, j)),
        ],
        out_specs=pl.BlockSpec((None, 2, 1, tn), lambda l, j: (l, 0, 0, j)),
        out_shape=jax.ShapeDtypeStruct((DEPTH, 2, 1, n), F32),
        scratch_shapes=[pltpu.VMEM((d, LANES), F32), pltpu.VMEM((d, LANES), F32)],
        compiler_params=_params("arbitrary", "arbitrary"),
        name="modulation",
    )(cl, cc, w_mod, b_mod.reshape(DEPTH, 1, n))


def _mod_spec(layer, row, k, tn, col_of):
    per = D_MODEL // tn
    return pl.BlockSpec((None, None, 1, tn), lambda *g: (layer, row, 0, k * per + col_of(*g)))


NORM_ROWS = 16


def _norm_mod_kernel(x_ref, g_ref, sh_ref, sc_ref, o_ref):
    g = g_ref[...]
    gain = g + g * sc_ref[...]
    shift = sh_ref[...]
    for r0 in range(0, x_ref.shape[0], NORM_ROWS):
        x = x_ref[r0:r0 + NORM_ROWS, :]
        r = lax.rsqrt(jnp.mean(x * x, axis=-1, keepdims=True) + EPS)
        o_ref[r0:r0 + NORM_ROWS, :] = ((x * r) * gain + shift).astype(o_ref.dtype)


def _norm_mod(x, g, layer, mods, row, k_shift, k_scale):
    m, d = x.shape
    tm = min(m, 1024)
    zero = lambda i: 0
    return pl.pallas_call(
        _norm_mod_kernel,
        grid=(m // tm,),
        in_specs=[
            pl.BlockSpec((tm, d), lambda i: (i, 0)),
            pl.BlockSpec((None, 1, d), lambda i: (layer, 0, 0)),
            _mod_spec(layer, row, k_shift, d, zero),
            _mod_spec(layer, row, k_scale, d, zero),
        ],
        out_specs=pl.BlockSpec((tm, d), lambda i: (i, 0)),
        out_shape=jax.ShapeDtypeStruct((m, d), BF16),
        compiler_params=_params("arbitrary"),
        name="norm_mod",
    )(x, g, mods, mods)


def _cast_weights(w_refs, wb_ref):
    @pl.when(pl.program_id(1) == 0)
    def _():
        c0 = 0
        for w_ref in w_refs:
            wn = w_ref.shape[1]
            wb_ref[:, c0:c0 + wn] = w_ref[...].astype(BF16)
            c0 += wn


def _proj_plain_kernel(n_w, blk, h_ref, *refs):
    o_ref, wb_ref = refs[n_w:]
    _cast_weights(refs[:n_w], wb_ref)
    o_ref[...] = _dot(h_ref[...], wb_ref[...]).astype(o_ref.dtype)


def _blocked_dot(lhs_block, rows, blk, epilogue):
    blk = min(rows, blk)
    for r0 in range(0, rows, blk):
        acc = lhs_block(r0, blk)
        for r in range(0, blk, ROW_CHUNK):
            epilogue(acc[r:r + ROW_CHUNK, :], r0 + r)


def _proj_gelu_kernel(n_w, blk, h_ref, *refs):
    o_ref, wb_ref = refs[n_w:]
    _cast_weights(refs[:n_w], wb_ref)

    def epilogue(acc, r):
        o_ref[r:r + ROW_CHUNK, :] = _gelu_tanh(acc).astype(o_ref.dtype)

    _blocked_dot(lambda r0, n: _dot(h_ref[r0:r0 + n, :], wb_ref[...]), h_ref.shape[0], blk, epilogue)


def _swap16(x, even):
    return jnp.where(even, pltpu.roll(x, LANES - 16, axis=1), pltpu.roll(x, 16, axis=1))


def _proj_qkv_kernel(rope, n_w, blk, h_ref, *refs):
    if rope:
        cos_ref, sin_ref, o_ref, wb_ref = refs[n_w:]
    else:
        o_ref, wb_ref = refs[n_w:]
    _cast_weights(refs[:n_w], wb_ref)
    q_scale = HEAD_DIM ** -0.5
    n_rot = (ATT_WIDTH + KV_WIDTH) // LANES
    even = (lax.broadcasted_iota(jnp.int32, (ROW_CHUNK, LANES), 1) & 16) == 0

    def epilogue(acc, r):
        if rope:
            cos = cos_ref[r:r + ROW_CHUNK, :]
            sin = sin_ref[r:r + ROW_CHUNK, :]
        for cidx in range(acc.shape[1] // LANES):
            t = acc[:, cidx * LANES:(cidx + 1) * LANES]
            if cidx < ATT_WIDTH // LANES:
                t = t * q_scale
            if rope and cidx < n_rot:
                t = t * cos + _swap16(t, even) * sin
            o_ref[r:r + ROW_CHUNK, cidx * LANES:(cidx + 1) * LANES] = t.astype(o_ref.dtype)

    _blocked_dot(lambda r0, n: _dot(h_ref[r0:r0 + n, :], wb_ref[...]), h_ref.shape[0], blk, epilogue)


def _proj(kind, h, w, w_index, col0, n, tn, wtn, out_dtype, tm=None, extra=(), blk=PROJ_BLOCK):
    m, k = h.shape
    tm = tm or min(m, 1024)
    assert col0 % wtn == 0 and tn % wtn == 0 and n % tn == 0 and m % tm == 0
    n_w = tn // wtn
    kernels = {
        "plain": _proj_plain_kernel,
        "gelu": _proj_gelu_kernel,
        "qkv": functools.partial(_proj_qkv_kernel, False),
        "qkv_rope": functools.partial(_proj_qkv_kernel, True),
    }
    in_specs = [pl.BlockSpec((tm, k), lambda j, i: (i, 0))]
    for p in range(n_w):
        in_specs.append(pl.BlockSpec((None, k, wtn),
                                     lambda j, i, p=p: (w_index, 0, col0 // wtn + j * n_w + p)))
    in_specs += [pl.BlockSpec((tm, LANES), lambda j, i: (i, 0)) for _ in extra]
    return pl.pallas_call(
        functools.partial(kernels[kind], n_w, blk),
        grid=(n // tn, m // tm),
        in_specs=in_specs,
        out_specs=pl.BlockSpec((tm, tn), lambda j, i: (i, j)),
        out_shape=jax.ShapeDtypeStruct((m, n), out_dtype),
        scratch_shapes=[pltpu.VMEM((k, tn), BF16)],
        compiler_params=_params("arbitrary", "arbitrary"),
        name="proj_" + kind,
    )(h, *([w] * n_w), *extra)


def _proj_res_kernel(n_parts, n_groups, n_lat, *refs):
    per = n_parts + 2
    groups = [refs[g * per:(g + 1) * per] for g in range(n_groups)]
    w_ref = refs[n_groups * per]
    o_refs = refs[n_groups * per + 1:n_groups * per + 1 + n_groups]
    wb_ref = refs[-1]
    _cast_weights([w_ref], wb_ref)

    def tile(group, o_ref):
        a_refs, (x_ref, gt_ref) = group[:n_parts], group[n_parts:]

        def lhs_block(r0, n):
            acc = None
            k0 = 0
            for a_ref in a_refs:
                kp = a_ref.shape[1]
                part = _dot(a_ref[r0:r0 + n, :], wb_ref[k0:k0 + kp, :])
                acc = part if acc is None else acc + part
                k0 += kp
            return acc

        def epilogue(acc, r):
            o_ref[r:r + ROW_CHUNK, :] = x_ref[r:r + ROW_CHUNK, :] + gt_ref[...] * acc

        _blocked_dot(lhs_block, x_ref.shape[0], PROJ_BLOCK, epilogue)

    i = pl.program_id(1)
    if n_groups == 1:
        tile(groups[0], o_refs[0])
    else:
        @pl.when(i < n_lat)
        def _():
            tile(groups[0], o_refs[0])

        @pl.when(i == n_lat)
        def _():
            tile(groups[1], o_refs[1])


def _proj_residual(parts, w, w_index, x, layer, mods, k_gate, tm, tn, parts_ctx=None, x_ctx=None):
    m, n = x.shape
    k = sum(p.shape[1] for p in parts)
    tm = min(m, tm)
    n_lat = m // tm
    lat_row = lambda j, i: jnp.minimum(i, n_lat - 1)
    in_specs = [pl.BlockSpec((tm, p.shape[1]), lambda j, i: (lat_row(j, i), 0)) for p in parts]
    in_specs += [pl.BlockSpec((tm, tn), lambda j, i: (lat_row(j, i), j)),
                 _mod_spec(layer, 0, k_gate, tn, lambda j, i: j)]
    args = [*parts, x, mods]
    out_specs = [pl.BlockSpec((tm, tn), lambda j, i: (lat_row(j, i), j))]
    out_shape = [jax.ShapeDtypeStruct((m, n), F32)]
    n_groups = 1
    if parts_ctx is not None:
        mc = x_ctx.shape[0]
        in_specs += [pl.BlockSpec((mc, p.shape[1]), lambda j, i: (0, 0)) for p in parts_ctx]
        in_specs += [pl.BlockSpec((mc, tn), lambda j, i: (0, j)),
                     _mod_spec(layer, 1, k_gate, tn, lambda j, i: j)]
        args += [*parts_ctx, x_ctx, mods]
        out_specs.append(pl.BlockSpec((mc, tn), lambda j, i: (0, j)))
        out_shape.append(jax.ShapeDtypeStruct((mc, n), F32))
        n_groups = 2
    in_specs.append(pl.BlockSpec((None, k, tn), lambda j, i: (w_index, 0, j)))
    args.append(w)
    outs = pl.pallas_call(
        functools.partial(_proj_res_kernel, len(parts), n_groups, n_lat),
        grid=(n // tn, n_lat + n_groups - 1),
        in_specs=in_specs,
        out_specs=out_specs,
        out_shape=out_shape,
        scratch_shapes=[pltpu.VMEM((k, tn), BF16)],
        compiler_params=_params("arbitrary", "arbitrary"),
        name="proj_residual",
    )(*args)
    return outs if n_groups == 2 else (outs[0], None)


def _proj_res_norm_kernel(n_parts, *refs):
    a_refs = refs[:n_parts]
    w_ref, x_ref, gt_ref, g_ref, sh_ref, sc_ref, o_ref, h_ref, wb_ref = refs[n_parts:]

    @pl.when(pl.program_id(0) == 0)
    def _():
        wb_ref[...] = w_ref[...].astype(BF16)

    acc = None
    k0 = 0
    for a_ref in a_refs:
        kp = a_ref.shape[1]
        part = _dot(a_ref[...], wb_ref[k0:k0 + kp, :])
        acc = part if acc is None else acc + part
        k0 += kp
    gate = gt_ref[...]
    g = g_ref[...]
    gain = g + g * sc_ref[...]
    shift = sh_ref[...]
    for r0 in range(0, x_ref.shape[0], NORM_ROWS):
        x = x_ref[r0:r0 + NORM_ROWS, :] + gate * acc[r0:r0 + NORM_ROWS, :]
        o_ref[r0:r0 + NORM_ROWS, :] = x
        r = lax.rsqrt(jnp.mean(x * x, axis=-1, keepdims=True) + EPS)
        h_ref[r0:r0 + NORM_ROWS, :] = ((x * r) * gain + shift).astype(h_ref.dtype)


def _proj_residual_norm(parts, w, w_index, x, layer, mods, row, k_gate, g, k_shift, k_scale):
    m, n = x.shape
    k = sum(p.shape[1] for p in parts)
    tm = min(m, 512)
    zero = lambda i: 0
    in_specs = [pl.BlockSpec((tm, p.shape[1]), lambda i: (i, 0)) for p in parts]
    in_specs += [
        pl.BlockSpec((None, k, n), lambda i: (w_index, 0, 0), pipeline_mode=pl.Buffered(1)),
        pl.BlockSpec((tm, n), lambda i: (i, 0)),
        _mod_spec(layer, row, k_gate, n, zero),
        pl.BlockSpec((None, 1, n), lambda i: (layer, 0, 0)),
        _mod_spec(layer, row, k_shift, n, zero),
        _mod_spec(layer, row, k_scale, n, zero),
    ]
    return pl.pallas_call(
        functools.partial(_proj_res_norm_kernel, len(parts)),
        grid=(m // tm,),
        in_specs=in_specs,
        out_specs=[pl.BlockSpec((tm, n), lambda i: (i, 0))] * 2,
        out_shape=[jax.ShapeDtypeStruct((m, n), F32), jax.ShapeDtypeStruct((m, n), BF16)],
        scratch_shapes=[pltpu.VMEM((k, n), BF16)],
        compiler_params=_params("arbitrary"),
        name="proj_residual_norm",
    )(*parts, w, x, mods, g, mods, mods)


def _conv_proj_kernel(n_w, kw, left, gated, h_ref, hp_ref, hn_ref, *refs):
    w_refs = refs[:n_w]
    cw_refs = refs[n_w:2 * n_w]
    cb_refs = refs[2 * n_w:3 * n_w]
    o_ref, wb_ref, ext_ref, z_ref = refs[3 * n_w:]
    i = pl.program_id(1)
    tm = h_ref.shape[0]
    tn = w_refs[0].shape[1]
    out_slabs = tn // LANES

    @pl.when(i == 0)
    def _():
        for p, w_ref in enumerate(w_refs):
            wb_ref[:, p * tn:(p + 1) * tn] = w_ref[...].astype(BF16)

    ext_ref[0:HALO, :] = jnp.where(i > 0, hp_ref[...], jnp.zeros_like(hp_ref))
    ext_ref[HALO:HALO + tm, :] = h_ref[...]
    ext_ref[HALO + tm:, :] = jnp.where(i < pl.num_programs(1) - 1, hn_ref[...], jnp.zeros_like(hn_ref))

    z = _dot(ext_ref[...], wb_ref[...])
    for t in range(n_w * out_slabs):
        z_ref[t] = z[:, t * LANES:(t + 1) * LANES]
    for so in range(out_slabs):
        c0 = so * LANES
        for r in range(0, tm, ROW_CHUNK):
            outs = []
            for p in range(n_w):
                u = cb_refs[p][:, c0:c0 + LANES]
                for tap in range(kw):
                    r0 = HALO - left + tap + r
                    u = u + z_ref[p * out_slabs + so, r0:r0 + ROW_CHUNK, :] * cw_refs[p][tap:tap + 1, c0:c0 + LANES]
                outs.append(u)
            if gated:
                g, v = outs
                res = g * _sigmoid(g) * v
            else:
                res = outs[0]
            o_ref[r:r + ROW_CHUNK, c0:c0 + LANES] = res.astype(o_ref.dtype)


def _conv_proj(h, w, w_index, col_blocks, conv_blocks, n_out, cw, cb, kw, left, gated, tm, tn, out_dtype):
    m, k = h.shape
    tm = min(m, tm)
    n_w = len(col_blocks)
    hb = m // HALO
    tb = tm // HALO
    in_specs = [
        pl.BlockSpec((tm, k), lambda j, i: (i, 0)),
        pl.BlockSpec((HALO, k), lambda j, i: (jnp.maximum(i * tb - 1, 0), 0)),
        pl.BlockSpec((HALO, k), lambda j, i: (jnp.minimum((i + 1) * tb, hb - 1), 0)),
    ]
    for c0 in col_blocks:
        in_specs.append(pl.BlockSpec((None, k, tn), lambda j, i, c0=c0: (w_index, 0, c0 + j)))
    for c0 in conv_blocks:
        in_specs.append(pl.BlockSpec((None, kw, tn), lambda j, i, c0=c0: (w_index, 0, c0 + j)))
    for c0 in conv_blocks:
        in_specs.append(pl.BlockSpec((None, 1, tn), lambda j, i, c0=c0: (w_index, 0, c0 + j)))
    args = [h, h, h] + [w] * n_w + [cw] * n_w + [cb] * n_w
    ext_rows = tm + 2 * HALO
    return pl.pallas_call(
        functools.partial(_conv_proj_kernel, n_w, kw, left, gated),
        grid=(n_out // tn, m // tm),
        in_specs=in_specs,
        out_specs=pl.BlockSpec((tm, tn), lambda j, i: (i, j)),
        out_shape=jax.ShapeDtypeStruct((m, n_out), out_dtype),
        scratch_shapes=[pltpu.VMEM((k, n_w * tn), BF16), pltpu.VMEM((ext_rows, k), BF16),
                        pltpu.VMEM((n_w * tn // LANES, ext_rows, LANES), F32)],
        compiler_params=_params("arbitrary", "arbitrary"),
        name="conv_proj",
    )(*args)


def _pair_operand(band, kv_head):
    b = band.astype(F32)
    rolled = pltpu.roll(b, HEAD_DIM, axis=1)
    low = lax.broadcasted_iota(jnp.int32, b.shape, 1) < HEAD_DIM
    zero = jnp.zeros_like(b)
    if kv_head == 0:
        top = jnp.where(low, b, zero)
        bot = jnp.where(low, zero, rolled)
    else:
        top = jnp.where(low, rolled, zero)
        bot = jnp.where(low, zero, b)
    return jnp.concatenate([top, bot], axis=0).astype(BF16)


def _attention_core(q_ref, k_band, v_band, biases, sink_ref, fa_index, o_ref):
    tq = q_ref.shape[0]
    nk = k_band.shape[0]
    low = lax.broadcasted_iota(jnp.int32, (tq, LANES), 1) < HEAD_DIM
    pairs = N_HEADS // N_KV_HEADS // 2
    for kv_head in range(N_KV_HEADS):
        k2 = _pair_operand(k_band, kv_head)
        v2 = _pair_operand(v_band, kv_head)
        col0 = [(kv_head * pairs + pair) * LANES for pair in range(pairs)]
        q_rows = jnp.concatenate([q_ref[:, c0:c0 + LANES] for c0 in col0], axis=0)
        s_all = _dot_nt(q_rows, k2)
        p_rows, inv_rows = [], []
        for pair, c0 in enumerate(col0):
            s = s_all[pair * tq:(pair + 1) * tq, :]
            ps, inv = [], []
            for half in range(2):
                sink = sink_ref[fa_index, c0 // HEAD_DIM + half]
                cols = []
                for j in range(nk // LANES):
                    blk = s[:, half * nk + j * LANES:half * nk + (j + 1) * LANES]
                    cols.append(blk + biases[j] if j in biases else blk)
                top = cols[0]
                for blk in cols[1:]:
                    top = jnp.maximum(top, blk)
                mx = jnp.maximum(jnp.max(top, axis=1, keepdims=True), sink)
                tot = None
                for blk in cols:
                    p = jnp.exp(blk - mx)
                    tot = p if tot is None else tot + p
                    ps.append(p.astype(BF16))
                den = jnp.sum(tot, axis=1, keepdims=True) + jnp.exp(sink - mx)
                inv.append(1.0 / den)
            p_rows.append(jnp.concatenate(ps, axis=1))
            inv_rows.append(jnp.where(low, inv[0], inv[1]))
        o_all = _dot(jnp.concatenate(p_rows, axis=0), v2)
        for pair, c0 in enumerate(col0):
            o = o_all[pair * tq:(pair + 1) * tq, :] * inv_rows[pair]
            o_ref[:, c0:c0 + LANES] = o.astype(o_ref.dtype)


def _attn_lat_kernel(fa_index, sink_ref, q_ref, kp_ref, kc_ref, kn_ref, vp_ref, vc_ref, vn_ref,
                     kx_ref, vx_ref, o_ref):
    n = pl.program_id(0)
    nb = pl.num_programs(0)
    k_band = jnp.concatenate([kp_ref[...], kc_ref[...], kn_ref[...], kx_ref[...]], axis=0)
    v_band = jnp.concatenate([vp_ref[...], vc_ref[...], vn_ref[...], vx_ref[...]], axis=0)
    qi = lax.broadcasted_iota(jnp.int32, (BLOCK, BLOCK), 0)
    kj = lax.broadcasted_iota(jnp.int32, (BLOCK, BLOCK), 1)
    bias_prev = jnp.where((kj >= qi) & (n > 0), 0.0, NEG_INF).astype(F32)
    bias_next = jnp.where((kj <= qi) & (n < nb - 1), 0.0, NEG_INF).astype(F32)
    _attention_core(q_ref, k_band, v_band, {0: bias_prev, 2: bias_next}, sink_ref, fa_index, o_ref)


def _attn_ctx_kernel(fa_index, sink_ref, q_ref, kx_ref, vx_ref, o_ref):
    _attention_core(q_ref, kx_ref[...], vx_ref[...], {}, sink_ref, fa_index, o_ref)


def _attention_lat(qkv, qkv_ctx, sink, fa_index):
    s = qkv.shape[0]
    n_ctx = qkv_ctx.shape[0]
    nb = s // BLOCK
    fw = LANES
    kcol = ATT_WIDTH // fw
    vcol = kcol + 1
    prev = lambda n: jnp.maximum(n - 1, 0)
    nxt = lambda n: jnp.minimum(n + 1, nb - 1)
    in_specs = [
        pl.BlockSpec(memory_space=pltpu.SMEM),
        pl.BlockSpec((BLOCK, ATT_WIDTH), lambda n: (n, 0)),
        pl.BlockSpec((BLOCK, fw), lambda n: (prev(n), kcol)),
        pl.BlockSpec((BLOCK, fw), lambda n: (n, kcol)),
        pl.BlockSpec((BLOCK, fw), lambda n: (nxt(n), kcol)),
        pl.BlockSpec((BLOCK, fw), lambda n: (prev(n), vcol)),
        pl.BlockSpec((BLOCK, fw), lambda n: (n, vcol)),
        pl.BlockSpec((BLOCK, fw), lambda n: (nxt(n), vcol)),
        pl.BlockSpec((n_ctx, fw), lambda n: (0, kcol)),
        pl.BlockSpec((n_ctx, fw), lambda n: (0, vcol)),
    ]
    return pl.pallas_call(
        functools.partial(_attn_lat_kernel, fa_index),
        grid=(nb,),
        in_specs=in_specs,
        out_specs=pl.BlockSpec((BLOCK, ATT_WIDTH), lambda n: (n, 0)),
        out_shape=jax.ShapeDtypeStruct((s, ATT_WIDTH), BF16),
        compiler_params=_params("arbitrary"),
        name="attention_latent",
    )(sink, qkv, qkv, qkv, qkv, qkv, qkv, qkv, qkv_ctx, qkv_ctx)


def _attention_ctx(qkv_ctx, sink, fa_index):
    n_ctx = qkv_ctx.shape[0]
    fw = LANES
    kcol = ATT_WIDTH // fw
    return pl.pallas_call(
        functools.partial(_attn_ctx_kernel, fa_index),
        grid=(1,),
        in_specs=[
            pl.BlockSpec(memory_space=pltpu.SMEM),
            pl.BlockSpec((n_ctx, ATT_WIDTH), lambda n: (0, 0)),
            pl.BlockSpec((n_ctx, fw), lambda n: (0, kcol)),
            pl.BlockSpec((n_ctx, fw), lambda n: (0, kcol + 1)),
        ],
        out_specs=pl.BlockSpec((n_ctx, ATT_WIDTH), lambda n: (0, 0)),
        out_shape=jax.ShapeDtypeStruct((n_ctx, ATT_WIDTH), BF16),
        compiler_params=_params("arbitrary"),
        name="attention_context",
    )(sink, qkv_ctx, qkv_ctx, qkv_ctx)


FFT_N1 = 64
FFT_N2 = 128


def _dft_cos_sin(n):
    idx = np.arange(n)
    ang = 2.0 * np.pi * ((idx[:, None] * idx[None, :]) % n) / n
    return np.cos(ang), np.sin(ang)


def _fourier_constants():
    c1, s1 = _dft_cos_sin(FFT_N1)
    stage1 = np.concatenate([c1, -s1], axis=0)
    c2, s2 = _dft_cos_sin(FFT_N2)
    stage2 = np.block([[c2, s2], [-s2, c2]])
    cc, sc = _dft_cos_sin(FG_W)
    chan = np.concatenate([cc, sc], axis=0)
    return (jnp.asarray(stage1, F32), jnp.asarray(stage2, F32), jnp.asarray(chan, F32))


def _twiddle_tables():
    k1 = jnp.arange(FFT_N1, dtype=jnp.int32)[:, None]
    n2 = jnp.arange(FFT_N2, dtype=jnp.int32)[None, :]
    ang = ((k1 * n2) % (FFT_N1 * FFT_N2)).astype(F32) * (2.0 * math.pi / (FFT_N1 * FFT_N2))
    wr = jnp.repeat(jnp.cos(ang), LANES, axis=1)
    wi = jnp.repeat(-jnp.sin(ang), LANES, axis=1)
    return wr, wi


def _fourier_stage1_kernel(x_ref, m_ref, wr_ref, wi_ref, t_ref):
    y = _dot(m_ref[...].astype(BF16), x_ref[...].astype(BF16))
    reps = F_WIDTH // LANES
    for b in range(x_ref.shape[1] // F_WIDTH):
        cols = slice(b * F_WIDTH, (b + 1) * F_WIDTH)
        yr = y[:FFT_N1, cols]
        yi = y[FFT_N1:, cols]
        wr = jnp.tile(wr_ref[:, b * LANES:(b + 1) * LANES], (1, reps))
        wi = jnp.tile(wi_ref[:, b * LANES:(b + 1) * LANES], (1, reps))
        t_ref[:, 2 * b * F_WIDTH:(2 * b + 1) * F_WIDTH] = (yr * wr - yi * wi).astype(t_ref.dtype)
        t_ref[:, (2 * b + 1) * F_WIDTH:(2 * b + 2) * F_WIDTH] = (yr * wi + yi * wr).astype(t_ref.dtype)


def _channel_stage(p, chan_ref, scale, o_ref):
    r = p.shape[0] // 2
    pr = p[:r].astype(BF16)
    pi = p[r:].astype(BF16)
    chan_c = chan_ref[:FG_W, :].astype(BF16)
    chan_s = chan_ref[FG_W:, :].astype(BF16)
    for g in range(N_FG):
        cols = slice(g * FG_W, (g + 1) * FG_W)
        y = _dot(pr[:, cols], chan_c) + _dot(pi[:, cols], chan_s)
        o_ref[:, cols] = (y * scale).astype(o_ref.dtype)


def _fourier_stage2_kernel(scale, t_ref, m_ref, chan_ref, o_ref):
    m = m_ref[...].astype(BF16)
    for kk in range(t_ref.shape[0]):
        t = jnp.concatenate([t_ref[kk, :, :F_WIDTH], t_ref[kk, :, F_WIDTH:]], axis=0)
        _channel_stage(_dot(m, t), chan_ref, scale, o_ref.at[:, kk * F_WIDTH:(kk + 1) * F_WIDTH])


def _fourier_ctx_kernel(scale, x_ref, m_ref, chan_ref, o_ref):
    _channel_stage(_dot(m_ref[...].astype(BF16), x_ref[...].astype(BF16)), chan_ref, scale, o_ref)


def _fourier_lat(f, consts, twiddles):
    n = f.shape[0]
    assert n == FFT_N1 * FFT_N2
    stage1, stage2, chan = consts
    wr, wi = twiddles
    n2_blk = 8
    cols = n2_blk * F_WIDTH
    full = lambda a: pl.BlockSpec(a.shape, lambda j: (0,) * a.ndim)
    t = pl.pallas_call(
        _fourier_stage1_kernel,
        grid=(FFT_N2 // n2_blk,),
        in_specs=[
            pl.BlockSpec((FFT_N1, cols), lambda j: (0, j)),
            full(stage1),
            pl.BlockSpec((FFT_N1, n2_blk * LANES), lambda j: (0, j)),
            pl.BlockSpec((FFT_N1, n2_blk * LANES), lambda j: (0, j)),
        ],
        out_specs=pl.BlockSpec((FFT_N1, 2 * cols), lambda j: (0, j)),
        out_shape=jax.ShapeDtypeStruct((FFT_N1, FFT_N2 * 2 * F_WIDTH), BF16),
        compiler_params=_params("arbitrary"),
        name="fourier_stage1",
    )(f.reshape(FFT_N1, FFT_N2 * F_WIDTH), stage1, wr, wi)
    t = t.reshape(FFT_N1, FFT_N2, 2 * F_WIDTH)
    scale = 1.0 / math.sqrt(n * FG_W)
    k1_blk = 4
    out = pl.pallas_call(
        functools.partial(_fourier_stage2_kernel, scale),
        grid=(FFT_N1 // k1_blk,),
        in_specs=[
            pl.BlockSpec((k1_blk, FFT_N2, 2 * F_WIDTH), lambda k1: (k1, 0, 0)),
            full(stage2),
            full(chan),
        ],
        out_specs=pl.BlockSpec((FFT_N2, k1_blk * F_WIDTH), lambda k1: (0, k1)),
        out_shape=jax.ShapeDtypeStruct((FFT_N2, FFT_N1 * F_WIDTH), BF16),
        compiler_params=_params("arbitrary"),
        name="fourier_stage2",
    )(t, stage2, chan)
    return out.reshape(n, F_WIDTH)


def _fourier_ctx(f, chan):
    n = f.shape[0]
    c, s = _dft_cos_sin(n)
    m = jnp.asarray(np.concatenate([c, -s], axis=0), F32)
    full = lambda a: pl.BlockSpec(a.shape, lambda j: (0,) * a.ndim)
    return pl.pallas_call(
        functools.partial(_fourier_ctx_kernel, 1.0 / math.sqrt(n * FG_W)),
        grid=(1,),
        in_specs=[full(f), full(m), full(chan)],
        out_specs=pl.BlockSpec((n, F_WIDTH), lambda j: (0, 0)),
        out_shape=jax.ShapeDtypeStruct((n, F_WIDTH), BF16),
        compiler_params=_params("arbitrary"),
        name="fourier_context",
    )(f, m, chan)


def _scan8(a, b, row, reverse):
    for d in (1, 2, 4):
        shift = (SUBLANES - d) if reverse else d
        keep = (row < SUBLANES - d) if reverse else (row >= d)
        b = jnp.where(keep, b + a * pltpu.roll(b, shift, axis=0), b)
        a = jnp.where(keep, a * pltpu.roll(a, shift, axis=0), a)
    return a, b


def _rows(x, n):
    return jnp.broadcast_to(x, (n, x.shape[1]))


def _rglru_kernel(reverse, combine, xs_ref, wa_ref, wi_ref, ba_ref, bi_ref, lam_ref, h0_ref, *refs):
    if combine:
        hb_ref, gg_ref, o_ref, last_ref, w_scr, a_scr, b_scr, c_scr, carry_scr = refs
    else:
        o_ref, last_ref, w_scr, a_scr, b_scr, c_scr, carry_scr = refs
    t = pl.program_id(1)
    tt, c = xs_ref.shape
    n_slab = c // LANES
    slabs_per_block = RNN_BLOCK // LANES
    groups = tt // SUBLANES
    blocks = groups // SUBLANES
    row = lax.broadcasted_iota(jnp.int32, (SUBLANES, LANES), 0)
    edge = 0 if reverse else SUBLANES - 1

    @pl.when(t == 0)
    def _():
        carry_scr[...] = jnp.broadcast_to(h0_ref[...], carry_scr.shape)
        w_scr[:, :, :RNN_BLOCK] = (0.5 * wa_ref[...]).astype(BF16)
        w_scr[:, :, RNN_BLOCK:] = (0.5 * wi_ref[...]).astype(BF16)

    pre = [_dot(xs_ref[:, nb * RNN_BLOCK:(nb + 1) * RNN_BLOCK].astype(BF16), w_scr[nb])
           for nb in range(c // RNN_BLOCK)]
    neg_lam = -lam_ref[...]
    softplus = jnp.maximum(neg_lam, 0.0) + jnp.log1p(jnp.exp(-jnp.abs(neg_lam)))
    k_all = softplus * (-0.5 * LRU_C * math.log2(math.e))
    ba_all = 0.5 * ba_ref[...]
    bi_all = 0.5 * bi_ref[...]

    for s in range(n_slab):
        lanes = slice(s * LANES, (s + 1) * LANES)
        k = _rows(k_all[:, lanes], SUBLANES)
        ba = _rows(ba_all[:, lanes], SUBLANES)
        bi = _rows(bi_all[:, lanes], SUBLANES)
        pre_b = pre[s // slabs_per_block]
        l0 = (s % slabs_per_block) * LANES
        for g in range(groups):
            r0 = g * SUBLANES
            tr = jnp.tanh(pre_b[r0:r0 + SUBLANES, l0:l0 + LANES] + ba)
            ti = jnp.tanh(pre_b[r0:r0 + SUBLANES, RNN_BLOCK + l0:RNN_BLOCK + l0 + LANES] + bi)
            a = jnp.exp2((1.0 + tr) * k)
            y = 1.0 - a * a
            root = jnp.where(y > 0.0, y * lax.rsqrt(y), 0.0)
            b = root * ((1.0 + ti) * (0.5 * xs_ref[r0:r0 + SUBLANES, lanes]))
            a_cum, b_loc = _scan8(a, b, row, reverse)
            a_scr[s, r0:r0 + SUBLANES, :] = a_cum
            b_scr[s, r0:r0 + SUBLANES, :] = b_loc

    for s in range(n_slab):
        a2 = a_scr[s, pl.ds(edge, groups, stride=SUBLANES), :]
        b2 = b_scr[s, pl.ds(edge, groups, stride=SUBLANES), :]
        carry = carry_scr[:, s * LANES:(s + 1) * LANES]
        enter_row = groups if reverse else SUBLANES - 1
        base = 0 if reverse else SUBLANES
        c_scr[s, enter_row:enter_row + 1, :] = carry[0:1, :]
        for j in (range(blocks - 1, -1, -1) if reverse else range(blocks)):
            r0 = j * SUBLANES
            a_cum, b_loc = _scan8(a2[r0:r0 + SUBLANES, :], b2[r0:r0 + SUBLANES, :], row, reverse)
            st = b_loc + a_cum * carry
            c_scr[s, base + r0:base + r0 + SUBLANES, :] = st
            carry = _rows(st[edge:edge + 1, :], SUBLANES)
        carry_scr[:, s * LANES:(s + 1) * LANES] = carry
        last_ref[:, s * LANES:(s + 1) * LANES] = carry[0:1, :]

    pair = 2 * SUBLANES
    for s in range(n_slab):
        lanes = slice(s * LANES, (s + 1) * LANES)
        for g in range(0, groups, 2):
            r0 = g * SUBLANES
            src = g + 1 if reverse else g + SUBLANES - 1
            enter = jnp.concatenate([_rows(c_scr[s, src:src + 1, :], SUBLANES),
                                     _rows(c_scr[s, src + 1:src + 2, :], SUBLANES)], axis=0)
            h = b_scr[s, r0:r0 + pair, :] + a_scr[s, r0:r0 + pair, :] * enter
            if combine:
                h = (h + hb_ref[r0:r0 + pair, lanes]) * gg_ref[r0:r0 + pair, lanes].astype(F32)
            o_ref[r0:r0 + pair, lanes] = h.astype(o_ref.dtype)


def _rglru_scan(xs, w_a, w_i, b_a, b_i, lam, rg_index, direction, h0, h_other=None, gelu_gate=None):
    m = xs.shape[0]
    tt = min(m, 1024)
    nt = m // tt
    c = SCAN_BLOCKS * RNN_BLOCK
    reverse = direction == 1
    combine = h_other is not None
    tix = (lambda t: nt - 1 - t) if reverse else (lambda t: t)
    wspec = pl.BlockSpec((None, None, SCAN_BLOCKS, RNN_BLOCK, RNN_BLOCK),
                         lambda cb, t: (rg_index, direction, cb, 0, 0))
    vspec = pl.BlockSpec((None, None, 1, c), lambda cb, t: (rg_index, direction, 0, cb))
    tile = pl.BlockSpec((tt, c), lambda cb, t: (tix(t), cb))
    in_specs = [tile, wspec, wspec, vspec, vspec, vspec, pl.BlockSpec((1, c), lambda cb, t: (0, cb))]
    args = [xs, w_a, w_i, b_a, b_i, lam, h0]
    if combine:
        in_specs += [tile, tile]
        args += [h_other, gelu_gate]
    return pl.pallas_call(
        functools.partial(_rglru_kernel, reverse, combine),
        grid=(D_RNN // c, nt),
        in_specs=in_specs,
        out_specs=[tile, pl.BlockSpec((1, c), lambda cb, t: (0, cb))],
        out_shape=[jax.ShapeDtypeStruct((m, D_RNN), BF16 if combine else F32),
                   jax.ShapeDtypeStruct((1, D_RNN), F32)],
        scratch_shapes=[pltpu.VMEM((SCAN_BLOCKS, RNN_BLOCK, 2 * RNN_BLOCK), BF16),
                        pltpu.VMEM((c // LANES, tt, LANES), F32),
                        pltpu.VMEM((c // LANES, tt, LANES), F32),
                        pltpu.VMEM((c // LANES, tt // SUBLANES + 2 * SUBLANES, LANES), F32),
                        pltpu.VMEM((SUBLANES, c), F32)],
        compiler_params=_params("arbitrary", "arbitrary"),
        name="rglru_scan",
    )(*args)


def _final_norm_kernel(x_ref, g_ref, o_ref):
    g = g_ref[...]
    for r0 in range(0, x_ref.shape[0], NORM_ROWS):
        x = x_ref[r0:r0 + NORM_ROWS, :]
        r = lax.rsqrt(jnp.mean(x * x, axis=-1, keepdims=True) + EPS)
        o_ref[r0:r0 + NORM_ROWS, :] = (x * r) * g


def _final_norm(x, g):
    m, d = x.shape
    tm = 512
    return pl.pallas_call(
        _final_norm_kernel,
        grid=(m // tm,),
        in_specs=[pl.BlockSpec((tm, d), lambda i: (i, 0)), pl.BlockSpec((1, d), lambda i: (0, 0))],
        out_specs=pl.BlockSpec((tm, d), lambda i: (i, 0)),
        out_shape=jax.ShapeDtypeStruct((m, d), F32),
        compiler_params=_params("arbitrary"),
        name="final_norm",
    )(x, g.reshape(1, d))


def _rope_tables(n):
    f = HEAD_DIM // 4
    inv = ROPE_BASE ** (-jnp.arange(f, dtype=F32) / f)
    pos = jnp.arange(n, dtype=jnp.int32)
    ang_r = (pos // GRID_W).astype(F32)[:, None] * inv[None, :]
    ang_c = (pos % GRID_W).astype(F32)[:, None] * inv[None, :]
    cr, sr, cc, sc = jnp.cos(ang_r), jnp.sin(ang_r), jnp.cos(ang_c), jnp.sin(ang_c)
    cos = jnp.concatenate([cr, cr, cc, cc], axis=1)
    sin = jnp.concatenate([-sr, sr, -sc, sc], axis=1)
    reps = LANES // HEAD_DIM
    return jnp.tile(cos, (1, reps)), jnp.tile(sin, (1, reps))


def _mixer_out(parts, w, w_index, x_lat, layer, mods, parts_ctx, x_ctx, g_ffn):
    x_lat, h_lat = _proj_residual_norm(parts, w, w_index, x_lat, layer, mods, 0, 2, g_ffn, 3, 4)
    h_ctx = None
    if parts_ctx is None:
        x_ctx = None
    else:
        x_ctx, h_ctx = _proj_residual_norm(parts_ctx, w, w_index, x_ctx, layer, mods, 1, 2, g_ffn, 3, 4)
    return x_lat, x_ctx, h_lat, h_ctx


def _fourier_attn_layer(layer, i, x_lat, x_ctx, h_lat, h_ctx, mods, fa_w_in, fa_w_out, attn_sink,
                        tables, ctx_out, g_ffn):
    rope, consts, twiddles = tables
    qkv_w = ATT_WIDTH + 2 * KV_WIDTH
    f_lat = _proj("plain", h_lat, fa_w_in, i, 0, F_WIDTH, F_WIDTH, F_WIDTH, BF16)
    qkv_lat = _proj("qkv_rope", h_lat, fa_w_in, i, F_WIDTH, qkv_w, qkv_w, 256, BF16, tm=512, extra=rope)
    qkv_ctx = _proj("qkv", h_ctx, fa_w_in, i, F_WIDTH, qkv_w, qkv_w, 256, BF16)
    fo_lat = _fourier_lat(f_lat, consts, twiddles)
    ao_lat = _attention_lat(qkv_lat, qkv_ctx, attn_sink, i)
    parts_ctx = None
    if ctx_out:
        f_ctx = _proj("plain", h_ctx, fa_w_in, i, 0, F_WIDTH, F_WIDTH, F_WIDTH, BF16)
        parts_ctx = [_fourier_ctx(f_ctx, consts[2]), _attention_ctx(qkv_ctx, attn_sink, i)]
    return _mixer_out([fo_lat, ao_lat], fa_w_out, i, x_lat, layer, mods, parts_ctx, x_ctx, g_ffn)


def _rglru_layer(layer, i, x_lat, x_ctx, h_lat, h_ctx, mods, rg_w_in, rg_conv_w, rg_conv_b, w_a, b_a,
                 w_i, b_i, lam, rg_w_out, ctx_out, g_ffn):
    tn = 1024
    xcol = D_RNN // tn
    zero_state = jnp.zeros((1, D_RNN), F32)
    gate_lat = _proj("gelu", h_lat, rg_w_in, i, 0, D_RNN, tn, tn, BF16)
    conv = functools.partial(_conv_proj, w=rg_w_in, w_index=i, col_blocks=[xcol], conv_blocks=[0], n_out=D_RNN,
                             cw=rg_conv_w, cb=rg_conv_b, kw=CONV_W, left=CONV_LEFT, gated=False, tm=1024, tn=tn,
                             out_dtype=F32)
    xs_lat = conv(h_lat)
    xs_ctx = conv(h_ctx)
    scan = functools.partial(_rglru_scan, w_a=w_a, w_i=w_i, b_a=b_a, b_i=b_i, lam=lam, rg_index=i)
    hb_ctx, s_bwd = scan(xs_ctx, direction=1, h0=zero_state)
    parts_ctx = None
    if ctx_out:
        gate_ctx = _proj("gelu", h_ctx, rg_w_in, i, 0, D_RNN, tn, tn, BF16)
        y_ctx, s_fwd = scan(xs_ctx, direction=0, h0=zero_state, h_other=hb_ctx, gelu_gate=gate_ctx)
        parts_ctx = [y_ctx]
    else:
        _, s_fwd = scan(xs_ctx, direction=0, h0=zero_state)
    hb_lat, _ = scan(xs_lat, direction=1, h0=s_bwd)
    y_lat, _ = scan(xs_lat, direction=0, h0=s_fwd, h_other=hb_lat, gelu_gate=gate_lat)
    return _mixer_out([y_lat], rg_w_out, i, x_lat, layer, mods, parts_ctx, x_ctx, g_ffn)


def _conv_ffn(layer, x_lat, x_ctx, h_lat, h_ctx, mods, w_up, conv_w, conv_b, w_down):
    tf = 512
    blocks = [0, D_FF // tf]
    conv = functools.partial(_conv_proj, w=w_up, w_index=layer, col_blocks=blocks, conv_blocks=blocks, n_out=D_FF,
                             cw=conv_w, cb=conv_b, kw=FFN_CONV_W, left=FFN_CONV_LEFT, gated=True, tm=1024, tn=tf,
                             out_dtype=BF16)
    act_lat = conv(h_lat)
    parts_ctx = None if x_ctx is None else [conv(h_ctx)]
    return _proj_residual([act_lat], w_down, layer, x_lat, layer, mods, 5, 512, 512, parts_ctx, x_ctx)


def kernel(x, c, ctx, c_ctx, w_mod, b_mod, g_mix, g_ffn, fa_w_in, fa_w_out, attn_sink, rg_w_in, rg_conv_w,
           rg_conv_b, rg_w_a, rg_b_a, rg_w_i, rg_b_i, rg_lambda, rg_w_out, ffn_w_up, ffn_conv_w, ffn_conv_b,
           ffn_w_down, g_final):
    assert x.shape[0] == 1 and ctx.shape[0] == 1
    n = x.shape[1]
    x_lat = x[0]
    x_ctx = ctx[0]
    mods = _modulation(c, c_ctx, w_mod, b_mod)
    tables = (_rope_tables(n), _fourier_constants(), _twiddle_tables())
    g_mix3 = g_mix.reshape(DEPTH, 1, D_MODEL)
    g_ffn3 = g_ffn.reshape(DEPTH, 1, D_MODEL)
    n_rg = rg_conv_b.shape[0]
    rg_conv_b3 = rg_conv_b.reshape(n_rg, 1, D_RNN)
    rg_b_a4 = rg_b_a.reshape(n_rg, 2, 1, D_RNN)
    rg_b_i4 = rg_b_i.reshape(n_rg, 2, 1, D_RNN)
    rg_lam4 = rg_lambda.reshape(n_rg, 2, 1, D_RNN)
    ffn_conv_b3 = ffn_conv_b.reshape(DEPTH, 1, 2 * D_FF)
    for layer in range(DEPTH):
        ctx_out = layer < DEPTH - 1
        i = layer // 2
        h_lat = _norm_mod(x_lat, g_mix3, layer, mods, 0, 0, 1)
        h_ctx = _norm_mod(x_ctx, g_mix3, layer, mods, 1, 0, 1)
        if layer % 2 == 0:
            mixed = _fourier_attn_layer(layer, i, x_lat, x_ctx, h_lat, h_ctx, mods, fa_w_in, fa_w_out,
                                        attn_sink, tables, ctx_out, g_ffn3)
        else:
            mixed = _rglru_layer(layer, i, x_lat, x_ctx, h_lat, h_ctx, mods, rg_w_in, rg_conv_w, rg_conv_b3,
                                 rg_w_a, rg_b_a4, rg_w_i, rg_b_i4, rg_lam4, rg_w_out, ctx_out, g_ffn3)
        x_lat, x_ctx, hf_lat, hf_ctx = mixed
        x_lat, x_ctx = _conv_ffn(layer, x_lat, x_ctx if ctx_out else None, hf_lat, hf_ctx, mods,
                                 ffn_w_up, ffn_conv_w, ffn_conv_b3, ffn_w_down)
    return _final_norm(x_lat, g_final)[None]
```

```python
import functools
import math

import numpy as np
import jax
import jax.numpy as jnp
from jax import lax
from jax.experimental import pallas as pl
from jax.experimental.pallas import tpu as pltpu

D_MODEL = 2048
DEPTH = 4
GRID_W = 64
N_FG = 4
FG_W = 256
F_WIDTH = N_FG * FG_W
N_HEADS = 16
N_KV_HEADS = 2
HEAD_DIM = 64
ATT_WIDTH = N_HEADS * HEAD_DIM
KV_WIDTH = N_KV_HEADS * HEAD_DIM
WINDOW = 128
BLOCK = 128
ROPE_BASE = 10000.0
D_RNN = D_MODEL
N_RNN_BLOCKS = 8
RNN_BLOCK = D_RNN // N_RNN_BLOCKS
CONV_W = 4
CONV_LEFT = 2
LRU_C = 8.0
D_FF = 5632
FFN_CONV_W = 3
FFN_CONV_LEFT = 1
N_MOD = 6
EPS = 1e-6
NEG_INF = -1e30

LANES = 128
SUBLANES = 8
HALO = 16
VMEM_LIMIT = 56 * 1024 * 1024
ROW_CHUNK = 64
PROJ_BLOCK = 1024
SCAN_BLOCKS = 4

BF16 = jnp.bfloat16
F32 = jnp.float32


def _params(*sem):
    return pltpu.CompilerParams(dimension_semantics=sem, vmem_limit_bytes=VMEM_LIMIT)


def _dot(a, b):
    return jnp.dot(a, b, preferred_element_type=F32)


def _dot_nt(a, b):
    return lax.dot_general(a, b, (((1,), (1,)), ((), ())), preferred_element_type=F32)


def _gelu_tanh(x):
    return 0.5 * x * (1.0 + jnp.tanh(math.sqrt(2.0 / math.pi) * (x + 0.044715 * (x * x * x))))


def _sigmoid(x):
    return 0.5 * (1.0 + jnp.tanh(0.5 * x))


def _mod_kernel(cl_ref, cc_ref, w_ref, b_ref, o_ref, sl_ref, sc_ref):
    @pl.when((pl.program_id(0) == 0) & (pl.program_id(1) == 0))
    def _():
        cl = cl_ref[...]
        cc = cc_ref[...]
        sl_ref[...] = cl * _sigmoid(cl)
        sc_ref[...] = cc * _sigmoid(cc)

    tn = w_ref.shape[1]
    reps = tn // LANES

    def body(kg, carry):
        al, ac = carry
        r0 = pl.multiple_of(kg * SUBLANES, SUBLANES)
        w8 = w_ref[pl.ds(r0, SUBLANES), :]
        s8l = jnp.tile(sl_ref[pl.ds(r0, SUBLANES), :], (1, reps))
        s8c = jnp.tile(sc_ref[pl.ds(r0, SUBLANES), :], (1, reps))
        return al + w8 * s8l, ac + w8 * s8c

    zero = jnp.zeros((SUBLANES, tn), F32)
    al, ac = lax.fori_loop(0, w_ref.shape[0] // SUBLANES, body, (zero, zero), unroll=4)
    b = b_ref[...]
    o_ref[0] = jnp.sum(al, axis=0, keepdims=True) + b
    o_ref[1] = jnp.sum(ac, axis=0, keepdims=True) + b


def _modulation(c, c_ctx, w_mod, b_mod):
    d = D_MODEL
    n = N_MOD * d
    tn = 1024
    cl = jnp.broadcast_to(c.reshape(d, 1), (d, LANES))
    cc = jnp.broadcast_to(c_ctx.reshape(d, 1), (d, LANES))
    return pl.pallas_call(
        _mod_kernel,
        grid=(DEPTH, n // tn),
        in_specs=[
            pl.BlockSpec((d, LANES), lambda l, j: (0, 0)),
            pl.BlockSpec((d, LANES), lambda l, j: (0, 0)),
            pl.BlockSpec((None, d, tn), lambda l, j: (l, 0, j)),
            pl.BlockSpec((None, 1, tn), lambda l, j: (l, 0, j)),
        ],
        out_specs=pl.BlockSpec((None, 2, 1, tn), lambda l, j: (l, 0, 0, j)),
        out_shape=jax.ShapeDtypeStruct((DEPTH, 2, 1, n), F32),
        scratch_shapes=[pltpu.VMEM((d, LANES), F32), pltpu.VMEM((d, LANES), F32)],
        compiler_params=_params("arbitrary", "arbitrary"),
        name="modulation",
    )(cl, cc, w_mod, b_mod.reshape(DEPTH, 1, n))


def _mod_spec(layer, row, k, tn, col_of):
    per = D_MODEL // tn
    return pl.BlockSpec((None, None, 1, tn), lambda *g: (layer, row, 0, k * per + col_of(*g)))


NORM_ROWS = 16


def _norm_mod_kernel(x_ref, g_ref, sh_ref, sc_ref, o_ref):
    g = g_ref[...]
    gain = g + g * sc_ref[...]
    shift = sh_ref[...]
    for r0 in range(0, x_ref.shape[0], NORM_ROWS):
        x = x_ref[r0:r0 + NORM_ROWS, :]
        r = lax.rsqrt(jnp.mean(x * x, axis=-1, keepdims=True) + EPS)
        o_ref[r0:r0 + NORM_ROWS, :] = ((x * r) * gain + shift).astype(o_ref.dtype)


def _norm_mod(x, g, layer, mods, row, k_shift, k_scale):
    m, d = x.shape
    tm = min(m, 1024)
    zero = lambda i: 0
    return pl.pallas_call(
        _norm_mod_kernel,
        grid=(m // tm,),
        in_specs=[
            pl.BlockSpec((tm, d), lambda i: (i, 0)),
            pl.BlockSpec((None, 1, d), lambda i: (layer, 0, 0)),
            _mod_spec(layer, row, k_shift, d, zero),
            _mod_spec(layer, row, k_scale, d, zero),
        ],
        out_specs=pl.BlockSpec((tm, d), lambda i: (i, 0)),
        out_shape=jax.ShapeDtypeStruct((m, d), BF16),
        compiler_params=_params("arbitrary"),
        name="norm_mod",
    )(x, g, mods, mods)


def _cast_weights(w_refs, wb_ref):
    @pl.when(pl.program_id(1) == 0)
    def _():
        c0 = 0
        for w_ref in w_refs:
            wn = w_ref.shape[1]
            wb_ref[:, c0:c0 + wn] = w_ref[...].astype(BF16)
            c0 += wn


def _proj_plain_kernel(n_w, blk, h_ref, *refs):
    o_ref, wb_ref = refs[n_w:]
    _cast_weights(refs[:n_w], wb_ref)
    o_ref[...] = _dot(h_ref[...], wb_ref[...]).astype(o_ref.dtype)


def _blocked_dot(lhs_block, rows, blk, epilogue):
    blk = min(rows, blk)
    for r0 in range(0, rows, blk):
        acc = lhs_block(r0, blk)
        for r in range(0, blk, ROW_CHUNK):
            epilogue(acc[r:r + ROW_CHUNK, :], r0 + r)


def _proj_gelu_kernel(n_w, blk, h_ref, *refs):
    o_ref, wb_ref = refs[n_w:]
    _cast_weights(refs[:n_w], wb_ref)

    def epilogue(acc, r):
        o_ref[r:r + ROW_CHUNK, :] = _gelu_tanh(acc).astype(o_ref.dtype)

    _blocked_dot(lambda r0, n: _dot(h_ref[r0:r0 + n, :], wb_ref[...]), h_ref.shape[0], blk, epilogue)


def _swap16(x, even):
    return jnp.where(even, pltpu.roll(x, LANES - 16, axis=1), pltpu.roll(x, 16, axis=1))


def _proj_qkv_kernel(rope, n_w, blk, h_ref, *refs):
    if rope:
        cos_ref, sin_ref, o_ref, wb_ref = refs[n_w:]
    else:
        o_ref, wb_ref = refs[n_w:]
    _cast_weights(refs[:n_w], wb_ref)
    q_scale = HEAD_DIM ** -0.5
    n_rot = (ATT_WIDTH + KV_WIDTH) // LANES
    even = (lax.broadcasted_iota(jnp.int32, (ROW_CHUNK, LANES), 1) & 16) == 0

    def epilogue(acc, r):
        if rope:
            cos = cos_ref[r:r + ROW_CHUNK, :]
            sin = sin_ref[r:r + ROW_CHUNK, :]
        for cidx in range(acc.shape[1] // LANES):
            t = acc[:, cidx * LANES:(cidx + 1) * LANES]
            if cidx < ATT_WIDTH // LANES:
                t = t * q_scale
            if rope and cidx < n_rot:
                t = t * cos + _swap16(t, even) * sin
            o_ref[r:r + ROW_CHUNK, cidx * LANES:(cidx + 1) * LANES] = t.astype(o_ref.dtype)

    _blocked_dot(lambda r0, n: _dot(h_ref[r0:r0 + n, :], wb_ref[...]), h_ref.shape[0], blk, epilogue)


def _proj(kind, h, w, w_index, col0, n, tn, wtn, out_dtype, tm=None, extra=(), blk=PROJ_BLOCK):
    m, k = h.shape
    tm = tm or min(m, 1024)
    assert col0 % wtn == 0 and tn % wtn == 0 and n % tn == 0 and m % tm == 0
    n_w = tn // wtn
    kernels = {
        "plain": _proj_plain_kernel,
        "gelu": _proj_gelu_kernel,
        "qkv": functools.partial(_proj_qkv_kernel, False),
        "qkv_rope": functools.partial(_proj_qkv_kernel, True),
    }
    in_specs = [pl.BlockSpec((tm, k), lambda j, i: (i, 0))]
    for p in range(n_w):
        in_specs.append(pl.BlockSpec((None, k, wtn),
                                     lambda j, i, p=p: (w_index, 0, col0 // wtn + j * n_w + p)))
    in_specs += [pl.BlockSpec((tm, LANES), lambda j, i: (i, 0)) for _ in extra]
    return pl.pallas_call(
        functools.partial(kernels[kind], n_w, blk),
        grid=(n // tn, m // tm),
        in_specs=in_specs,
        out_specs=pl.BlockSpec((tm, tn), lambda j, i: (i, j)),
        out_shape=jax.ShapeDtypeStruct((m, n), out_dtype),
        scratch_shapes=[pltpu.VMEM((k, tn), BF16)],
        compiler_params=_params("arbitrary", "arbitrary"),
        name="proj_" + kind,
    )(h, *([w] * n_w), *extra)


def _proj_res_kernel(n_parts, n_groups, n_lat, *refs):
    per = n_parts + 2
    groups = [refs[g * per:(g + 1) * per] for g in range(n_groups)]
    w_ref = refs[n_groups * per]
    o_refs = refs[n_groups * per + 1:n_groups * per + 1 + n_groups]
    wb_ref = refs[-1]
    _cast_weights([w_ref], wb_ref)

    def tile(group, o_ref):
        a_refs, (x_ref, gt_ref) = group[:n_parts], group[n_parts:]

        def lhs_block(r0, n):
            acc = None
            k0 = 0
            for a_ref in a_refs:
                kp = a_ref.shape[1]
                part = _dot(a_ref[r0:r0 + n, :], wb_ref[k0:k0 + kp, :])
                acc = part if acc is None else acc + part
                k0 += kp
            return acc

        def epilogue(acc, r):
            o_ref[r:r + ROW_CHUNK, :] = x_ref[r:r + ROW_CHUNK, :] + gt_ref[...] * acc

        _blocked_dot(lhs_block, x_ref.shape[0], PROJ_BLOCK, epilogue)

    i = pl.program_id(1)
    if n_groups == 1:
        tile(groups[0], o_refs[0])
    else:
        @pl.when(i < n_lat)
        def _():
            tile(groups[0], o_refs[0])

        @pl.when(i == n_lat)
        def _():
            tile(groups[1], o_refs[1])


def _proj_residual(parts, w, w_index, x, layer, mods, k_gate, tm, tn, parts_ctx=None, x_ctx=None):
    m, n = x.shape
    k = sum(p.shape[1] for p in parts)
    tm = min(m, tm)
    n_lat = m // tm
    lat_row = lambda j, i: jnp.minimum(i, n_lat - 1)
    in_specs = [pl.BlockSpec((tm, p.shape[1]), lambda j, i: (lat_row(j, i), 0)) for p in parts]
    in_specs += [pl.BlockSpec((tm, tn), lambda j, i: (lat_row(j, i), j)),
                 _mod_spec(layer, 0, k_gate, tn, lambda j, i: j)]
    args = [*parts, x, mods]
    out_specs = [pl.BlockSpec((tm, tn), lambda j, i: (lat_row(j, i), j))]
    out_shape = [jax.ShapeDtypeStruct((m, n), F32)]
    n_groups = 1
    if parts_ctx is not None:
        mc = x_ctx.shape[0]
        in_specs += [pl.BlockSpec((mc, p.shape[1]), lambda j, i: (0, 0)) for p in parts_ctx]
        in_specs += [pl.BlockSpec((mc, tn), lambda j, i: (0, j)),
                     _mod_spec(layer, 1, k_gate, tn, lambda j, i: j)]
        args += [*parts_ctx, x_ctx, mods]
        out_specs.append(pl.BlockSpec((mc, tn), lambda j, i: (0, j)))
        out_shape.append(jax.ShapeDtypeStruct((mc, n), F32))
        n_groups = 2
    in_specs.append(pl.BlockSpec((None, k, tn), lambda j, i: (w_index, 0, j)))
    args.append(w)
    outs = pl.pallas_call(
        functools.partial(_proj_res_kernel, len(parts), n_groups, n_lat),
        grid=(n // tn, n_lat + n_groups - 1),
        in_specs=in_specs,
        out_specs=out_specs,
        out_shape=out_shape,
        scratch_shapes=[pltpu.VMEM((k, tn), BF16)],
        compiler_params=_params("arbitrary", "arbitrary"),
        name="proj_residual",
    )(*args)
    return outs if n_groups == 2 else (outs[0], None)


def _proj_res_norm_kernel(n_parts, *refs):
    a_refs = refs[:n_parts]
    w_ref, x_ref, gt_ref, g_ref, sh_ref, sc_ref, o_ref, h_ref, wb_ref = refs[n_parts:]

    @pl.when(pl.program_id(0) == 0)
    def _():
        wb_ref[...] = w_ref[...].astype(BF16)

    acc = None
    k0 = 0
    for a_ref in a_refs:
        kp = a_ref.shape[1]
        part = _dot(a_ref[...], wb_ref[k0:k0 + kp, :])
        acc = part if acc is None else acc + part
        k0 += kp
    gate = gt_ref[...]
    g = g_ref[...]
    gain = g + g * sc_ref[...]
    shift = sh_ref[...]
    for r0 in range(0, x_ref.shape[0], NORM_ROWS):
        x = x_ref[r0:r0 + NORM_ROWS, :] + gate * acc[r0:r0 + NORM_ROWS, :]
        o_ref[r0:r0 + NORM_ROWS, :] = x
        r = lax.rsqrt(jnp.mean(x * x, axis=-1, keepdims=True) + EPS)
        h_ref[r0:r0 + NORM_ROWS, :] = ((x * r) * gain + shift).astype(h_ref.dtype)


def _proj_residual_norm(parts, w, w_index, x, layer, mods, row, k_gate, g, k_shift, k_scale):
    m, n = x.shape
    k = sum(p.shape[1] for p in parts)
    tm = min(m, 512)
    zero = lambda i: 0
    in_specs = [pl.BlockSpec((tm, p.shape[1]), lambda i: (i, 0)) for p in parts]
    in_specs += [
        pl.BlockSpec((None, k, n), lambda i: (w_index, 0, 0), pipeline_mode=pl.Buffered(1)),
        pl.BlockSpec((tm, n), lambda i: (i, 0)),
        _mod_spec(layer, row, k_gate, n, zero),
        pl.BlockSpec((None, 1, n), lambda i: (layer, 0, 0)),
        _mod_spec(layer, row, k_shift, n, zero),
        _mod_spec(layer, row, k_scale, n, zero),
    ]
    return pl.pallas_call(
        functools.partial(_proj_res_norm_kernel, len(parts)),
        grid=(m // tm,),
        in_specs=in_specs,
        out_specs=[pl.BlockSpec((tm, n), lambda i: (i, 0))] * 2,
        out_shape=[jax.ShapeDtypeStruct((m, n), F32), jax.ShapeDtypeStruct((m, n), BF16)],
        scratch_shapes=[pltpu.VMEM((k, n), BF16)],
        compiler_params=_params("arbitrary"),
        name="proj_residual_norm",
    )(*parts, w, x, mods, g, mods, mods)


def _conv_proj_kernel(n_w, kw, left, gated, h_ref, hp_ref, hn_ref, *refs):
    w_refs = refs[:n_w]
    cw_refs = refs[n_w:2 * n_w]
    cb_refs = refs[2 * n_w:3 * n_w]
    o_ref, wb_ref, ext_ref, z_ref = refs[3 * n_w:]
    i = pl.program_id(1)
    tm = h_ref.shape[0]
    tn = w_refs[0].shape[1]
    out_slabs = tn // LANES

    @pl.when(i == 0)
    def _():
        for p, w_ref in enumerate(w_refs):
            wb_ref[:, p * tn:(p + 1) * tn] = w_ref[...].astype(BF16)

    ext_ref[0:HALO, :] = jnp.where(i > 0, hp_ref[...], jnp.zeros_like(hp_ref))
    ext_ref[HALO:HALO + tm, :] = h_ref[...]
    ext_ref[HALO + tm:, :] = jnp.where(i < pl.num_programs(1) - 1, hn_ref[...], jnp.zeros_like(hn_ref))

    z = _dot(ext_ref[...], wb_ref[...])
    for t in range(n_w * out_slabs):
        z_ref[t] = z[:, t * LANES:(t + 1) * LANES]
    for so in range(out_slabs):
        c0 = so * LANES
        for r in range(0, tm, ROW_CHUNK):
            outs = []
            for p in range(n_w):
                u = cb_refs[p][:, c0:c0 + LANES]
                for tap in range(kw):
                    r0 = HALO - left + tap + r
                    u = u + z_ref[p * out_slabs + so, r0:r0 + ROW_CHUNK, :] * cw_refs[p][tap:tap + 1, c0:c0 + LANES]
                outs.append(u)
            if gated:
                g, v = outs
                res = g * _sigmoid(g) * v
            else:
                res = outs[0]
            o_ref[r:r + ROW_CHUNK, c0:c0 + LANES] = res.astype(o_ref.dtype)


def _conv_proj(h, w, w_index, col_blocks, conv_blocks, n_out, cw, cb, kw, left, gated, tm, tn, out_dtype):
    m, k = h.shape
    tm = min(m, tm)
    n_w = len(col_blocks)
    hb = m // HALO
    tb = tm // HALO
    in_specs = [
        pl.BlockSpec((tm, k), lambda j, i: (i, 0)),
        pl.BlockSpec((HALO, k), lambda j, i: (jnp.maximum(i * tb - 1, 0), 0)),
        pl.BlockSpec((HALO, k), lambda j, i: (jnp.minimum((i + 1) * tb, hb - 1), 0)),
    ]
    for c0 in col_blocks:
        in_specs.append(pl.BlockSpec((None, k, tn), lambda j, i, c0=c0: (w_index, 0, c0 + j)))
    for c0 in conv_blocks:
        in_specs.append(pl.BlockSpec((None, kw, tn), lambda j, i, c0=c0: (w_index, 0, c0 + j)))
    for c0 in conv_blocks:
        in_specs.append(pl.BlockSpec((None, 1, tn), lambda j, i, c0=c0: (w_index, 0, c0 + j)))
    args = [h, h, h] + [w] * n_w + [cw] * n_w + [cb] * n_w
    ext_rows = tm + 2 * HALO
    return pl.pallas_call(
        functools.partial(_conv_proj_kernel, n_w, kw, left, gated),
        grid=(n_out // tn, m // tm),
        in_specs=in_specs,
        out_specs=pl.BlockSpec((tm, tn), lambda j, i: (i, j)),
        out_shape=jax.ShapeDtypeStruct((m, n_out), out_dtype),
        scratch_shapes=[pltpu.VMEM((k, n_w * tn), BF16), pltpu.VMEM((ext_rows, k), BF16),
                        pltpu.VMEM((n_w * tn // LANES, ext_rows, LANES), F32)],
        compiler_params=_params("arbitrary", "arbitrary"),
        name="conv_proj",
    )(*args)


def _pair_operand(band, kv_head):
    b = band.astype(F32)
    rolled = pltpu.roll(b, HEAD_DIM, axis=1)
    low = lax.broadcasted_iota(jnp.int32, b.shape, 1) < HEAD_DIM
    zero = jnp.zeros_like(b)
    if kv_head == 0:
        top = jnp.where(low, b, zero)
        bot = jnp.where(low, zero, rolled)
    else:
        top = jnp.where(low, rolled, zero)
        bot = jnp.where(low, zero, b)
    return jnp.concatenate([top, bot], axis=0).astype(BF16)


def _attention_core(q_ref, k_band, v_band, biases, sink_ref, fa_index, o_ref):
    tq = q_ref.shape[0]
    nk = k_band.shape[0]
    low = lax.broadcasted_iota(jnp.int32, (tq, LANES), 1) < HEAD_DIM
    pairs = N_HEADS // N_KV_HEADS // 2
    for kv_head in range(N_KV_HEADS):
        k2 = _pair_operand(k_band, kv_head)
        v2 = _pair_operand(v_band, kv_head)
        col0 = [(kv_head * pairs + pair) * LANES for pair in range(pairs)]
        q_rows = jnp.concatenate([q_ref[:, c0:c0 + LANES] for c0 in col0], axis=0)
        s_all = _dot_nt(q_rows, k2)
        p_rows, inv_rows = [], []
        for pair, c0 in enumerate(col0):
            s = s_all[pair * tq:(pair + 1) * tq, :]
            ps, inv = [], []
            for half in range(2):
                sink = sink_ref[fa_index, c0 // HEAD_DIM + half]
                cols = []
                for j in range(nk // LANES):
                    blk = s[:, half * nk + j * LANES:half * nk + (j + 1) * LANES]
                    cols.append(blk + biases[j] if j in biases else blk)
                top = cols[0]
                for blk in cols[1:]:
                    top = jnp.maximum(top, blk)
                mx = jnp.maximum(jnp.max(top, axis=1, keepdims=True), sink)
                tot = None
                for blk in cols:
                    p = jnp.exp(blk - mx)
                    tot = p if tot is None else tot + p
                    ps.append(p.astype(BF16))
                den = jnp.sum(tot, axis=1, keepdims=True) + jnp.exp(sink - mx)
                inv.append(1.0 / den)
            p_rows.append(jnp.concatenate(ps, axis=1))
            inv_rows.append(jnp.where(low, inv[0], inv[1]))
        o_all = _dot(jnp.concatenate(p_rows, axis=0), v2)
        for pair, c0 in enumerate(col0):
            o = o_all[pair * tq:(pair + 1) * tq, :] * inv_rows[pair]
            o_ref[:, c0:c0 + LANES] = o.astype(o_ref.dtype)


def _attn_lat_kernel(fa_index, sink_ref, q_ref, kp_ref, kc_ref, kn_ref, vp_ref, vc_ref, vn_ref,
                     kx_ref, vx_ref, o_ref):
    n = pl.program_id(0)
    nb = pl.num_programs(0)
    k_band = jnp.concatenate([kp_ref[...], kc_ref[...], kn_ref[...], kx_ref[...]], axis=0)
    v_band = jnp.concatenate([vp_ref[...], vc_ref[...], vn_ref[...], vx_ref[...]], axis=0)
    qi = lax.broadcasted_iota(jnp.int32, (BLOCK, BLOCK), 0)
    kj = lax.broadcasted_iota(jnp.int32, (BLOCK, BLOCK), 1)
    bias_prev = jnp.where((kj >= qi) & (n > 0), 0.0, NEG_INF).astype(F32)
    bias_next = jnp.where((kj <= qi) & (n < nb - 1), 0.0, NEG_INF).astype(F32)
    _attention_core(q_ref, k_band, v_band, {0: bias_prev, 2: bias_next}, sink_ref, fa_index, o_ref)


def _attn_ctx_kernel(fa_index, sink_ref, q_ref, kx_ref, vx_ref, o_ref):
    _attention_core(q_ref, kx_ref[...], vx_ref[...], {}, sink_ref, fa_index, o_ref)


def _attention_lat(qkv, qkv_ctx, sink, fa_index):
    s = qkv.shape[0]
    n_ctx = qkv_ctx.shape[0]
    nb = s // BLOCK
    fw = LANES
    kcol = ATT_WIDTH // fw
    vcol = kcol + 1
    prev = lambda n: jnp.maximum(n - 1, 0)
    nxt = lambda n: jnp.minimum(n + 1, nb - 1)
    in_specs = [
        pl.BlockSpec(memory_space=pltpu.SMEM),
        pl.BlockSpec((BLOCK, ATT_WIDTH), lambda n: (n, 0)),
        pl.BlockSpec((BLOCK, fw), lambda n: (prev(n), kcol)),
        pl.BlockSpec((BLOCK, fw), lambda n: (n, kcol)),
        pl.BlockSpec((BLOCK, fw), lambda n: (nxt(n), kcol)),
        pl.BlockSpec((BLOCK, fw), lambda n: (prev(n), vcol)),
        pl.BlockSpec((BLOCK, fw), lambda n: (n, vcol)),
        pl.BlockSpec((BLOCK, fw), lambda n: (nxt(n), vcol)),
        pl.BlockSpec((n_ctx, fw), lambda n: (0, kcol)),
        pl.BlockSpec((n_ctx, fw), lambda n: (0, vcol)),
    ]
    return pl.pallas_call(
        functools.partial(_attn_lat_kernel, fa_index),
        grid=(nb,),
        in_specs=in_specs,
        out_specs=pl.BlockSpec((BLOCK, ATT_WIDTH), lambda n: (n, 0)),
        out_shape=jax.ShapeDtypeStruct((s, ATT_WIDTH), BF16),
        compiler_params=_params("arbitrary"),
        name="attention_latent",
    )(sink, qkv, qkv, qkv, qkv, qkv, qkv, qkv, qkv_ctx, qkv_ctx)


def _attention_ctx(qkv_ctx, sink, fa_index):
    n_ctx = qkv_ctx.shape[0]
    fw = LANES
    kcol = ATT_WIDTH // fw
    return pl.pallas_call(
        functools.partial(_attn_ctx_kernel, fa_index),
        grid=(1,),
        in_specs=[
            pl.BlockSpec(memory_space=pltpu.SMEM),
            pl.BlockSpec((n_ctx, ATT_WIDTH), lambda n: (0, 0)),
            pl.BlockSpec((n_ctx, fw), lambda n: (0, kcol)),
            pl.BlockSpec((n_ctx, fw), lambda n: (0, kcol + 1)),
        ],
        out_specs=pl.BlockSpec((n_ctx, ATT_WIDTH), lambda n: (0, 0)),
        out_shape=jax.ShapeDtypeStruct((n_ctx, ATT_WIDTH), BF16),
        compiler_params=_params("arbitrary"),
        name="attention_context",
    )(sink, qkv_ctx, qkv_ctx, qkv_ctx)


FFT_N1 = 64
FFT_N2 = 128


def _dft_cos_sin(n):
    idx = np.arange(n)
    ang = 2.0 * np.pi * ((idx[:, None] * idx[None, :]) % n) / n
    return np.cos(ang), np.sin(ang)


def _fourier_constants():
    c1, s1 = _dft_cos_sin(FFT_N1)
    stage1 = np.concatenate([c1, -s1], axis=0)
    c2, s2 = _dft_cos_sin(FFT_N2)
    stage2 = np.block([[c2, s2], [-s2, c2]])
    cc, sc = _dft_cos_sin(FG_W)
    chan = np.concatenate([cc, sc], axis=0)
    return (jnp.asarray(stage1, F32), jnp.asarray(stage2, F32), jnp.asarray(chan, F32))


def _twiddle_tables():
    k1 = jnp.arange(FFT_N1, dtype=jnp.int32)[:, None]
    n2 = jnp.arange(FFT_N2, dtype=jnp.int32)[None, :]
    ang = ((k1 * n2) % (FFT_N1 * FFT_N2)).astype(F32) * (2.0 * math.pi / (FFT_N1 * FFT_N2))
    wr = jnp.repeat(jnp.cos(ang), LANES, axis=1)
    wi = jnp.repeat(-jnp.sin(ang), LANES, axis=1)
    return wr, wi


def _fourier_stage1_kernel(x_ref, m_ref, wr_ref, wi_ref, t_ref):
    y = _dot(m_ref[...].astype(BF16), x_ref[...].astype(BF16))
    reps = F_WIDTH // LANES
    for b in range(x_ref.shape[1] // F_WIDTH):
        cols = slice(b * F_WIDTH, (b + 1) * F_WIDTH)
        yr = y[:FFT_N1, cols]
        yi = y[FFT_N1:, cols]
        wr = jnp.tile(wr_ref[:, b * LANES:(b + 1) * LANES], (1, reps))
        wi = jnp.tile(wi_ref[:, b * LANES:(b + 1) * LANES], (1, reps))
        t_ref[:, 2 * b * F_WIDTH:(2 * b + 1) * F_WIDTH] = (yr * wr - yi * wi).astype(t_ref.dtype)
        t_ref[:, (2 * b + 1) * F_WIDTH:(2 * b + 2) * F_WIDTH] = (yr * wi + yi * wr).astype(t_ref.dtype)


def _channel_stage(p, chan_ref, scale, o_ref):
    r = p.shape[0] // 2
    pr = p[:r].astype(BF16)
    pi = p[r:].astype(BF16)
    chan_c = chan_ref[:FG_W, :].astype(BF16)
    chan_s = chan_ref[FG_W:, :].astype(BF16)
    for g in range(N_FG):
        cols = slice(g * FG_W, (g + 1) * FG_W)
        y = _dot(pr[:, cols], chan_c) + _dot(pi[:, cols], chan_s)
        o_ref[:, cols] = (y * scale).astype(o_ref.dtype)


def _fourier_stage2_kernel(scale, t_ref, m_ref, chan_ref, o_ref):
    m = m_ref[...].astype(BF16)
    for kk in range(t_ref.shape[0]):
        t = jnp.concatenate([t_ref[kk, :, :F_WIDTH], t_ref[kk, :, F_WIDTH:]], axis=0)
        _channel_stage(_dot(m, t), chan_ref, scale, o_ref.at[:, kk * F_WIDTH:(kk + 1) * F_WIDTH])


def _fourier_ctx_kernel(scale, x_ref, m_ref, chan_ref, o_ref):
    _channel_stage(_dot(m_ref[...].astype(BF16), x_ref[...].astype(BF16)), chan_ref, scale, o_ref)


def _fourier_lat(f, consts, twiddles):
    n = f.shape[0]
    assert n == FFT_N1 * FFT_N2
    stage1, stage2, chan = consts
    wr, wi = twiddles
    n2_blk = 8
    cols = n2_blk * F_WIDTH
    full = lambda a: pl.BlockSpec(a.shape, lambda j: (0,) * a.ndim)
    t = pl.pallas_call(
        _fourier_stage1_kernel,
        grid=(FFT_N2 // n2_blk,),
        in_specs=[
            pl.BlockSpec((FFT_N1, cols), lambda j: (0, j)),
            full(stage1),
            pl.BlockSpec((FFT_N1, n2_blk * LANES), lambda j: (0, j)),
            pl.BlockSpec((FFT_N1, n2_blk * LANES), lambda j: (0, j)),
        ],
        out_specs=pl.BlockSpec((FFT_N1, 2 * cols), lambda j: (0, j)),
        out_shape=jax.ShapeDtypeStruct((FFT_N1, FFT_N2 * 2 * F_WIDTH), BF16),
        compiler_params=_params("arbitrary"),
        name="fourier_stage1",
    )(f.reshape(FFT_N1, FFT_N2 * F_WIDTH), stage1, wr, wi)
    t = t.reshape(FFT_N1, FFT_N2, 2 * F_WIDTH)
    scale = 1.0 / math.sqrt(n * FG_W)
    k1_blk = 4
    out = pl.pallas_call(
        functools.partial(_fourier_stage2_kernel, scale),
        grid=(FFT_N1 // k1_blk,),
        in_specs=[
            pl.BlockSpec((k1_blk, FFT_N2, 2 * F_WIDTH), lambda k1: (k1, 0, 0)),
            full(stage2),
            full(chan),
        ],
        out_specs=pl.BlockSpec((FFT_N2, k1_blk * F_WIDTH), lambda k1: (0, k1)),
        out_shape=jax.ShapeDtypeStruct((FFT_N2, FFT_N1 * F_WIDTH), BF16),
        compiler_params=_params("arbitrary"),
        name="fourier_stage2",
    )(t, stage2, chan)
    return out.reshape(n, F_WIDTH)


def _fourier_ctx(f, chan):
    n = f.shape[0]
    c, s = _dft_cos_sin(n)
    m = jnp.asarray(np.concatenate([c, -s], axis=0), F32)
    full = lambda a: pl.BlockSpec(a.shape, lambda j: (0,) * a.ndim)
    return pl.pallas_call(
        functools.partial(_fourier_ctx_kernel, 1.0 / math.sqrt(n * FG_W)),
        grid=(1,),
        in_specs=[full(f), full(m), full(chan)],
        out_specs=pl.BlockSpec((n, F_WIDTH), lambda j: (0, 0)),
        out_shape=jax.ShapeDtypeStruct((n, F_WIDTH), BF16),
        compiler_params=_params("arbitrary"),
        name="fourier_context",
    )(f, m, chan)


def _scan8(a, b, row, reverse):
    for d in (1, 2, 4):
        shift = (SUBLANES - d) if reverse else d
        keep = (row < SUBLANES - d) if reverse else (row >= d)
        b = jnp.where(keep, b + a * pltpu.roll(b, shift, axis=0), b)
        a = jnp.where(keep, a * pltpu.roll(a, shift, axis=0), a)
    return a, b


def _rows(x, n):
    return jnp.broadcast_to(x, (n, x.shape[1]))


def _rglru_kernel(reverse, combine, xs_ref, wa_ref, wi_ref, ba_ref, bi_ref, lam_ref, h0_ref, *refs):
    if combine:
        hb_ref, gg_ref, o_ref, last_ref, w_scr, a_scr, b_scr, c_scr, carry_scr = refs
    else:
        o_ref, last_ref, w_scr, a_scr, b_scr, c_scr, carry_scr = refs
    t = pl.program_id(1)
    tt, c = xs_ref.shape
    n_slab = c // LANES
    slabs_per_block = RNN_BLOCK // LANES
    groups = tt // SUBLANES
    blocks = groups // SUBLANES
    row = lax.broadcasted_iota(jnp.int32, (SUBLANES, LANES), 0)
    edge = 0 if reverse else SUBLANES - 1

    @pl.when(t == 0)
    def _():
        carry_scr[...] = jnp.broadcast_to(h0_ref[...], carry_scr.shape)
        w_scr[:, :, :RNN_BLOCK] = (0.5 * wa_ref[...]).astype(BF16)
        w_scr[:, :, RNN_BLOCK:] = (0.5 * wi_ref[...]).astype(BF16)

    pre = [_dot(xs_ref[:, nb * RNN_BLOCK:(nb + 1) * RNN_BLOCK].astype(BF16), w_scr[nb])
           for nb in range(c // RNN_BLOCK)]
    neg_lam = -lam_ref[...]
    softplus = jnp.maximum(neg_lam, 0.0) + jnp.log1p(jnp.exp(-jnp.abs(neg_lam)))
    k_all = softplus * (-0.5 * LRU_C * math.log2(math.e))
    ba_all = 0.5 * ba_ref[...]
    bi_all = 0.5 * bi_ref[...]

    for s in range(n_slab):
        lanes = slice(s * LANES, (s + 1) * LANES)
        k = _rows(k_all[:, lanes], SUBLANES)
        ba = _rows(ba_all[:, lanes], SUBLANES)
        bi = _rows(bi_all[:, lanes], SUBLANES)
        pre_b = pre[s // slabs_per_block]
        l0 = (s % slabs_per_block) * LANES
        for g in range(groups):
            r0 = g * SUBLANES
            tr = jnp.tanh(pre_b[r0:r0 + SUBLANES, l0:l0 + LANES] + ba)
            ti = jnp.tanh(pre_b[r0:r0 + SUBLANES, RNN_BLOCK + l0:RNN_BLOCK + l0 + LANES] + bi)
            a = jnp.exp2((1.0 + tr) * k)
            y = 1.0 - a * a
            root = jnp.where(y > 0.0, y * lax.rsqrt(y), 0.0)
            b = root * ((1.0 + ti) * (0.5 * xs_ref[r0:r0 + SUBLANES, lanes]))
            a_cum, b_loc = _scan8(a, b, row, reverse)
            a_scr[s, r0:r0 + SUBLANES, :] = a_cum
            b_scr[s, r0:r0 + SUBLANES, :] = b_loc

    for s in range(n_slab):
        a2 = a_scr[s, pl.ds(edge, groups, stride=SUBLANES), :]
        b2 = b_scr[s, pl.ds(edge, groups, stride=SUBLANES), :]
        carry = carry_scr[:, s * LANES:(s + 1) * LANES]
        enter_row = groups if reverse else SUBLANES - 1
        base = 0 if reverse else SUBLANES
        c_scr[s, enter_row:enter_row + 1, :] = carry[0:1, :]
        for j in (range(blocks - 1, -1, -1) if reverse else range(blocks)):
            r0 = j * SUBLANES
            a_cum, b_loc = _scan8(a2[r0:r0 + SUBLANES, :], b2[r0:r0 + SUBLANES, :], row, reverse)
            st = b_loc + a_cum * carry
            c_scr[s, base + r0:base + r0 + SUBLANES, :] = st
            carry = _rows(st[edge:edge + 1, :], SUBLANES)
        carry_scr[:, s * LANES:(s + 1) * LANES] = carry
        last_ref[:, s * LANES:(s + 1) * LANES] = carry[0:1, :]

    pair = 2 * SUBLANES
    for s in range(n_slab):
        lanes = slice(s * LANES, (s + 1) * LANES)
        for g in range(0, groups, 2):
            r0 = g * SUBLANES
            src = g + 1 if reverse else g + SUBLANES - 1
            enter = jnp.concatenate([_rows(c_scr[s, src:src + 1, :], SUBLANES),
                                     _rows(c_scr[s, src + 1:src + 2, :], SUBLANES)], axis=0)
            h = b_scr[s, r0:r0 + pair, :] + a_scr[s, r0:r0 + pair, :] * enter
            if combine:
                h = (h + hb_ref[r0:r0 + pair, lanes]) * gg_ref[r0:r0 + pair, lanes].astype(F32)
            o_ref[r0:r0 + pair, lanes] = h.astype(o_ref.dtype)


def _rglru_scan(xs, w_a, w_i, b_a, b_i, lam, rg_index, direction, h0, h_other=None, gelu_gate=None):
    m = xs.shape[0]
    tt = min(m, 1024)
    nt = m // tt
    c = SCAN_BLOCKS * RNN_BLOCK
    reverse = direction == 1
    combine = h_other is not None
    tix = (lambda t: nt - 1 - t) if reverse else (lambda t: t)
    wspec = pl.BlockSpec((None, None, SCAN_BLOCKS, RNN_BLOCK, RNN_BLOCK),
                         lambda cb, t: (rg_index, direction, cb, 0, 0))
    vspec = pl.BlockSpec((None, None, 1, c), lambda cb, t: (rg_index, direction, 0, cb))
    tile = pl.BlockSpec((tt, c), lambda cb, t: (tix(t), cb))
    in_specs = [tile, wspec, wspec, vspec, vspec, vspec, pl.BlockSpec((1, c), lambda cb, t: (0, cb))]
    args = [xs, w_a, w_i, b_a, b_i, lam, h0]
    if combine:
        in_specs += [tile, tile]
        args += [h_other, gelu_gate]
    return pl.pallas_call(
        functools.partial(_rglru_kernel, reverse, combine),
        grid=(D_RNN // c, nt),
        in_specs=in_specs,
        out_specs=[tile, pl.BlockSpec((1, c), lambda cb, t: (0, cb))],
        out_shape=[jax.ShapeDtypeStruct((m, D_RNN), BF16 if combine else F32),
                   jax.ShapeDtypeStruct((1, D_RNN), F32)],
        scratch_shapes=[pltpu.VMEM((SCAN_BLOCKS, RNN_BLOCK, 2 * RNN_BLOCK), BF16),
                        pltpu.VMEM((c // LANES, tt, LANES), F32),
                        pltpu.VMEM((c // LANES, tt, LANES), F32),
                        pltpu.VMEM((c // LANES, tt // SUBLANES + 2 * SUBLANES, LANES), F32),
                        pltpu.VMEM((SUBLANES, c), F32)],
        compiler_params=_params("arbitrary", "arbitrary"),
        name="rglru_scan",
    )(*args)


def _final_norm_kernel(x_ref, g_ref, o_ref):
    g = g_ref[...]
    for r0 in range(0, x_ref.shape[0], NORM_ROWS):
        x = x_ref[r0:r0 + NORM_ROWS, :]
        r = lax.rsqrt(jnp.mean(x * x, axis=-1, keepdims=True) + EPS)
        o_ref[r0:r0 + NORM_ROWS, :] = (x * r) * g


def _final_norm(x, g):
    m, d = x.shape
    tm = 512
    return pl.pallas_call(
        _final_norm_kernel,
        grid=(m // tm,),
        in_specs=[pl.BlockSpec((tm, d), lambda i: (i, 0)), pl.BlockSpec((1, d), lambda i: (0, 0))],
        out_specs=pl.BlockSpec((tm, d), lambda i: (i, 0)),
        out_shape=jax.ShapeDtypeStruct((m, d), F32),
        compiler_params=_params("arbitrary"),
        name="final_norm",
    )(x, g.reshape(1, d))


def _rope_tables(n):
    f = HEAD_DIM // 4
    inv = ROPE_BASE ** (-jnp.arange(f, dtype=F32) / f)
    pos = jnp.arange(n, dtype=jnp.int32)
    ang_r = (pos // GRID_W).astype(F32)[:, None] * inv[None, :]
    ang_c = (pos % GRID_W).astype(F32)[:, None] * inv[None, :]
    cr, sr, cc, sc = jnp.cos(ang_r), jnp.sin(ang_r), jnp.cos(ang_c), jnp.sin(ang_c)
    cos = jnp.concatenate([cr, cr, cc, cc], axis=1)
    sin = jnp.concatenate([-sr, sr, -sc, sc], axis=1)
    reps = LANES // HEAD_DIM
    return jnp.tile(cos, (1, reps)), jnp.tile(sin, (1, reps))


def _mixer_out(parts, w, w_index, x_lat, layer, mods, parts_ctx, x_ctx, g_ffn):
    x_lat, h_lat = _proj_residual_norm(parts, w, w_index, x_lat, layer, mods, 0, 2, g_ffn, 3, 4)
    h_ctx = None
    if parts_ctx is None:
        x_ctx = None
    else:
        x_ctx, h_ctx = _proj_residual_norm(parts_ctx, w, w_index, x_ctx, layer, mods, 1, 2, g_ffn, 3, 4)
    return x_lat, x_ctx, h_lat, h_ctx


def _fourier_attn_layer(layer, i, x_lat, x_ctx, h_lat, h_ctx, mods, fa_w_in, fa_w_out, attn_sink,
                        tables, ctx_out, g_ffn):
    rope, consts, twiddles = tables
    qkv_w = ATT_WIDTH + 2 * KV_WIDTH
    f_lat = _proj("plain", h_lat, fa_w_in, i, 0, F_WIDTH, F_WIDTH, F_WIDTH, BF16)
    qkv_lat = _proj("qkv_rope", h_lat, fa_w_in, i, F_WIDTH, qkv_w, qkv_w, 256, BF16, tm=512, extra=rope)
    qkv_ctx = _proj("qkv", h_ctx, fa_w_in, i, F_WIDTH, qkv_w, qkv_w, 256, BF16)
    fo_lat = _fourier_lat(f_lat, consts, twiddles)
    ao_lat = _attention_lat(qkv_lat, qkv_ctx, attn_sink, i)
    parts_ctx = None
    if ctx_out:
        f_ctx = _proj("plain", h_ctx, fa_w_in, i, 0, F_WIDTH, F_WIDTH, F_WIDTH, BF16)
        parts_ctx = [_fourier_ctx(f_ctx, consts[2]), _attention_ctx(qkv_ctx, attn_sink, i)]
    return _mixer_out([fo_lat, ao_lat], fa_w_out, i, x_lat, layer, mods, parts_ctx, x_ctx, g_ffn)


def _rglru_layer(layer, i, x_lat, x_ctx, h_lat, h_ctx, mods, rg_w_in, rg_conv_w, rg_conv_b, w_a, b_a,
                 w_i, b_i, lam, rg_w_out, ctx_out, g_ffn):
    tn = 1024
    xcol = D_RNN // tn
    zero_state = jnp.zeros((1, D_RNN), F32)
    gate_lat = _proj("gelu", h_lat, rg_w_in, i, 0, D_RNN, tn, tn, BF16)
    conv = functools.partial(_conv_proj, w=rg_w_in, w_index=i, col_blocks=[xcol], conv_blocks=[0], n_out=D_RNN,
                             cw=rg_conv_w, cb=rg_conv_b, kw=CONV_W, left=CONV_LEFT, gated=False, tm=1024, tn=tn,
                             out_dtype=F32)
    xs_lat = conv(h_lat)
    xs_ctx = conv(h_ctx)
    scan = functools.partial(_rglru_scan, w_a=w_a, w_i=w_i, b_a=b_a, b_i=b_i, lam=lam, rg_index=i)
    hb_ctx, s_bwd = scan(xs_ctx, direction=1, h0=zero_state)
    parts_ctx = None
    if ctx_out:
        gate_ctx = _proj("gelu", h_ctx, rg_w_in, i, 0, D_RNN, tn, tn, BF16)
        y_ctx, s_fwd = scan(xs_ctx, direction=0, h0=zero_state, h_other=hb_ctx, gelu_gate=gate_ctx)
        parts_ctx = [y_ctx]
    else:
        _, s_fwd = scan(xs_ctx, direction=0, h0=zero_state)
    hb_lat, _ = scan(xs_lat, direction=1, h0=s_bwd)
    y_lat, _ = scan(xs_lat, direction=0, h0=s_fwd, h_other=hb_lat, gelu_gate=gate_lat)
    return _mixer_out([y_lat], rg_w_out, i, x_lat, layer, mods, parts_ctx, x_ctx, g_ffn)


def _conv_ffn(layer, x_lat, x_ctx, h_lat, h_ctx, mods, w_up, conv_w, conv_b, w_down):
    tf = 512
    blocks = [0, D_FF // tf]
    conv = functools.partial(_conv_proj, w=w_up, w_index=layer, col_blocks=blocks, conv_blocks=blocks, n_out=D_FF,
                             cw=conv_w, cb=conv_b, kw=FFN_CONV_W, left=FFN_CONV_LEFT, gated=True, tm=1024, tn=tf,
                             out_dtype=BF16)
    act_lat = conv(h_lat)
    parts_ctx = None if x_ctx is None else [conv(h_ctx)]
    return _proj_residual([act_lat], w_down, layer, x_lat, layer, mods, 5, 512, 512, parts_ctx, x_ctx)


def kernel(x, c, ctx, c_ctx, w_mod, b_mod, g_mix, g_ffn, fa_w_in, fa_w_out, attn_sink, rg_w_in, rg_conv_w,
           rg_conv_b, rg_w_a, rg_b_a, rg_w_i, rg_b_i, rg_lambda, rg_w_out, ffn_w_up, ffn_conv_w, ffn_conv_b,
           ffn_w_down, g_final):
    assert x.shape[0] == 1 and ctx.shape[0] == 1
    n = x.shape[1]
    x_lat = x[0]
    x_ctx = ctx[0]
    mods = _modulation(c, c_ctx, w_mod, b_mod)
    tables = (_rope_tables(n), _fourier_constants(), _twiddle_tables())
    g_mix3 = g_mix.reshape(DEPTH, 1, D_MODEL)
    g_ffn3 = g_ffn.reshape(DEPTH, 1, D_MODEL)
    n_rg = rg_conv_b.shape[0]
    rg_conv_b3 = rg_conv_b.reshape(n_rg, 1, D_RNN)
    rg_b_a4 = rg_b_a.reshape(n_rg, 2, 1, D_RNN)
    rg_b_i4 = rg_b_i.reshape(n_rg, 2, 1, D_RNN)
    rg_lam4 = rg_lambda.reshape(n_rg, 2, 1, D_RNN)
    ffn_conv_b3 = ffn_conv_b.reshape(DEPTH, 1, 2 * D_FF)
    for layer in range(DEPTH):
        ctx_out = layer < DEPTH - 1
        i = layer // 2
        h_lat = _norm_mod(x_lat, g_mix3, layer, mods, 0, 0, 1)
        h_ctx = _norm_mod(x_ctx, g_mix3, layer, mods, 1, 0, 1)
        if layer % 2 == 0:
            mixed = _fourier_attn_layer(layer, i, x_lat, x_ctx, h_lat, h_ctx, mods, fa_w_in, fa_w_out,
                                        attn_sink, tables, ctx_out, g_ffn3)
        else:
            mixed = _rglru_layer(layer, i, x_lat, x_ctx, h_lat, h_ctx, mods, rg_w_in, rg_conv_w, rg_conv_b3,
                                 rg_w_a, rg_b_a4, rg_w_i, rg_b_i4, rg_lam4, rg_w_out, ctx_out, g_ffn3)
        x_lat, x_ctx, hf_lat, hf_ctx = mixed
        x_lat, x_ctx = _conv_ffn(layer, x_lat, x_ctx if ctx_out else None, hf_lat, hf_ctx, mods,
                                 ffn_w_up, ffn_conv_w, ffn_conv_b3, ffn_w_down)
    return _final_norm(x_lat, g_final)[None]
```
